```python
import math
import jax, jax.numpy as jnp
from jax import lax
import numpy as np

D_MODEL = 2048
BATCH = 4
SEQ = 2048
DEPTH = 2

ROPE_THETA = 10000.0
BLOCK_Q = 128
D_FF = 5632
DN_ALPHA = (2 * DEPTH) ** 0.25
DN_BETA = (8 * DEPTH) ** (-0.25)
LN_EPS = 1e-5
RMS_EPS = 1e-6
NEG = -1e30
FORCE_SCORE = 1e9
N_EVEN = (DEPTH + 1) // 2
N_ODD = DEPTH // 2

MLA_HEADS = 8
MLA_NOPE = 128
MLA_ROPE = 64
MLA_V = 128
MLA_Q_RANK = 512
MLA_KV_RANK = 512

SWA_HEADS = 16
SWA_KV_HEADS = 2
SWA_HEAD_DIM = 64
SWA_WINDOW = 128

EVEN_IN = MLA_Q_RANK + MLA_KV_RANK + MLA_ROPE + (SWA_HEADS + 2 * SWA_KV_HEADS) * SWA_HEAD_DIM
EVEN_OUT = MLA_HEADS * MLA_V + SWA_HEADS * SWA_HEAD_DIM

NSA_HEADS = 16
NSA_KV_GROUPS = 2
NSA_HEAD_DIM = 128
CMP_BLOCK = 32
CMP_STRIDE = 16
CMP_HIDDEN = 256
SLC_BLOCK = 64
N_SEL = 8
NSA_WINDOW = 512

ODD_IN = NSA_HEADS * NSA_HEAD_DIM + 6 * NSA_KV_GROUPS * NSA_HEAD_DIM + 3 * NSA_HEADS
ODD_OUT = NSA_HEADS * NSA_HEAD_DIM

kernel_name = "hybrid_mla_swa_nsa_macaron_deepnorm"

F32 = jnp.float32


def _offsets(sizes):
    return [int(v) for v in np.cumsum(sizes)]


def layer_norm(x, g, b):
    xf = x.astype(F32)
    mu = xf.mean(-1, keepdims=True)
    var = jnp.square(xf - mu).mean(-1, keepdims=True)
    return ((xf - mu) * lax.rsqrt(var + LN_EPS) * g.astype(F32) + b.astype(F32)).astype(x.dtype)


def rms_norm(x, g):
    xf = x.astype(F32)
    return (xf * lax.rsqrt(jnp.mean(xf * xf, -1, keepdims=True) + RMS_EPS) * g.astype(F32)).astype(x.dtype)


def swiglu(x, wg, wu, wd):
    return (jax.nn.silu(x @ wg) * (x @ wu)) @ wd


def rope_tables(seq, dim):
    inv = 1.0 / (ROPE_THETA ** (jnp.arange(0, dim, 2, dtype=F32) / dim))
    ang = jnp.arange(seq, dtype=F32)[:, None] * inv[None, :]
    return jnp.cos(ang), jnp.sin(ang)


def apply_rope(x, cos, sin):
    x1, x2 = jnp.split(x, 2, axis=-1)
    c = cos[None, :, None, :].astype(x.dtype)
    s = sin[None, :, None, :].astype(x.dtype)
    return jnp.concatenate([x1 * c - x2 * s, x1 * s + x2 * c], axis=-1)


def causal_block_attention(q, k, v, scale):
    B, S, H, dq = q.shape
    nb = S // BLOCK_Q
    qb = q.reshape(B, nb, BLOCK_Q, H, dq).transpose(1, 0, 2, 3, 4)
    kpos = jnp.arange(S)

    def one(args):
        i, qi = args
        s = jnp.einsum('bqhd,bkhd->bhqk', qi, k).astype(F32) * scale
        qpos = i * BLOCK_Q + jnp.arange(BLOCK_Q)
        mask = kpos[None, :] <= qpos[:, None]
        p = jax.nn.softmax(jnp.where(mask, s, NEG), axis=-1)
        return jnp.einsum('bhqk,bkhd->bqhd', p.astype(v.dtype), v)

    out = lax.map(one, (jnp.arange(nb), qb))
    return out.transpose(1, 0, 2, 3, 4).reshape(B, S, H, v.shape[-1])


def banded_attention(q, k, v, window, scale, sinks=None):
    B, S, G, R, d = q.shape
    nb = S // BLOCK_Q
    pad = -(-(window - 1) // BLOCK_Q) * BLOCK_Q
    span = pad + BLOCK_Q
    kp = jnp.pad(k, ((0, 0), (pad, 0), (0, 0), (0, 0)))
    vp = jnp.pad(v, ((0, 0), (pad, 0), (0, 0), (0, 0)))
    idx = (np.arange(nb) * BLOCK_Q)[:, None] + np.arange(span)[None, :]
    kb = kp[:, idx]
    vb = vp[:, idx]
    qb = q.reshape(B, nb, BLOCK_Q, G, R, d)
    s = jnp.einsum('bnqgrd,bnkgd->bngrqk', qb, kb).astype(F32) * scale
    qpos = (np.arange(nb) * BLOCK_Q)[:, None] + np.arange(BLOCK_Q)[None, :]
    kpos = idx - pad
    dist = qpos[:, :, None] - kpos[:, None, :]
    mask = jnp.asarray((dist >= 0) & (dist < window) & (kpos[:, None, :] >= 0))
    s = jnp.where(mask[None, :, None, None], s, NEG)
    if sinks is not None:
        sl = jnp.broadcast_to(sinks.reshape(G, R)[None, None, :, :, None, None].astype(F32), s.shape[:-1] + (1,))
        p = jax.nn.softmax(jnp.concatenate([s, sl], axis=-1), axis=-1)[..., :-1]
    else:
        p = jax.nn.softmax(s, axis=-1)
    o = jnp.einsum('bngrqk,bnkgd->bnqgrd', p.astype(v.dtype), vb)
    return o.reshape(B, S, G, R, d)


def compress_tokens(x, pos_emb, w1, w2):
    B, S, G, d = x.shape
    nc = (S - CMP_BLOCK) // CMP_STRIDE + 1
    idx = (np.arange(nc) * CMP_STRIDE)[:, None] + np.arange(CMP_BLOCK)[None, :]
    xb = x[:, idx] + pos_emb[:, None, :].astype(x.dtype)
    xb = xb.transpose(0, 1, 3, 2, 4).reshape(B, nc, G, CMP_BLOCK * d)
    return jax.nn.gelu(xb @ w1) @ w2


def selected_block_attention(q, k, v, blk_idx, blk_valid, scale):
    B, S, G, R, d = q.shape
    nb = S // BLOCK_Q
    n_sel = blk_idx.shape[-1]
    kT = k.transpose(0, 2, 1, 3)
    vT = v.transpose(0, 2, 1, 3)
    gather = jax.vmap(jax.vmap(lambda a, i: a[i]))
    qb = q.reshape(B, nb, BLOCK_Q, G, R, d).transpose(1, 0, 2, 3, 4, 5)

    def to_blocks(a):
        return a.reshape(B, G, nb, BLOCK_Q, n_sel).transpose(2, 0, 1, 3, 4)

    offs = jnp.arange(SLC_BLOCK)

    def one(args):
        i, qi, bi, vi = args
        tok = (bi[..., None] * SLC_BLOCK + offs).reshape(B, G, BLOCK_Q, n_sel * SLC_BLOCK)
        qpos = i * BLOCK_Q + jnp.arange(BLOCK_Q)
        m = jnp.repeat(vi, SLC_BLOCK, axis=-1) & (tok <= qpos[:, None])
        kg = gather(kT, tok)
        vg = gather(vT, tok)
        s = jnp.einsum('bqgrd,bgqnd->bgrqn', qi, kg).astype(F32) * scale
        m = m[:, :, None]
        p = jax.nn.softmax(jnp.where(m, s, NEG), axis=-1) * m
        return jnp.einsum('bgrqn,bgqnd->bqgrd', p.astype(v.dtype), vg)

    out = lax.map(one, (jnp.arange(nb), qb, to_blocks(blk_idx), to_blocks(blk_valid)))
    return out.transpose(1, 0, 2, 3, 4, 5).reshape(B, S, G, R, d)


def even_mixer(h, w_in, q_norm, w_uq, kv_norm, w_ukv, sinks, w_out):
    B, S, _ = h.shape
    z = h @ w_in
    c_q, c_kv, k_pe, q_s, k_s, v_s = jnp.split(z, _offsets([MLA_Q_RANK, MLA_KV_RANK, MLA_ROPE, SWA_HEADS * SWA_HEAD_DIM, SWA_KV_HEADS * SWA_HEAD_DIM]), axis=-1)
    cos_r, sin_r = rope_tables(S, MLA_ROPE)
    q = (rms_norm(c_q, q_norm) @ w_uq).reshape(B, S, MLA_HEADS, MLA_NOPE + MLA_ROPE)
    q = jnp.concatenate([q[..., :MLA_NOPE], apply_rope(q[..., MLA_NOPE:], cos_r, sin_r)], axis=-1)
    kv = (rms_norm(c_kv, kv_norm) @ w_ukv).reshape(B, S, MLA_HEADS, MLA_NOPE + MLA_V)
    k_pe = apply_rope(k_pe.reshape(B, S, 1, MLA_ROPE), cos_r, sin_r)
    k = jnp.concatenate([kv[..., :MLA_NOPE], jnp.broadcast_to(k_pe, (B, S, MLA_HEADS, MLA_ROPE))], axis=-1)
    o_mla = causal_block_attention(q, k, kv[..., MLA_NOPE:], (MLA_NOPE + MLA_ROPE) ** -0.5)
    R = SWA_HEADS // SWA_KV_HEADS
    cos_s, sin_s = rope_tables(S, SWA_HEAD_DIM)
    qs = apply_rope(q_s.reshape(B, S, SWA_HEADS, SWA_HEAD_DIM), cos_s, sin_s).reshape(B, S, SWA_KV_HEADS, R, SWA_HEAD_DIM)
    ks = apply_rope(k_s.reshape(B, S, SWA_KV_HEADS, SWA_HEAD_DIM), cos_s, sin_s)
    vs = v_s.reshape(B, S, SWA_KV_HEADS, SWA_HEAD_DIM)
    o_swa = banded_attention(qs, ks, vs, SWA_WINDOW, SWA_HEAD_DIM ** -0.5, sinks)
    o = jnp.concatenate([o_mla.reshape(B, S, -1), o_swa.reshape(B, S, -1)], axis=-1)
    return o @ w_out


def odd_mixer(h, w_in, gate_b, cmp_pos_k, cmp_pos_v, cmp_k_w1, cmp_k_w2, cmp_v_w1, cmp_v_w2, w_out):
    B, S, _ = h.shape
    G, d = NSA_KV_GROUPS, NSA_HEAD_DIM
    R = NSA_HEADS // G
    kvw = G * d
    z = h @ w_in
    q, kc, vc, ks, vs, kw, vw, gl = jnp.split(z, _offsets([NSA_HEADS * d, kvw, kvw, kvw, kvw, kvw, kvw]), axis=-1)
    cos, sin = rope_tables(S, d)
    q = apply_rope(q.reshape(B, S, NSA_HEADS, d), cos, sin).reshape(B, S, G, R, d)
    kc = apply_rope(kc.reshape(B, S, G, d), cos, sin)
    ks = apply_rope(ks.reshape(B, S, G, d), cos, sin)
    kw = apply_rope(kw.reshape(B, S, G, d), cos, sin)
    vc = vc.reshape(B, S, G, d)
    vs = vs.reshape(B, S, G, d)
    vw = vw.reshape(B, S, G, d)
    scale = d ** -0.5
    t = jnp.arange(S)
    kcmp = compress_tokens(kc, cmp_pos_k, cmp_k_w1, cmp_k_w2)
    vcmp = compress_tokens(vc, cmp_pos_v, cmp_v_w1, cmp_v_w2)
    nc = kcmp.shape[1]
    cmp_end = jnp.arange(nc) * CMP_STRIDE + CMP_BLOCK - 1
    cmask = cmp_end[None, :] <= t[:, None]
    s = jnp.einsum('bsgrd,bcgd->bgrsc', q, kcmp).astype(F32) * scale
    p_cmp = jax.nn.softmax(jnp.where(cmask, s, NEG), axis=-1) * cmask
    o_cmp = jnp.einsum('bgrsc,bcgd->bsgrd', p_cmp.astype(h.dtype), vcmp)
    ns = S // SLC_BLOCK
    n_sel = min(N_SEL, ns)
    starts = np.arange(nc) * CMP_STRIDE
    jb = np.arange(ns)
    overlap = ((starts[:, None] < (jb[None, :] + 1) * SLC_BLOCK) & (starts[:, None] + CMP_BLOCK > jb[None, :] * SLC_BLOCK)).astype(np.float32)
    imp = jnp.einsum('bgrsc,cj->bgsj', p_cmp, jnp.asarray(overlap))
    blk = jnp.arange(ns)
    cur = t // SLC_BLOCK
    eligible = blk[None, :] <= cur[:, None]
    forced = (blk[None, :] == 0) | (blk[None, :] == cur[:, None]) | (blk[None, :] == cur[:, None] - 1)
    score = jnp.where(eligible, jnp.where(forced, FORCE_SCORE, imp), -1.0)
    top_val, top_idx = lax.top_k(score, n_sel)
    o_slc = selected_block_attention(q, ks, vs, top_idx, top_val >= 0.0, scale)
    o_win = banded_attention(q, kw, vw, NSA_WINDOW, scale)
    gates = jax.nn.sigmoid(gl + gate_b).reshape(B, S, G, R, 3)
    o = gates[..., 0:1] * o_cmp + gates[..., 1:2] * o_slc + gates[..., 2:3] * o_win
    return o.reshape(B, S, -1) @ w_out


def setup_inputs(seed: int = 0) -> dict:
    key = jax.random.key(seed)
    ks = jax.random.split(key, 24)
    nrm = lambda k, shape, scale: jax.random.normal(k, shape, F32) * scale
    return {
        "x": nrm(ks[0], (BATCH, SEQ, D_MODEL), 1.0),
        "ln_g": 1.0 + nrm(ks[1], (DEPTH, 3, D_MODEL), 0.02),
        "ln_b": nrm(ks[2], (DEPTH, 3, D_MODEL), 0.02),
        "ffn_w_gate": nrm(ks[3], (DEPTH, 2, D_MODEL, D_FF), D_MODEL ** -0.5),
        "ffn_w_up": nrm(ks[4], (DEPTH, 2, D_MODEL, D_FF), D_MODEL ** -0.5),
        "ffn_w_down": nrm(ks[5], (DEPTH, 2, D_FF, D_MODEL), DN_BETA * D_FF ** -0.5),
        "even_w_in": nrm(ks[6], (N_EVEN, D_MODEL, EVEN_IN), D_MODEL ** -0.5),
        "mla_q_norm": 1.0 + nrm(ks[7], (N_EVEN, MLA_Q_RANK), 0.02),
        "mla_w_uq": nrm(ks[8], (N_EVEN, MLA_Q_RANK, MLA_HEADS * (MLA_NOPE + MLA_ROPE)), MLA_Q_RANK ** -0.5),
        "mla_kv_norm": 1.0 + nrm(ks[9], (N_EVEN, MLA_KV_RANK), 0.02),
        "mla_w_ukv": nrm(ks[10], (N_EVEN, MLA_KV_RANK, MLA_HEADS * (MLA_NOPE + MLA_V)), MLA_KV_RANK ** -0.5),
        "swa_sinks": nrm(ks[11], (N_EVEN, SWA_HEADS), 0.5),
        "even_w_out": nrm(ks[12], (N_EVEN, EVEN_OUT, D_MODEL), DN_BETA * EVEN_OUT ** -0.5),
        "odd_w_in": nrm(ks[13], (N_ODD, D_MODEL, ODD_IN), D_MODEL ** -0.5),
        "nsa_gate_b": nrm(ks[14], (N_ODD, 3 * NSA_HEADS), 0.1),
        "nsa_cmp_pos_k": nrm(ks[15], (N_ODD, CMP_BLOCK, NSA_HEAD_DIM), 0.1),
        "nsa_cmp_pos_v": nrm(ks[16], (N_ODD, CMP_BLOCK, NSA_HEAD_DIM), 0.1),
        "nsa_cmp_k_w1": nrm(ks[17], (N_ODD, CMP_BLOCK * NSA_HEAD_DIM, CMP_HIDDEN), (CMP_BLOCK * NSA_HEAD_DIM) ** -0.5),
        "nsa_cmp_k_w2": nrm(ks[18], (N_ODD, CMP_HIDDEN, NSA_HEAD_DIM), CMP_HIDDEN ** -0.5),
        "nsa_cmp_v_w1": nrm(ks[19], (N_ODD, CMP_BLOCK * NSA_HEAD_DIM, CMP_HIDDEN), (CMP_BLOCK * NSA_HEAD_DIM) ** -0.5),
        "nsa_cmp_v_w2": nrm(ks[20], (N_ODD, CMP_HIDDEN, NSA_HEAD_DIM), CMP_HIDDEN ** -0.5),
        "odd_w_out": nrm(ks[21], (N_ODD, ODD_OUT, D_MODEL), DN_BETA * ODD_OUT ** -0.5),
    }


def reference(x, ln_g, ln_b, ffn_w_gate, ffn_w_up, ffn_w_down, even_w_in, mla_q_norm, mla_w_uq, mla_kv_norm, mla_w_ukv, swa_sinks, even_w_out, odd_w_in, nsa_gate_b, nsa_cmp_pos_k, nsa_cmp_pos_v, nsa_cmp_k_w1, nsa_cmp_k_w2, nsa_cmp_v_w1, nsa_cmp_v_w2, odd_w_out):
    for l in range(DEPTH):
        x = layer_norm(DN_ALPHA * x + 0.5 * swiglu(x, ffn_w_gate[l, 0], ffn_w_up[l, 0], ffn_w_down[l, 0]), ln_g[l, 0], ln_b[l, 0])
        j = l // 2
        if l % 2 == 0:
            mix = even_mixer(x, even_w_in[j], mla_q_norm[j], mla_w_uq[j], mla_kv_norm[j], mla_w_ukv[j], swa_sinks[j], even_w_out[j])
        else:
            mix = odd_mixer(x, odd_w_in[j], nsa_gate_b[j], nsa_cmp_pos_k[j], nsa_cmp_pos_v[j], nsa_cmp_k_w1[j], nsa_cmp_k_w2[j], nsa_cmp_v_w1[j], nsa_cmp_v_w2[j], odd_w_out[j])
        x = layer_norm(DN_ALPHA * x + mix, ln_g[l, 1], ln_b[l, 1])
        x = layer_norm(DN_ALPHA * x + 0.5 * swiglu(x, ffn_w_gate[l, 1], ffn_w_up[l, 1], ffn_w_down[l, 1]), ln_g[l, 2], ln_b[l, 2])
    return x
```

```python
import functools
import math

import numpy as np
import jax
import jax.numpy as jnp
from jax import lax
from jax.experimental import pallas as pl
from jax.experimental.pallas import tpu as pltpu

F32 = jnp.float32
BF = jnp.bfloat16

ROPE_THETA = 10000.0
LN_EPS = 1e-5
RMS_EPS = 1e-6
NEG = -1e30
FORCE_SCORE = 1e9

MLA_HEADS, MLA_NOPE, MLA_ROPE, MLA_V = 8, 128, 64, 128
MLA_Q_RANK, MLA_KV_RANK = 512, 512
SWA_HEADS, SWA_KV_HEADS, SWA_HEAD_DIM, SWA_WINDOW = 16, 2, 64, 128
NSA_HEADS, NSA_KV_GROUPS, NSA_HEAD_DIM = 16, 2, 128
CMP_BLOCK, CMP_STRIDE, CMP_HIDDEN = 32, 16, 256
SLC_BLOCK, N_SEL, NSA_WINDOW = 64, 8, 512

LANE = 128
VMEM_LIMIT_BYTES = 52 * 1024 * 1024


def _params(*sem):
    return pltpu.CompilerParams(dimension_semantics=sem, vmem_limit_bytes=VMEM_LIMIT_BYTES)


def _pick(n, candidates):
    for c in candidates:
        if n % c == 0:
            return c
    raise ValueError(f"no tile in {candidates} divides {n}")


def _proj_kernel(*refs, has_rms, has_rope, half):
    it = iter(refs)
    x_ref, w_ref = next(it), next(it)
    g_ref = next(it) if has_rms else None
    cos_ref, sin_ref = (next(it), next(it)) if has_rope else (None, None)
    o_ref = next(it)
    x = x_ref[...]
    if has_rms:
        xf = x.astype(F32)
        x = xf * lax.rsqrt(jnp.mean(xf * xf, -1, keepdims=True) + RMS_EPS) * g_ref[...]
    acc = jnp.dot(x.astype(BF), w_ref[...], preferred_element_type=F32)
    if has_rope:
        n = acc.shape[-1]
        nxt = pltpu.roll(acc, n - half, 1)
        prv = pltpu.roll(acc, half, 1)
        lane = lax.broadcasted_iota(jnp.int32, acc.shape, 1)
        partner = jnp.where(lane % (2 * half) < half, nxt, prv)
        acc = acc * cos_ref[...] + partner * sin_ref[...]
    o_ref[...] = acc.astype(o_ref.dtype)


def _proj(x, w, *, k_block=0, k_size=None, rms_g=None, rope=None, seq=None, out_dtype=F32):
    m = x.shape[0]
    k_size = x.shape[1] if k_size is None else k_size
    n = w.shape[1]
    tm = _pick(m, (512, 256, 128))
    tn = _pick(n, (768, 640, 512, 384, 256, 128))
    in_specs = [pl.BlockSpec((tm, k_size), lambda i, j: (i, k_block)),
                pl.BlockSpec((k_size, tn), lambda i, j: (0, j))]
    args = [x, w]
    if rms_g is not None:
        in_specs.append(pl.BlockSpec((1, k_size), lambda i, j: (0, 0)))
        args.append(rms_g.reshape(1, k_size).astype(F32))
    half = 0
    if rope is not None:
        cos, sin, half = rope
        sb = seq // tm
        in_specs += [pl.BlockSpec((tm, tn), lambda i, j: (i % sb, j))] * 2
        args += [cos, sin]
    return pl.pallas_call(
        functools.partial(_proj_kernel, has_rms=rms_g is not None, has_rope=rope is not None, half=half),
        grid=(m // tm, n // tn),
        in_specs=in_specs,
        out_specs=pl.BlockSpec((tm, tn), lambda i, j: (i, j)),
        out_shape=jax.ShapeDtypeStruct((m, n), out_dtype),
        compiler_params=_params("parallel", "parallel"),
    )(*args)


def _ffn_up_kernel(x_ref, wg_ref, wu_ref, o_ref):
    x = x_ref[...]
    g = jnp.dot(x, wg_ref[...], preferred_element_type=F32)
    u = jnp.dot(x, wu_ref[...], preferred_element_type=F32)
    o_ref[...] = (g * jax.nn.sigmoid(g) * u).astype(o_ref.dtype)


def _ffn_up(x16, wg, wu):
    m, k = x16.shape
    n = wg.shape[1]
    tm = _pick(m, (1024, 512, 256, 128))
    tn = _pick(n, (512, 256, 128))
    return pl.pallas_call(
        _ffn_up_kernel,
        grid=(n // tn, m // tm),
        in_specs=[pl.BlockSpec((tm, k), lambda j, i: (i, 0)),
                  pl.BlockSpec((k, tn), lambda j, i: (0, j)),
                  pl.BlockSpec((k, tn), lambda j, i: (0, j))],
        out_specs=pl.BlockSpec((tm, tn), lambda j, i: (i, j)),
        out_shape=jax.ShapeDtypeStruct((m, n), BF),
        compiler_params=_params("parallel", "parallel"),
    )(x16, wg, wu)


def _mm_res_ln_kernel(x_ref, w_ref, res_ref, g_ref, b_ref, o32_ref, o16_ref, acc_ref, *, nk, alpha, coef):
    k = pl.program_id(1)

    @pl.when(k == 0)
    def _():
        acc_ref[...] = jnp.zeros_like(acc_ref)

    acc_ref[...] += jnp.dot(x_ref[...], w_ref[...], preferred_element_type=F32)

    @pl.when(k == nk - 1)
    def _():
        y = alpha * res_ref[...] + coef * acc_ref[...]
        mu = jnp.mean(y, -1, keepdims=True)
        yc = y - mu
        var = jnp.mean(yc * yc, -1, keepdims=True)
        out = yc * lax.rsqrt(var + LN_EPS) * g_ref[...] + b_ref[...]
        o32_ref[...] = out
        o16_ref[...] = out.astype(BF)


def _mm_res_ln(x16, w, res, g, b, *, alpha, coef):
    m, k = x16.shape
    n = w.shape[1]
    tm = _pick(m, (512, 256, 128))
    tk = _pick(k, (1408, 1024, 512, 256, 128))
    nk = k // tk
    return pl.pallas_call(
        functools.partial(_mm_res_ln_kernel, nk=nk, alpha=alpha, coef=coef),
        grid=(m // tm, nk),
        in_specs=[pl.BlockSpec((tm, tk), lambda i, kk: (i, kk)),
                  pl.BlockSpec((tk, n), lambda i, kk: (kk, 0)),
                  pl.BlockSpec((tm, n), lambda i, kk: (i, 0)),
                  pl.BlockSpec((1, n), lambda i, kk: (0, 0)),
                  pl.BlockSpec((1, n), lambda i, kk: (0, 0))],
        out_specs=[pl.BlockSpec((tm, n), lambda i, kk: (i, 0)),
                   pl.BlockSpec((tm, n), lambda i, kk: (i, 0))],
        out_shape=[jax.ShapeDtypeStruct((m, n), F32), jax.ShapeDtypeStruct((m, n), BF)],
        scratch_shapes=[pltpu.VMEM((tm, n), F32)],
        compiler_params=_params("parallel", "arbitrary"),
    )(x16, w, res, g.reshape(1, n).astype(F32), b.reshape(1, n).astype(F32))


def _attn_kernel(qi_ref, kj_ref, fst_ref, lst_ref, *refs, n_rep, t, scale, window, seq, has_sel, has_sink):
    it = iter(refs)
    q_ref, k_ref, v_ref = next(it), next(it), next(it)
    sel_ref = next(it) if has_sel else None
    sink_ref = next(it) if has_sink else None
    o_ref = next(it)
    m_sc, l_sc, acc_sc = next(it), next(it), next(it)
    p = pl.program_id(1)
    qi = qi_ref[p]
    kj = kj_ref[p]

    @pl.when(fst_ref[p] == 1)
    def _():
        if has_sink:
            m_sc[...] = sink_ref[0]
            l_sc[...] = jnp.ones_like(l_sc)
        else:
            m_sc[...] = jnp.full_like(m_sc, NEG)
            l_sc[...] = jnp.zeros_like(l_sc)
        acc_sc[...] = jnp.zeros_like(acc_sc)

    dq = q_ref.shape[-1]
    q = q_ref[0].reshape(n_rep * t, dq)
    s = lax.dot_general(q, k_ref[0], (((1,), (1,)), ((), ())), preferred_element_type=F32) * scale
    row = lax.broadcasted_iota(jnp.int32, (t, t), 0)
    col = lax.broadcasted_iota(jnp.int32, (t, t), 1)
    dist = (qi - kj) * t + row - col
    mask = dist >= 0
    if window < seq:
        mask = mask & (dist < window)
    if has_sel:
        n_blk = sel_ref.shape[-1]
        blk_of_key = (kj * t + lax.broadcasted_iota(jnp.int32, (n_blk, t), 1)) // SLC_BLOCK
        expand = jnp.where(lax.broadcasted_iota(jnp.int32, (n_blk, t), 0) == blk_of_key, 1.0, 0.0).astype(BF)
        picked = jnp.dot(sel_ref[0].astype(BF), expand, preferred_element_type=F32)
        mask = mask & (picked > 0.5)
    s = jnp.where(mask[None], s.reshape(n_rep, t, t), NEG).reshape(n_rep * t, t)
    m_prev = m_sc[...]
    m_new = jnp.maximum(m_prev, jnp.max(s, -1, keepdims=True))
    alpha = jnp.exp(m_prev - m_new)
    pexp = jnp.exp(s - m_new)
    l_sc[...] = alpha * l_sc[...] + jnp.sum(pexp, -1, keepdims=True)
    acc_sc[...] = alpha * acc_sc[...] + jnp.dot(pexp.astype(BF), v_ref[0], preferred_element_type=F32)
    m_sc[...] = m_new

    @pl.when(lst_ref[p] == 1)
    def _():
        o = acc_sc[...] / l_sc[...]
        o_ref[0] = o.reshape(n_rep, t, o.shape[-1]).astype(o_ref.dtype)


def _attention(q, k, v, *, scale, window, t, sel=None, sinks=None, out_dtype=BF):
    bg, n_rep, seq, dq = q.shape
    dv = v.shape[-1]
    nq = seq // t
    back = min(nq - 1, -(-(window - 1) // t))
    pairs = [(i, j) for i in range(nq) for j in range(max(0, i - back), i + 1)]
    qi = jnp.asarray([a for a, _ in pairs], jnp.int32)
    kj = jnp.asarray([b for _, b in pairs], jnp.int32)
    fst = jnp.asarray([int(b == max(0, a - back)) for a, b in pairs], jnp.int32)
    lst = jnp.asarray([int(a == b) for a, b in pairs], jnp.int32)
    in_specs = [pl.BlockSpec((1, n_rep, t, dq), lambda b, p, qi, kj, f, l: (b, 0, qi[p], 0)),
                pl.BlockSpec((1, t, dq), lambda b, p, qi, kj, f, l: (b, kj[p], 0)),
                pl.BlockSpec((1, t, dv), lambda b, p, qi, kj, f, l: (b, kj[p], 0))]
    args = [q, k, v]
    if sel is not None:
        in_specs.append(pl.BlockSpec((1, t, sel.shape[-1]), lambda b, p, qi, kj, f, l: (b, qi[p], 0)))
        args.append(sel)
    if sinks is not None:
        n_grp = sinks.shape[0]
        in_specs.append(pl.BlockSpec((1, n_rep * t, 1), lambda b, p, qi, kj, f, l: (b % n_grp, 0, 0)))
        args.append(sinks)
    kern = functools.partial(_attn_kernel, n_rep=n_rep, t=t, scale=scale, window=window, seq=seq,
                             has_sel=sel is not None, has_sink=sinks is not None)
    return pl.pallas_call(
        kern,
        grid_spec=pltpu.PrefetchScalarGridSpec(
            num_scalar_prefetch=4,
            grid=(bg, len(pairs)),
            in_specs=in_specs,
            out_specs=pl.BlockSpec((1, n_rep, t, dv), lambda b, p, qi, kj, f, l: (b, 0, qi[p], 0)),
            scratch_shapes=[pltpu.VMEM((n_rep * t, 1), F32), pltpu.VMEM((n_rep * t, 1), F32),
                            pltpu.VMEM((n_rep * t, dv), F32)]),
        out_shape=jax.ShapeDtypeStruct((bg, n_rep, seq, dv), out_dtype),
        compiler_params=_params("parallel", "arbitrary"),
    )(qi, kj, fst, lst, *args)


def _compress_kernel(x_ref, pa_ref, pb_ref, w1a_ref, w1b_ref, w2_ref, o_ref):
    x = x_ref[0]
    a = jnp.dot((x + pa_ref[...]).astype(BF), w1a_ref[...], preferred_element_type=F32)
    b = jnp.dot((x + pb_ref[...]).astype(BF), w1b_ref[...], preferred_element_type=F32)
    h = jax.nn.gelu(a + pltpu.roll(b, b.shape[0] - 1, 0))
    o_ref[0] = jnp.dot(h.astype(BF), w2_ref[...], preferred_element_type=F32).astype(o_ref.dtype)


def _compress(x, pos, w1, w2):
    bg, nch, width = x.shape
    d = w2.shape[1]
    hid = w1.shape[1]
    pa = pos[:CMP_STRIDE].reshape(1, width).astype(F32)
    pb = pos[CMP_STRIDE:].reshape(1, width).astype(F32)
    w1 = w1.astype(BF)
    return pl.pallas_call(
        _compress_kernel,
        grid=(bg,),
        in_specs=[pl.BlockSpec((1, nch, width), lambda b: (b, 0, 0)),
                  pl.BlockSpec((1, width), lambda b: (0, 0)),
                  pl.BlockSpec((1, width), lambda b: (0, 0)),
                  pl.BlockSpec((width, hid), lambda b: (0, 0)),
                  pl.BlockSpec((width, hid), lambda b: (1, 0)),
                  pl.BlockSpec((hid, d), lambda b: (0, 0))],
        out_specs=pl.BlockSpec((1, nch, d), lambda b: (b, 0, 0)),
        out_shape=jax.ShapeDtypeStruct((bg, nch, d), BF),
        compiler_params=_params("parallel"),
    )(x, pa, pb, w1, w1, w2.astype(BF))


def _cmp_attn_kernel(q_ref, k_ref, v_ref, ov_ref, o_ref, sel_ref, *, n_rep, t, scale):
    qi = pl.program_id(1)
    d = q_ref.shape[-1]
    nc = k_ref.shape[1]
    q = q_ref[0].reshape(n_rep * t, d)
    s = lax.dot_general(q, k_ref[0], (((1,), (1,)), ((), ())), preferred_element_type=F32) * scale
    qpos = qi * t + lax.broadcasted_iota(jnp.int32, (t, nc), 0)
    cmp_end = lax.broadcasted_iota(jnp.int32, (t, nc), 1) * CMP_STRIDE + (CMP_BLOCK - 1)
    cmask = cmp_end <= qpos
    s = jnp.where(cmask[None], s.reshape(n_rep, t, nc), NEG)
    e = jnp.exp(s - jnp.max(s, -1, keepdims=True))
    prob = e / jnp.sum(e, -1, keepdims=True)
    prob = jnp.where(cmask[None], prob, 0.0)
    o = jnp.dot(prob.reshape(n_rep * t, nc).astype(BF), v_ref[0], preferred_element_type=F32)
    o_ref[0] = o.reshape(n_rep, t, d).astype(o_ref.dtype)
    psum = jnp.sum(prob, axis=0)
    p_hi = psum.astype(BF)
    p_lo = (psum - p_hi.astype(F32)).astype(BF)
    imp = (jnp.dot(p_hi, ov_ref[...], preferred_element_type=F32)
           + jnp.dot(p_lo, ov_ref[...], preferred_element_type=F32))
    ns = imp.shape[-1]
    blk = lax.broadcasted_iota(jnp.int32, (t, ns), 1)
    cur = (qi * t + lax.broadcasted_iota(jnp.int32, (t, ns), 0)) // SLC_BLOCK
    eligible = blk <= cur
    forced = (blk == 0) | (blk == cur) | (blk == cur - 1)
    score = jnp.where(eligible, jnp.where(forced, FORCE_SCORE, imp), -1.0)
    beaten_by = jnp.zeros((t, ns), F32)
    for kk in range(ns):
        sk = score[:, kk:kk + 1]
        beats = (sk > score) | ((sk == score) & (blk > kk))
        beaten_by = beaten_by + jnp.where(beats, 1.0, 0.0)
    n_sel = min(N_SEL, ns)
    sel_ref[0] = jnp.where((beaten_by < n_sel) & (score >= 0.0), 1.0, 0.0)


def _cmp_attention(q, kcmp, vcmp, *, scale, t):
    bg, n_rep, seq, d = q.shape
    nc = kcmp.shape[1]
    ns = seq // SLC_BLOCK
    starts = np.arange(nc) * CMP_STRIDE
    jb = np.arange(ns)
    overlap = ((starts[:, None] < (jb[None, :] + 1) * SLC_BLOCK)
               & (starts[:, None] + CMP_BLOCK > jb[None, :] * SLC_BLOCK)).astype(np.float32)
    return pl.pallas_call(
        functools.partial(_cmp_attn_kernel, n_rep=n_rep, t=t, scale=scale),
        grid=(bg, seq // t),
        in_specs=[pl.BlockSpec((1, n_rep, t, d), lambda b, i: (b, 0, i, 0)),
                  pl.BlockSpec((1, nc, d), lambda b, i: (b, 0, 0)),
                  pl.BlockSpec((1, nc, d), lambda b, i: (b, 0, 0)),
                  pl.BlockSpec((nc, ns), lambda b, i: (0, 0))],
        out_specs=[pl.BlockSpec((1, n_rep, t, d), lambda b, i: (b, 0, i, 0)),
                   pl.BlockSpec((1, t, ns), lambda b, i: (b, i, 0))],
        out_shape=[jax.ShapeDtypeStruct((bg, n_rep, seq, d), BF),
                   jax.ShapeDtypeStruct((bg, seq, ns), F32)],
        compiler_params=_params("parallel", "parallel"),
    )(q, kcmp, vcmp, jnp.asarray(overlap, BF))


def _gate_kernel(gl_ref, gb_ref, oc_ref, os_ref, ow_ref, o_ref):
    gates = jax.nn.sigmoid(gl_ref[0] + gb_ref[0])
    o = (gates[:, 0:1] * oc_ref[0].astype(F32) + gates[:, 1:2] * os_ref[0].astype(F32)
         + gates[:, 2:3] * ow_ref[0].astype(F32))
    o_ref[0] = o.astype(o_ref.dtype)


def _gate_combine(gl, gate_b, o_cmp, o_slc, o_win, batch):
    bh, seq, d = o_cmp.shape
    heads = bh // batch
    ts = _pick(seq, (512, 256, 128))
    o_spec = pl.BlockSpec((1, ts, d), lambda b, i: (b, i, 0))
    return pl.pallas_call(
        _gate_kernel,
        grid=(bh, seq // ts),
        in_specs=[pl.BlockSpec((1, ts, 3), lambda b, i: (b, i, 0)),
                  pl.BlockSpec((1, 1, 3), lambda b, i: (b % heads, 0, 0)),
                  o_spec, o_spec, o_spec],
        out_specs=pl.BlockSpec((1, ts, d), lambda b, i: (b // heads, i, b % heads)),
        out_shape=jax.ShapeDtypeStruct((batch, seq, heads * d), BF),
        compiler_params=_params("parallel", "parallel"),
    )(gl, gate_b.reshape(heads, 1, 3).astype(F32), o_cmp, o_slc, o_win)


def _rope_tables(seq, dim, n_cols, rope_cols):
    inv = 1.0 / (ROPE_THETA ** (jnp.arange(0, dim, 2, dtype=F32) / dim))
    ang = jnp.arange(seq, dtype=F32)[:, None] * inv[None, :]
    cos_h = jnp.concatenate([jnp.cos(ang), jnp.cos(ang)], -1)
    sin_h = jnp.concatenate([-jnp.sin(ang), jnp.sin(ang)], -1)
    cos = jnp.ones((seq, n_cols), F32)
    sin = jnp.zeros((seq, n_cols), F32)
    for a, b in rope_cols:
        reps = (b - a) // dim
        cos = cos.at[:, a:b].set(jnp.tile(cos_h, (1, reps)))
        sin = sin.at[:, a:b].set(jnp.tile(sin_h, (1, reps)))
    return cos, sin


def _pad_cols(w, n):
    return jnp.pad(w, ((0, 0), (0, n - w.shape[1])))


def _even_mixer(x16, batch, seq, w_in, q_norm, w_uq, kv_norm, w_ukv, sinks):
    H, dn, dr, dv = MLA_HEADS, MLA_NOPE, MLA_ROPE, MLA_V
    G, R, ds = SWA_KV_HEADS, SWA_HEADS // SWA_KV_HEADS, SWA_HEAD_DIM
    n_in = w_in.shape[1]
    n_pad = -(-n_in // 512) * 512
    o_kpe = MLA_Q_RANK + MLA_KV_RANK
    o_qs = o_kpe + dr
    o_ks = o_qs + SWA_HEADS * ds
    o_vs = o_ks + G * ds
    cos, sin = _rope_tables(seq, ds, n_pad, [(o_kpe, o_vs)])
    z = _proj(x16, _pad_cols(w_in, n_pad).astype(BF), rope=(cos, sin, ds // 2), seq=seq)

    w_uq_p = jnp.concatenate([w_uq.reshape(-1, H, dn + dr)[:, :, :dn].reshape(-1, H * dn),
                              w_uq.reshape(-1, H, dn + dr)[:, :, dn:].reshape(-1, H * dr)], -1)
    w_ukv_p = jnp.concatenate([w_ukv.reshape(-1, H, dn + dv)[:, :, :dn].reshape(-1, H * dn),
                               w_ukv.reshape(-1, H, dn + dv)[:, :, dn:].reshape(-1, H * dv)], -1)
    cos_q, sin_q = _rope_tables(seq, dr, H * (dn + dr), [(H * dn, H * (dn + dr))])
    q = _proj(z, w_uq_p.astype(BF), k_block=0, k_size=MLA_Q_RANK, rms_g=q_norm,
              rope=(cos_q, sin_q, dr // 2), seq=seq, out_dtype=BF)
    kv = _proj(z, w_ukv_p.astype(BF), k_block=1, k_size=MLA_KV_RANK, rms_g=kv_norm, out_dtype=BF)
    q_mla = jnp.concatenate([q[:, :H * dn].reshape(batch, seq, H, dn),
                             q[:, H * dn:].reshape(batch, seq, H, dr)], -1)
    k_pe = z[:, o_kpe:o_qs].astype(BF).reshape(batch, seq, 1, dr)
    k_mla = jnp.concatenate([kv[:, :H * dn].reshape(batch, seq, H, dn),
                             jnp.broadcast_to(k_pe, (batch, seq, H, dr))], -1)
    v_mla = kv[:, H * dn:].reshape(batch, seq, H, dv)
    q_mla = q_mla.transpose(0, 2, 1, 3).reshape(batch * H, 1, seq, dn + dr)
    k_mla = k_mla.transpose(0, 2, 1, 3).reshape(batch * H, seq, dn + dr)
    v_mla = v_mla.transpose(0, 2, 1, 3).reshape(batch * H, seq, dv)
    o_mla = _attention(q_mla, k_mla, v_mla, scale=(dn + dr) ** -0.5, window=seq, t=_pick(seq, (256, 128)))
    o_mla = o_mla.reshape(batch, H, seq, dv).transpose(0, 2, 1, 3).reshape(batch * seq, H * dv)

    t = 128
    q_s = z[:, o_qs:o_ks].astype(BF).reshape(batch, seq, G, R, ds).transpose(0, 2, 3, 1, 4)
    k_s = z[:, o_ks:o_vs].astype(BF).reshape(batch, seq, G, ds).transpose(0, 2, 1, 3)
    v_s = z[:, o_vs:o_vs + G * ds].astype(BF).reshape(batch, seq, G, ds).transpose(0, 2, 1, 3)
    sink_rows = jnp.broadcast_to(sinks.astype(F32).reshape(G, R, 1, 1), (G, R, t, 1)).reshape(G, R * t, 1)
    o_swa = _attention(q_s.reshape(batch * G, R, seq, ds), k_s.reshape(batch * G, seq, ds),
                       v_s.reshape(batch * G, seq, ds), scale=ds ** -0.5, window=SWA_WINDOW, t=t,
                       sinks=sink_rows)
    o_swa = o_swa.reshape(batch, G, R, seq, ds).transpose(0, 3, 1, 2, 4).reshape(batch * seq, G * R * ds)
    return jnp.concatenate([o_mla, o_swa], -1)


def _odd_mixer(x16, batch, seq, w_in, gate_b, cmp_pos_k, cmp_pos_v, k_w1, k_w2, v_w1, v_w2):
    G, d = NSA_KV_GROUPS, NSA_HEAD_DIM
    H = NSA_HEADS
    R = H // G
    kvw = G * d
    t = 128
    n_in = w_in.shape[1]
    n_pad = -(-n_in // 768) * 768
    o_kc = H * d
    offs = [o_kc + i * kvw for i in range(7)]
    cos, sin = _rope_tables(seq, d, n_pad, [(0, offs[1]), (offs[2], offs[3]), (offs[4], offs[5])])
    z = _proj(x16, _pad_cols(w_in, n_pad).astype(BF), rope=(cos, sin, d // 2), seq=seq)

    def heads_q(a):
        return a.astype(BF).reshape(batch, seq, G, R, d).transpose(0, 2, 3, 1, 4).reshape(batch * G, R, seq, d)

    def heads_kv(a, dtype=BF):
        return a.astype(dtype).reshape(batch, seq, G, d).transpose(0, 2, 1, 3).reshape(batch * G, seq, d)

    q = heads_q(z[:, :o_kc])
    kc, vc, ks, vs, kw, vw = (z[:, offs[i]:offs[i + 1]] for i in range(6))
    gl = z[:, offs[6]:offs[6] + 3 * H]
    scale = d ** -0.5
    nch = seq // CMP_STRIDE
    kcmp = _compress(heads_kv(kc, F32).reshape(batch * G, nch, CMP_STRIDE * d), cmp_pos_k, k_w1, k_w2)
    vcmp = _compress(heads_kv(vc, F32).reshape(batch * G, nch, CMP_STRIDE * d), cmp_pos_v, v_w1, v_w2)
    o_cmp, sel = _cmp_attention(q, kcmp, vcmp, scale=scale, t=t)
    o_slc = _attention(q, heads_kv(ks), heads_kv(vs), scale=scale, window=seq, t=t, sel=sel)
    o_win = _attention(q, heads_kv(kw), heads_kv(vw), scale=scale, window=NSA_WINDOW, t=t)
    gl = gl.reshape(batch, seq, H, 3).transpose(0, 2, 1, 3).reshape(batch * H, seq, 3)
    o = _gate_combine(gl, gate_b, o_cmp.reshape(batch * H, seq, d), o_slc.reshape(batch * H, seq, d),
                      o_win.reshape(batch * H, seq, d), batch)
    return o.reshape(batch * seq, H * d)


def _ffn(x32, x16, wg, wu, wd, g, b, alpha):
    h = _ffn_up(x16, wg.astype(BF), wu.astype(BF))
    return _mm_res_ln(h, wd.astype(BF), x32, g, b, alpha=alpha, coef=0.5)


def kernel(x, ln_g, ln_b, ffn_w_gate, ffn_w_up, ffn_w_down, even_w_in, mla_q_norm, mla_w_uq, mla_kv_norm, mla_w_ukv, swa_sinks, even_w_out, odd_w_in, nsa_gate_b, nsa_cmp_pos_k, nsa_cmp_pos_v, nsa_cmp_k_w1, nsa_cmp_k_w2, nsa_cmp_v_w1, nsa_cmp_v_w2, odd_w_out):
    batch, seq, dm = x.shape
    depth = ln_g.shape[0]
    alpha = float((2 * depth) ** 0.25)
    x32 = x.reshape(batch * seq, dm)
    x16 = x32.astype(BF)
    for l in range(depth):
        x32, x16 = _ffn(x32, x16, ffn_w_gate[l, 0], ffn_w_up[l, 0], ffn_w_down[l, 0], ln_g[l, 0], ln_b[l, 0], alpha)
        j = l // 2
        if l % 2 == 0:
            o = _even_mixer(x16, batch, seq, even_w_in[j], mla_q_norm[j], mla_w_uq[j], mla_kv_norm[j],
                            mla_w_ukv[j], swa_sinks[j])
            w_out = even_w_out[j]
        else:
            o = _odd_mixer(x16, batch, seq, odd_w_in[j], nsa_gate_b[j], nsa_cmp_pos_k[j], nsa_cmp_pos_v[j],
                           nsa_cmp_k_w1[j], nsa_cmp_k_w2[j], nsa_cmp_v_w1[j], nsa_cmp_v_w2[j])
            w_out = odd_w_out[j]
        x32, x16 = _mm_res_ln(o, w_out.astype(BF), x32, ln_g[l, 1], ln_b[l, 1], alpha=alpha, coef=1.0)
        x32, x16 = _ffn(x32, x16, ffn_w_gate[l, 1], ffn_w_up[l, 1], ffn_w_down[l, 1], ln_g[l, 2], ln_b[l, 2], alpha)
    return x32.reshape(batch, seq, dm)
```

```python
import functools
import math

import numpy as np
import jax
import jax.numpy as jnp
from jax import lax
from jax.experimental import pallas as pl
from jax.experimental.pallas import tpu as pltpu

F32 = jnp.float32
BF = jnp.bfloat16

ROPE_THETA = 10000.0
LN_EPS = 1e-5
RMS_EPS = 1e-6
NEG = -1e30
FORCE_SCORE = 1e9
LOG2E = math.log2(math.e)

MLA_HEADS, MLA_NOPE, MLA_ROPE, MLA_V = 8, 128, 64, 128
MLA_Q_RANK, MLA_KV_RANK = 512, 512
SWA_HEADS, SWA_KV_HEADS, SWA_HEAD_DIM, SWA_WINDOW = 16, 2, 64, 128
NSA_HEADS, NSA_KV_GROUPS, NSA_HEAD_DIM = 16, 2, 128
CMP_BLOCK, CMP_STRIDE, CMP_HIDDEN = 32, 16, 256
SLC_BLOCK, N_SEL, NSA_WINDOW = 64, 8, 512

LANE = 128
VMEM_LIMIT_BYTES = 52 * 1024 * 1024


def _params(*sem):
    return pltpu.CompilerParams(dimension_semantics=sem, vmem_limit_bytes=VMEM_LIMIT_BYTES)


def _pick(n, candidates):
    for c in candidates:
        if n % c == 0:
            return c
    raise ValueError(f"no tile in {candidates} divides {n}")


def _lane_tile(x, n):
    return x if n == 1 else jnp.concatenate([x] * n, axis=1)


def _proj_kernel(*refs, has_rms, has_rope, half):
    it = iter(refs)
    x_ref, w_ref = next(it), next(it)
    g_ref = next(it) if has_rms else None
    cos_ref, sin_ref = (next(it), next(it)) if has_rope else (None, None)
    o_ref = next(it)
    x = x_ref[...]
    if has_rms:
        xf = x.astype(F32)
        x = xf * lax.rsqrt(jnp.mean(xf * xf, -1, keepdims=True) + RMS_EPS) * g_ref[...]
    acc = jnp.dot(x.astype(BF), w_ref[...], preferred_element_type=F32)
    if has_rope:
        n = acc.shape[-1]
        nxt = pltpu.roll(acc, n - half, 1)
        prv = pltpu.roll(acc, half, 1)
        lane = lax.broadcasted_iota(jnp.int32, acc.shape, 1)
        partner = jnp.where(lane % (2 * half) < half, nxt, prv)
        acc = acc * cos_ref[...] + partner * sin_ref[...]
    o_ref[...] = acc.astype(o_ref.dtype)


def _proj(x, w, *, k_block=0, k_size=None, rms_g=None, rope=None, seq=None, out_dtype=F32, name="proj"):
    m = x.shape[0]
    k_size = x.shape[1] if k_size is None else k_size
    n = w.shape[1]
    tm = _pick(m, (512, 256, 128))
    tn = _pick(n, (768, 640, 512, 384, 256, 128))
    in_specs = [pl.BlockSpec((tm, k_size), lambda i, j: (i, k_block)),
                pl.BlockSpec((k_size, tn), lambda i, j: (0, j))]
    args = [x, w]
    if rms_g is not None:
        in_specs.append(pl.BlockSpec((1, k_size), lambda i, j: (0, 0)))
        args.append(rms_g.reshape(1, k_size).astype(F32))
    half = 0
    if rope is not None:
        cos, sin, half = rope
        sb = seq // tm
        in_specs += [pl.BlockSpec((tm, tn), lambda i, j: (i % sb, j))] * 2
        args += [cos, sin]
    return pl.pallas_call(
        functools.partial(_proj_kernel, has_rms=rms_g is not None, has_rope=rope is not None, half=half),
        grid=(m // tm, n // tn),
        in_specs=in_specs,
        out_specs=pl.BlockSpec((tm, tn), lambda i, j: (i, j)),
        out_shape=jax.ShapeDtypeStruct((m, n), out_dtype),
        compiler_params=_params("parallel", "parallel"),
        name=name,
    )(*args)


def _ffn_up_kernel(x_ref, wg_ref, wu_ref, o_ref, wg16, wu16):
    @pl.when(pl.program_id(1) == 0)
    def _():
        wg16[...] = wg_ref[...].astype(BF)
        wu16[...] = wu_ref[...].astype(BF)

    x = x_ref[...]
    g = jnp.dot(x, wg16[...], preferred_element_type=F32)
    u = jnp.dot(x, wu16[...], preferred_element_type=F32)
    o_ref[...] = (g * jax.nn.sigmoid(g) * u).astype(o_ref.dtype)


def _ffn_up(x16, wg, wu):
    m, k = x16.shape
    n = wg.shape[1]
    tm = _pick(m, (1024, 512, 256, 128))
    tn = _pick(n, (512, 256, 128))
    return pl.pallas_call(
        _ffn_up_kernel,
        grid=(n // tn, m // tm),
        in_specs=[pl.BlockSpec((tm, k), lambda j, i: (i, 0)),
                  pl.BlockSpec((k, tn), lambda j, i: (0, j)),
                  pl.BlockSpec((k, tn), lambda j, i: (0, j))],
        out_specs=pl.BlockSpec((tm, tn), lambda j, i: (i, j)),
        out_shape=jax.ShapeDtypeStruct((m, n), BF),
        scratch_shapes=[pltpu.VMEM((k, tn), BF), pltpu.VMEM((k, tn), BF)],
        compiler_params=_params("parallel", "arbitrary"),
        name="ffn_up",
    )(x16, wg, wu)


def _mm_res_ln_kernel(x_ref, w_ref, res_ref, g_ref, b_ref, o32_ref, o16_ref, acc_ref, *, nk, alpha, coef):
    k = pl.program_id(1)

    @pl.when(k == 0)
    def _():
        acc_ref[...] = jnp.zeros_like(acc_ref)

    acc_ref[...] += jnp.dot(x_ref[...], w_ref[...], preferred_element_type=F32)

    @pl.when(k == nk - 1)
    def _():
        y = alpha * res_ref[...] + coef * acc_ref[...]
        mu = jnp.mean(y, -1, keepdims=True)
        yc = y - mu
        var = jnp.mean(yc * yc, -1, keepdims=True)
        out = yc * lax.rsqrt(var + LN_EPS) * g_ref[...] + b_ref[...]
        o32_ref[...] = out
        o16_ref[...] = out.astype(BF)


def _mm_res_ln(x16, w, res, g, b, *, alpha, coef, name):
    m, k = x16.shape
    n = w.shape[1]
    tm = _pick(m, (512, 256, 128))
    tk = _pick(k, (1408, 1024, 512, 256, 128))
    nk = k // tk
    return pl.pallas_call(
        functools.partial(_mm_res_ln_kernel, nk=nk, alpha=alpha, coef=coef),
        grid=(m // tm, nk),
        in_specs=[pl.BlockSpec((tm, tk), lambda i, kk: (i, kk)),
                  pl.BlockSpec((tk, n), lambda i, kk: (kk, 0)),
                  pl.BlockSpec((tm, n), lambda i, kk: (i, 0)),
                  pl.BlockSpec((1, n), lambda i, kk: (0, 0)),
                  pl.BlockSpec((1, n), lambda i, kk: (0, 0))],
        out_specs=[pl.BlockSpec((tm, n), lambda i, kk: (i, 0)),
                   pl.BlockSpec((tm, n), lambda i, kk: (i, 0))],
        out_shape=[jax.ShapeDtypeStruct((m, n), F32), jax.ShapeDtypeStruct((m, n), BF)],
        scratch_shapes=[pltpu.VMEM((tm, n), F32)],
        compiler_params=_params("parallel", "arbitrary"),
        name=name,
    )(x16, w, res, g.reshape(1, n).astype(F32), b.reshape(1, n).astype(F32))


def _attn_kernel(qi_ref, kj_ref, fst_ref, lst_ref, *refs, n_rep, tq, tk, dv, c_exp, window, seq,
                 has_sel, has_sink, gate_col, has_add, token_major):
    it = iter(refs)
    q_ref, k_ref, v_ref = next(it), next(it), next(it)
    sel_ref = next(it) if has_sel else None
    sink_ref = next(it) if has_sink else None
    gl_ref, gb_ref = (next(it), next(it)) if gate_col is not None else (None, None)
    add_ref = next(it) if has_add else None
    o_ref = next(it)
    m_sc, acc_sc = next(it), next(it)
    p = pl.program_id(1)
    qi = qi_ref[p]
    kj = kj_ref[p]
    rows = n_rep * tq

    @pl.when(fst_ref[p] == 1)
    def _():
        if has_sink:
            m_sc[...] = sink_ref[0]
            lane = lax.broadcasted_iota(jnp.int32, acc_sc.shape, 1)
            acc_sc[...] = jnp.where(lane >= dv, 1.0, 0.0)
        else:
            m_sc[...] = jnp.full_like(m_sc, NEG)
            acc_sc[...] = jnp.zeros_like(acc_sc)

    dq = q_ref.shape[-1]
    q = q_ref[0].reshape(rows, dq)
    s = lax.dot_general(q, k_ref[0], (((1,), (1,)), ((), ())), preferred_element_type=F32)
    row = lax.broadcasted_iota(jnp.int32, (tq, tk), 0)
    col = lax.broadcasted_iota(jnp.int32, (tq, tk), 1)
    dist = qi * tq - kj * tk + row - col
    mask = dist >= 0
    if window < seq:
        mask = mask & (dist < window)
    if has_sel:
        n_blk = sel_ref.shape[-1]
        blk_of_key = (kj * tk + lax.broadcasted_iota(jnp.int32, (n_blk, tk), 1)) // SLC_BLOCK
        expand = jnp.where(lax.broadcasted_iota(jnp.int32, (n_blk, tk), 0) == blk_of_key, 1.0, 0.0).astype(BF)
        picked = jnp.dot(sel_ref[0].astype(BF), expand, preferred_element_type=F32)
        mask = mask & (picked > 0.5)
    s = jnp.where(mask[None], s.reshape(n_rep, tq, tk), NEG).reshape(rows, tk)
    m_prev = m_sc[...]
    m_new = jnp.maximum(m_prev, jnp.max(s, -1, keepdims=True))
    alpha = jnp.exp2((m_prev - m_new) * c_exp)
    w = jnp.exp2((s - _lane_tile(m_new, tk // LANE)) * c_exp)
    acc_sc[...] = (_lane_tile(alpha, acc_sc.shape[1] // LANE) * acc_sc[...]
                   + jnp.dot(w.astype(BF), v_ref[0], preferred_element_type=F32))
    m_sc[...] = m_new

    @pl.when(lst_ref[p] == 1)
    def _():
        acc = acc_sc[...]
        if dv % LANE == 0:
            o = acc[:, :dv] / acc[:, dv:]
        else:
            o = acc[:, :dv] / acc[:, dv:dv + 1]
        if gate_col is not None:
            gates = jax.nn.sigmoid(gl_ref[0] + gb_ref[0])
        for r in range(n_rep):
            o_r = o[r * tq:(r + 1) * tq]
            if gate_col is not None:
                o_r = o_r * gates[:, 3 * r + gate_col:3 * r + gate_col + 1]
            if has_add:
                o_r = o_r + add_ref[0, r].astype(F32)
            if token_major:
                o_ref[0, :, r * dv:(r + 1) * dv] = o_r.astype(o_ref.dtype)
            else:
                o_ref[0, r] = o_r.astype(o_ref.dtype)


def _attention(q, k, v, *, scale, window, tq, tk, sel=None, sinks=None, gate=None, add=None, out_batch=None,
               name="attn"):
    bg, n_rep, seq, dq = q.shape
    dv = v.shape[-1]
    rows = n_rep * tq
    nq = seq // tq
    pairs, fst, lst = [], [], []
    for i in range(nq):
        lo = max(0, i * tq - (window - 1)) // tk
        hi = (i * tq + tq - 1) // tk
        for j in range(lo, hi + 1):
            pairs.append((i, j))
            fst.append(int(j == lo))
            lst.append(int(j == hi))
    qi = jnp.asarray([a for a, _ in pairs], jnp.int32)
    kj = jnp.asarray([b for _, b in pairs], jnp.int32)
    v_ext = jnp.concatenate([v, jnp.ones_like(v)], -1)
    in_specs = [pl.BlockSpec((1, n_rep, tq, dq), lambda b, p, qi, kj, f, l: (b, 0, qi[p], 0)),
                pl.BlockSpec((1, tk, dq), lambda b, p, qi, kj, f, l: (b, kj[p], 0)),
                pl.BlockSpec((1, tk, 2 * dv), lambda b, p, qi, kj, f, l: (b, kj[p], 0))]
    args = [q, k, v_ext]
    if sel is not None:
        in_specs.append(pl.BlockSpec((1, tq, sel.shape[-1]), lambda b, p, qi, kj, f, l: (b, qi[p], 0)))
        args.append(sel)
    if sinks is not None:
        n_grp = sinks.shape[0]
        m0 = jnp.broadcast_to((sinks.astype(F32) / scale)[:, :, None, None], (n_grp, n_rep, tq, LANE))
        in_specs.append(pl.BlockSpec((1, rows, LANE), lambda b, p, qi, kj, f, l: (b % n_grp, 0, 0)))
        args.append(m0.reshape(n_grp, rows, LANE))
    gate_col = None
    if gate is not None:
        gl, gb, gate_col = gate
        n_gg = gb.shape[0]
        in_specs += [pl.BlockSpec((1, tq, gl.shape[-1]), lambda b, p, qi, kj, f, l: (b, qi[p], 0)),
                     pl.BlockSpec((1, 1, gb.shape[-1]), lambda b, p, qi, kj, f, l: (b % n_gg, 0, 0))]
        args += [gl, gb]
    if add is not None:
        in_specs.append(pl.BlockSpec((1, n_rep, tq, dv), lambda b, p, qi, kj, f, l: (b, 0, qi[p], 0)))
        args.append(add)
    if out_batch is None:
        out_spec = pl.BlockSpec((1, n_rep, tq, dv), lambda b, p, qi, kj, f, l: (b, 0, qi[p], 0))
        out_shape = jax.ShapeDtypeStruct((bg, n_rep, seq, dv), BF)
    else:
        n_grp = bg // out_batch
        out_spec = pl.BlockSpec((1, tq, n_rep * dv), lambda b, p, qi, kj, f, l: (b // n_grp, qi[p], b % n_grp))
        out_shape = jax.ShapeDtypeStruct((out_batch, seq, n_grp * n_rep * dv), BF)
    kern = functools.partial(_attn_kernel, n_rep=n_rep, tq=tq, tk=tk, dv=dv, c_exp=scale * LOG2E, window=window,
                             seq=seq, has_sel=sel is not None, has_sink=sinks is not None, gate_col=gate_col,
                             has_add=add is not None, token_major=out_batch is not None)
    return pl.pallas_call(
        kern,
        grid_spec=pltpu.PrefetchScalarGridSpec(
            num_scalar_prefetch=4,
            grid=(bg, len(pairs)),
            in_specs=in_specs,
            out_specs=out_spec,
            scratch_shapes=[pltpu.VMEM((rows, LANE), F32), pltpu.VMEM((rows, 2 * dv), F32)]),
        out_shape=out_shape,
        compiler_params=_params("parallel", "arbitrary"),
        name=name,
    )(qi, kj, jnp.asarray(fst, jnp.int32), jnp.asarray(lst, jnp.int32), *args)


def _compress_kernel(x_ref, pa_ref, pb_ref, w1a_ref, w1b_ref, w2_ref, o_ref):
    x = x_ref[0]
    a = jnp.dot((x + pa_ref[...]).astype(BF), w1a_ref[...], preferred_element_type=F32)
    b = jnp.dot((x + pb_ref[...]).astype(BF), w1b_ref[...], preferred_element_type=F32)
    h = jax.nn.gelu(a + pltpu.roll(b, b.shape[0] - 1, 0))
    o_ref[0] = jnp.dot(h.astype(BF), w2_ref[...], preferred_element_type=F32).astype(o_ref.dtype)


def _compress(x, pos, w1, w2, name):
    bg, nch, width = x.shape
    d = w2.shape[1]
    hid = w1.shape[1]
    pa = pos[:CMP_STRIDE].reshape(1, width).astype(F32)
    pb = pos[CMP_STRIDE:].reshape(1, width).astype(F32)
    w1 = w1.astype(BF)
    return pl.pallas_call(
        _compress_kernel,
        grid=(bg,),
        in_specs=[pl.BlockSpec((1, nch, width), lambda b: (b, 0, 0)),
                  pl.BlockSpec((1, width), lambda b: (0, 0)),
                  pl.BlockSpec((1, width), lambda b: (0, 0)),
                  pl.BlockSpec((width, hid), lambda b: (0, 0)),
                  pl.BlockSpec((width, hid), lambda b: (1, 0)),
                  pl.BlockSpec((hid, d), lambda b: (0, 0))],
        out_specs=pl.BlockSpec((1, nch, d), lambda b: (b, 0, 0)),
        out_shape=jax.ShapeDtypeStruct((bg, nch, d), BF),
        compiler_params=_params("parallel"),
        name=name,
    )(x, pa, pb, w1, w1, w2.astype(BF))


def _cmp_attn_kernel(q_ref, k_ref, v_ref, ovt_ref, gl_ref, gb_ref, o_ref, sel_ref, *, n_rep, tq, c_exp):
    qi = pl.program_id(1)
    d = q_ref.shape[-1]
    nc = k_ref.shape[1]
    rows = n_rep * tq
    q = q_ref[0].reshape(rows, d)
    s = lax.dot_general(q, k_ref[0], (((1,), (1,)), ((), ())), preferred_element_type=F32)
    qpos = qi * tq + lax.broadcasted_iota(jnp.int32, (tq, nc), 0)
    cmp_end = lax.broadcasted_iota(jnp.int32, (tq, nc), 1) * CMP_STRIDE + (CMP_BLOCK - 1)
    cmask = (cmp_end <= qpos)[None]
    s = jnp.where(cmask, s.reshape(n_rep, tq, nc), NEG)
    e = jnp.exp2((s - jnp.max(s, -1, keepdims=True)) * c_exp)
    e = jnp.where(cmask, e, 0.0).reshape(rows, nc)
    ext = jnp.dot(e.astype(BF), v_ref[0], preferred_element_type=F32)
    den = ext[:, d:]
    inv = 1.0 / jnp.where(den > 0.0, den, 1.0)
    gates = jax.nn.sigmoid(gl_ref[0] + gb_ref[0])
    o = ext[:, :d] * inv
    for r in range(n_rep):
        o_ref[0, r] = (o[r * tq:(r + 1) * tq] * gates[:, 3 * r:3 * r + 1]).astype(o_ref.dtype)
    inv_c = inv[:, :nc] if nc <= d else _lane_tile(inv, nc // d)
    prob = e * inv_c
    psum = jnp.sum(prob.reshape(n_rep, tq, nc), axis=0)
    p_hi = psum.astype(BF)
    p_lo = (psum - p_hi.astype(F32)).astype(BF)
    contract_last = (((1,), (1,)), ((), ()))
    imp = (lax.dot_general(ovt_ref[...], p_hi, contract_last, preferred_element_type=F32)
           + lax.dot_general(ovt_ref[...], p_lo, contract_last, preferred_element_type=F32))
    ns = imp.shape[0]
    blk = lax.broadcasted_iota(jnp.int32, (ns, tq), 0)
    cur = (qi * tq + lax.broadcasted_iota(jnp.int32, (ns, tq), 1)) // SLC_BLOCK
    eligible = blk <= cur
    forced = (blk == 0) | (blk == cur) | (blk == cur - 1)
    score = jnp.where(eligible, jnp.where(forced, FORCE_SCORE, imp), -1.0)
    beaten_by = jnp.zeros((ns, tq), F32)
    for kk in range(ns):
        sk = score[kk:kk + 1, :]
        beats = (sk > score) | ((sk == score) & (blk > kk))
        beaten_by = beaten_by + jnp.where(beats, 1.0, 0.0)
    n_sel = min(N_SEL, ns)
    sel_t = jnp.where((beaten_by < n_sel) & (score >= 0.0), 1.0, 0.0)
    sel_sq = jnp.concatenate([sel_t, jnp.zeros((tq - ns, tq), F32)], axis=0) if ns < tq else sel_t
    sel_ref[0] = sel_sq.T[:, :ns]


def _cmp_attention(q, kcmp, vcmp, gl, gb, *, scale, tq):
    bg, n_rep, seq, d = q.shape
    nc = kcmp.shape[1]
    ns = seq // SLC_BLOCK
    n_gg = gb.shape[0]
    starts = np.arange(nc) * CMP_STRIDE
    jb = np.arange(ns)
    overlap = ((starts[:, None] < (jb[None, :] + 1) * SLC_BLOCK)
               & (starts[:, None] + CMP_BLOCK > jb[None, :] * SLC_BLOCK)).astype(np.float32)
    v_ext = jnp.concatenate([vcmp, jnp.ones_like(vcmp)], -1)
    return pl.pallas_call(
        functools.partial(_cmp_attn_kernel, n_rep=n_rep, tq=tq, c_exp=scale * LOG2E),
        grid=(bg, seq // tq),
        in_specs=[pl.BlockSpec((1, n_rep, tq, d), lambda b, i: (b, 0, i, 0)),
                  pl.BlockSpec((1, nc, d), lambda b, i: (b, 0, 0)),
                  pl.BlockSpec((1, nc, 2 * d), lambda b, i: (b, 0, 0)),
                  pl.BlockSpec((ns, nc), lambda b, i: (0, 0)),
                  pl.BlockSpec((1, tq, gl.shape[-1]), lambda b, i: (b, i, 0)),
                  pl.BlockSpec((1, 1, gb.shape[-1]), lambda b, i: (b % n_gg, 0, 0))],
        out_specs=[pl.BlockSpec((1, n_rep, tq, d), lambda b, i: (b, 0, i, 0)),
                   pl.BlockSpec((1, tq, ns), lambda b, i: (b, i, 0))],
        out_shape=[jax.ShapeDtypeStruct((bg, n_rep, seq, d), BF),
                   jax.ShapeDtypeStruct((bg, seq, ns), F32)],
        compiler_params=_params("parallel", "parallel"),
        name="nsa_cmp_attn",
    )(q, kcmp, v_ext, jnp.asarray(overlap.T, BF), gl, gb)


def _rope_tables(seq, dim, n_cols, rope_cols):
    inv = 1.0 / (ROPE_THETA ** (jnp.arange(0, dim, 2, dtype=F32) / dim))
    ang = jnp.arange(seq, dtype=F32)[:, None] * inv[None, :]
    cos_h = jnp.concatenate([jnp.cos(ang), jnp.cos(ang)], -1)
    sin_h = jnp.concatenate([-jnp.sin(ang), jnp.sin(ang)], -1)
    cos = jnp.ones((seq, n_cols), F32)
    sin = jnp.zeros((seq, n_cols), F32)
    for a, b in rope_cols:
        reps = (b - a) // dim
        cos = cos.at[:, a:b].set(jnp.tile(cos_h, (1, reps)))
        sin = sin.at[:, a:b].set(jnp.tile(sin_h, (1, reps)))
    return cos, sin


def _pad_cols(w, n):
    return jnp.pad(w, ((0, 0), (0, n - w.shape[1])))


def _even_mixer(x16, batch, seq, w_in, q_norm, w_uq, kv_norm, w_ukv, sinks):
    H, dn, dr, dv = MLA_HEADS, MLA_NOPE, MLA_ROPE, MLA_V
    G, R, ds = SWA_KV_HEADS, SWA_HEADS // SWA_KV_HEADS, SWA_HEAD_DIM
    n_lat = MLA_Q_RANK + MLA_KV_RANK
    z_lat = _proj(x16, w_in[:, :n_lat].astype(BF), name="even_in_latent")
    n_rg = w_in.shape[1] - n_lat
    n_pad = -(-n_rg // 512) * 512
    o_qs = dr
    o_ks = o_qs + SWA_HEADS * ds
    o_vs = o_ks + G * ds
    cos, sin = _rope_tables(seq, ds, n_pad, [(0, o_vs)])
    z = _proj(x16, _pad_cols(w_in[:, n_lat:], n_pad).astype(BF), rope=(cos, sin, ds // 2), seq=seq,
              out_dtype=BF, name="even_in_rope")

    w_uq_p = jnp.concatenate([w_uq.reshape(-1, H, dn + dr)[:, :, :dn].reshape(-1, H * dn),
                              w_uq.reshape(-1, H, dn + dr)[:, :, dn:].reshape(-1, H * dr)], -1)
    w_ukv_p = jnp.concatenate([w_ukv.reshape(-1, H, dn + dv)[:, :, :dn].reshape(-1, H * dn),
                               w_ukv.reshape(-1, H, dn + dv)[:, :, dn:].reshape(-1, H * dv)], -1)
    cos_q, sin_q = _rope_tables(seq, dr, H * (dn + dr), [(H * dn, H * (dn + dr))])
    q = _proj(z_lat, w_uq_p.astype(BF), k_block=0, k_size=MLA_Q_RANK, rms_g=q_norm,
              rope=(cos_q, sin_q, dr // 2), seq=seq, out_dtype=BF, name="mla_q_up")
    kv = _proj(z_lat, w_ukv_p.astype(BF), k_block=1, k_size=MLA_KV_RANK, rms_g=kv_norm, out_dtype=BF,
               name="mla_kv_up")
    q_mla = jnp.concatenate([q[:, :H * dn].reshape(batch, seq, H, dn),
                             q[:, H * dn:].reshape(batch, seq, H, dr)], -1)
    k_pe = z[:, :dr].reshape(batch, seq, 1, dr)
    k_mla = jnp.concatenate([kv[:, :H * dn].reshape(batch, seq, H, dn),
                             jnp.broadcast_to(k_pe, (batch, seq, H, dr))], -1)
    v_mla = kv[:, H * dn:].reshape(batch, seq, H, dv)
    q_mla = q_mla.transpose(0, 2, 1, 3).reshape(batch * H, 1, seq, dn + dr)
    k_mla = k_mla.transpose(0, 2, 1, 3).reshape(batch * H, seq, dn + dr)
    v_mla = v_mla.transpose(0, 2, 1, 3).reshape(batch * H, seq, dv)
    o_mla = _attention(q_mla, k_mla, v_mla, scale=(dn + dr) ** -0.5, window=seq,
                       tq=_pick(seq, (512, 256, 128)), tk=_pick(seq, (256, 128)), out_batch=batch, name="mla_attn")

    q_s = z[:, o_qs:o_ks].reshape(batch, seq, G, R, ds).transpose(0, 2, 3, 1, 4)
    k_s = z[:, o_ks:o_vs].reshape(batch, seq, G, ds).transpose(0, 2, 1, 3)
    v_s = z[:, o_vs:o_vs + G * ds].reshape(batch, seq, G, ds).transpose(0, 2, 1, 3)
    o_swa = _attention(q_s.reshape(batch * G, R, seq, ds), k_s.reshape(batch * G, seq, ds),
                       v_s.reshape(batch * G, seq, ds), scale=ds ** -0.5, window=SWA_WINDOW, tq=128, tk=128,
                       sinks=sinks.reshape(G, R), out_batch=batch, name="swa_attn")
    return jnp.concatenate([o_mla, o_swa], -1).reshape(batch * seq, -1)


def _odd_mixer(x16, batch, seq, w_in, gate_b, cmp_pos_k, cmp_pos_v, k_w1, k_w2, v_w1, v_w2):
    G, d = NSA_KV_GROUPS, NSA_HEAD_DIM
    H = NSA_HEADS
    R = H // G
    kvw = G * d
    tq = 128
    tk = _pick(seq, (256, 128))
    o_kc = H * d
    offs = [o_kc + i * kvw for i in range(7)]
    cos, sin = _rope_tables(seq, d, offs[6], [(0, offs[1]), (offs[2], offs[3]), (offs[4], offs[5])])
    z = _proj(x16, w_in[:, :offs[6]].astype(BF), rope=(cos, sin, d // 2), seq=seq, out_dtype=BF, name="odd_in")
    gl = _proj(x16, _pad_cols(w_in[:, offs[6]:], LANE).astype(BF), name="odd_in_gate")[:, :3 * H]
    gl = gl.reshape(batch, seq, G, 3 * R).transpose(0, 2, 1, 3).reshape(batch * G, seq, 3 * R)
    gb = gate_b.astype(F32).reshape(G, 1, 3 * R)

    def heads_kv(a, dtype=BF):
        return a.astype(dtype).reshape(batch, seq, G, d).transpose(0, 2, 1, 3).reshape(batch * G, seq, d)

    q = z[:, :o_kc].reshape(batch, seq, G, R, d).transpose(0, 2, 3, 1, 4).reshape(batch * G, R, seq, d)
    kc, vc, ks, vs, kw, vw = (z[:, offs[i]:offs[i + 1]] for i in range(6))
    scale = d ** -0.5
    nch = seq // CMP_STRIDE
    kcmp = _compress(heads_kv(kc, F32).reshape(batch * G, nch, CMP_STRIDE * d), cmp_pos_k, k_w1, k_w2,
                     "nsa_compress_k")
    vcmp = _compress(heads_kv(vc, F32).reshape(batch * G, nch, CMP_STRIDE * d), cmp_pos_v, v_w1, v_w2,
                     "nsa_compress_v")
    o, sel = _cmp_attention(q, kcmp, vcmp, gl, gb, scale=scale, tq=tq)
    o = _attention(q, heads_kv(ks), heads_kv(vs), scale=scale, window=seq, tq=tq, tk=tk, sel=sel,
                   gate=(gl, gb, 1), add=o, name="nsa_slc_attn")
    o = _attention(q, heads_kv(kw), heads_kv(vw), scale=scale, window=NSA_WINDOW, tq=tq, tk=tk,
                   gate=(gl, gb, 2), add=o, out_batch=batch, name="nsa_win_attn")
    return o.reshape(batch * seq, H * d)


def _ffn(x32, x16, wg, wu, wd, g, b, alpha):
    h = _ffn_up(x16, wg, wu)
    return _mm_res_ln(h, wd.astype(BF), x32, g, b, alpha=alpha, coef=0.5, name="ffn_down_ln")


def kernel(x, ln_g, ln_b, ffn_w_gate, ffn_w_up, ffn_w_down, even_w_in, mla_q_norm, mla_w_uq, mla_kv_norm, mla_w_ukv, swa_sinks, even_w_out, odd_w_in, nsa_gate_b, nsa_cmp_pos_k, nsa_cmp_pos_v, nsa_cmp_k_w1, nsa_cmp_k_w2, nsa_cmp_v_w1, nsa_cmp_v_w2, odd_w_out):
    batch, seq, dm = x.shape
    depth = ln_g.shape[0]
    alpha = float((2 * depth) ** 0.25)
    x32 = x.reshape(batch * seq, dm)
    x16 = x32.astype(BF)
    for l in range(depth):
        x32, x16 = _ffn(x32, x16, ffn_w_gate[l, 0], ffn_w_up[l, 0], ffn_w_down[l, 0], ln_g[l, 0], ln_b[l, 0], alpha)
        j = l // 2
        if l % 2 == 0:
            o = _even_mixer(x16, batch, seq, even_w_in[j], mla_q_norm[j], mla_w_uq[j], mla_kv_norm[j],
                            mla_w_ukv[j], swa_sinks[j])
            w_out = even_w_out[j]
        else:
            o = _odd_mixer(x16, batch, seq, odd_w_in[j], nsa_gate_b[j], nsa_cmp_pos_k[j], nsa_cmp_pos_v[j],
                           nsa_cmp_k_w1[j], nsa_cmp_k_w2[j], nsa_cmp_v_w1[j], nsa_cmp_v_w2[j])
            w_out = odd_w_out[j]
        x32, x16 = _mm_res_ln(o, w_out.astype(BF), x32, ln_g[l, 1], ln_b[l, 1], alpha=alpha, coef=1.0,
                              name="mixer_out_ln")
        x32, x16 = _ffn(x32, x16, ffn_w_gate[l, 1], ffn_w_up[l, 1], ffn_w_down[l, 1], ln_g[l, 2], ln_b[l, 2], alpha)
    return x32.reshape(batch, seq, dm)
```

```python
import functools
import math

import numpy as np
import jax
import jax.numpy as jnp
from jax import lax
from jax.experimental import pallas as pl
from jax.experimental.pallas import tpu as pltpu

F32 = jnp.float32
BF = jnp.bfloat16

ROPE_THETA = 10000.0
LN_EPS = 1e-5
RMS_EPS = 1e-6
NEG = -1e30
FORCE_SCORE = 1e9
LOG2E = math.log2(math.e)

MLA_HEADS, MLA_NOPE, MLA_ROPE, MLA_V = 8, 128, 64, 128
MLA_Q_RANK, MLA_KV_RANK = 512, 512
SWA_HEADS, SWA_KV_HEADS, SWA_HEAD_DIM, SWA_WINDOW = 16, 2, 64, 128
NSA_HEADS, NSA_KV_GROUPS, NSA_HEAD_DIM = 16, 2, 128
CMP_BLOCK, CMP_STRIDE, CMP_HIDDEN = 32, 16, 256
SLC_BLOCK, N_SEL, NSA_WINDOW = 64, 8, 512

LANE = 128
UNIT_ROWS = 128
VMEM_LIMIT_BYTES = 52 * 1024 * 1024


def _params(*sem):
    return pltpu.CompilerParams(dimension_semantics=sem, vmem_limit_bytes=VMEM_LIMIT_BYTES)


def _pick(n, candidates):
    for c in candidates:
        if n % c == 0:
            return c
    raise ValueError(f"no tile in {candidates} divides {n}")


def _lane_tile(x, n):
    return x if n == 1 else jnp.concatenate([x] * n, axis=1)


def _proj_kernel(*refs, has_rms, has_rope, half):
    it = iter(refs)
    x_ref, w_ref = next(it), next(it)
    g_ref = next(it) if has_rms else None
    cos_ref, sin_ref = (next(it), next(it)) if has_rope else (None, None)
    o_ref = next(it)
    x = x_ref[...]
    if has_rms:
        xf = x.astype(F32)
        x = xf * lax.rsqrt(jnp.mean(xf * xf, -1, keepdims=True) + RMS_EPS) * g_ref[...]
    acc = jnp.dot(x.astype(BF), w_ref[...], preferred_element_type=F32)
    if has_rope:
        n = acc.shape[-1]
        nxt = pltpu.roll(acc, n - half, 1)
        prv = pltpu.roll(acc, half, 1)
        lane = lax.broadcasted_iota(jnp.int32, acc.shape, 1)
        partner = jnp.where(lane % (2 * half) < half, nxt, prv)
        acc = acc * cos_ref[...] + partner * sin_ref[...]
    o_ref[...] = acc.astype(o_ref.dtype)


def _proj(x, w, *, k_block=0, k_size=None, rms_g=None, rope=None, seq=None, out_dtype=F32, name="proj"):
    m = x.shape[0]
    k_size = x.shape[1] if k_size is None else k_size
    n = w.shape[1]
    tm = _pick(m, (512, 256, 128))
    tn = _pick(n, (768, 640, 512, 384, 256, 128))
    in_specs = [pl.BlockSpec((tm, k_size), lambda i, j: (i, k_block)),
                pl.BlockSpec((k_size, tn), lambda i, j: (0, j))]
    args = [x, w]
    if rms_g is not None:
        in_specs.append(pl.BlockSpec((1, k_size), lambda i, j: (0, 0)))
        args.append(rms_g.reshape(1, k_size).astype(F32))
    half = 0
    if rope is not None:
        cos, sin, half = rope
        sb = seq // tm
        in_specs += [pl.BlockSpec((tm, tn), lambda i, j: (i % sb, j))] * 2
        args += [cos, sin]
    return pl.pallas_call(
        functools.partial(_proj_kernel, has_rms=rms_g is not None, has_rope=rope is not None, half=half),
        grid=(m // tm, n // tn),
        in_specs=in_specs,
        out_specs=pl.BlockSpec((tm, tn), lambda i, j: (i, j)),
        out_shape=jax.ShapeDtypeStruct((m, n), out_dtype),
        compiler_params=_params("parallel", "parallel"),
        name=name,
    )(*args)


def _ffn_up_kernel(x_ref, wg_ref, wu_ref, o_ref, wg16, wu16):
    @pl.when(pl.program_id(1) == 0)
    def _():
        wg16[...] = wg_ref[...].astype(BF)
        wu16[...] = wu_ref[...].astype(BF)

    x = x_ref[...]
    g = jnp.dot(x, wg16[...], preferred_element_type=F32)
    u = jnp.dot(x, wu16[...], preferred_element_type=F32)
    o_ref[...] = (g * jax.nn.sigmoid(g) * u).astype(o_ref.dtype)


def _ffn_up(x16, wg_all, wu_all, layer, idx):
    m, k = x16.shape
    n = wg_all.shape[-1]
    tm = _pick(m, (1024, 512, 256, 128))
    tn = _pick(n, (512, 256, 128))
    w_spec = pl.BlockSpec((None, None, k, tn), lambda j, i: (layer, idx, 0, j))
    return pl.pallas_call(
        _ffn_up_kernel,
        grid=(n // tn, m // tm),
        in_specs=[pl.BlockSpec((tm, k), lambda j, i: (i, 0)), w_spec, w_spec],
        out_specs=pl.BlockSpec((tm, tn), lambda j, i: (i, j)),
        out_shape=jax.ShapeDtypeStruct((m, n), BF),
        scratch_shapes=[pltpu.VMEM((k, tn), BF), pltpu.VMEM((k, tn), BF)],
        compiler_params=_params("parallel", "arbitrary"),
        name="ffn_up",
    )(x16, wg_all, wu_all)


def _mm_res_ln_kernel(x_ref, w_ref, res_ref, g_ref, b_ref, o32_ref, o16_ref, acc_ref, *, nk, alpha, coef):
    k = pl.program_id(1)

    @pl.when(k == 0)
    def _():
        acc_ref[...] = jnp.zeros_like(acc_ref)

    acc_ref[...] += jnp.dot(x_ref[...], w_ref[...], preferred_element_type=F32)

    @pl.when(k == nk - 1)
    def _():
        y = alpha * res_ref[...] + coef * acc_ref[...]
        mu = jnp.mean(y, -1, keepdims=True)
        yc = y - mu
        var = jnp.mean(yc * yc, -1, keepdims=True)
        out = yc * lax.rsqrt(var + LN_EPS) * g_ref[...] + b_ref[...]
        o32_ref[...] = out
        o16_ref[...] = out.astype(BF)


def _mm_res_ln(x16, w_all, w_idx, res, g, b, *, alpha, coef, name):
    m, k = x16.shape
    n = w_all.shape[-1]
    tm = _pick(m, (512, 256, 128))
    tk = _pick(k, (1408, 1024, 512, 256, 128))
    nk = k // tk
    lead = (None,) * len(w_idx)
    return pl.pallas_call(
        functools.partial(_mm_res_ln_kernel, nk=nk, alpha=alpha, coef=coef),
        grid=(m // tm, nk),
        in_specs=[pl.BlockSpec((tm, tk), lambda i, kk: (i, kk)),
                  pl.BlockSpec(lead + (tk, n), lambda i, kk: tuple(w_idx) + (kk, 0)),
                  pl.BlockSpec((tm, n), lambda i, kk: (i, 0)),
                  pl.BlockSpec((1, n), lambda i, kk: (0, 0)),
                  pl.BlockSpec((1, n), lambda i, kk: (0, 0))],
        out_specs=[pl.BlockSpec((tm, n), lambda i, kk: (i, 0)),
                   pl.BlockSpec((tm, n), lambda i, kk: (i, 0))],
        out_shape=[jax.ShapeDtypeStruct((m, n), F32), jax.ShapeDtypeStruct((m, n), BF)],
        scratch_shapes=[pltpu.VMEM((tm, n), F32)],
        compiler_params=_params("parallel", "arbitrary"),
        name=name,
    )(x16, w_all, res, g.reshape(1, n).astype(F32), b.reshape(1, n).astype(F32))


def _attn_kernel(qi_ref, kj_ref, fst_ref, lst_ref, *refs, variant, n_grp, n_rep, tq, tk, rc, dv, c_exp, window,
                 seq, has_sel, has_sink, gate_col, has_add):
    it = iter(refs)
    if variant == "mla":
        qn_ref, qr_ref, kn_ref, kp_ref, v_ref = (next(it) for _ in range(5))
    elif variant == "swa":
        q_ref, kv_ref = next(it), next(it)
    else:
        q_ref, k_ref, v_ref = next(it), next(it), next(it)
    sel_ref = next(it) if has_sel else None
    sink_ref = next(it) if has_sink else None
    gl_ref, gb_ref = (next(it), next(it)) if gate_col is not None else (None, None)
    add_ref = next(it) if has_add else None
    o_ref = next(it)
    q_sc, m_sc, acc_sc = next(it), next(it), next(it)
    p = pl.program_id(1)
    qi = qi_ref[p]
    kj = kj_ref[p]
    rows = n_rep * tq

    @pl.when(fst_ref[p] == 1)
    def _():
        if has_sink:
            m_sc[...] = sink_ref[0]
            lane = lax.broadcasted_iota(jnp.int32, acc_sc.shape, 1)
            acc_sc[...] = jnp.where(lane >= dv, 1.0, 0.0)
        else:
            m_sc[...] = jnp.full_like(m_sc, NEG)
            acc_sc[...] = jnp.zeros_like(acc_sc)
        if variant == "mla":
            half_of_lane = lax.broadcasted_iota(jnp.int32, (tq, LANE), 1) // MLA_ROPE
            mine = half_of_lane == (pl.program_id(0) % n_grp) % 2
            q_sc[:, :LANE] = qn_ref[...]
            q_sc[:, LANE:] = jnp.where(mine, qr_ref[...], jnp.zeros_like(qr_ref[...]))
        elif variant == "swa":
            low = lax.broadcasted_iota(jnp.int32, (tq, LANE), 1) < dv
            for r in range(n_rep):
                chunk = q_ref[:, (r // 2) * LANE:(r // 2 + 1) * LANE].astype(F32)
                if r % 2 == 1:
                    chunk = pltpu.roll(chunk, dv, 1)
                q_sc[r * tq:(r + 1) * tq, :] = jnp.where(low, chunk, 0.0).astype(BF)
        else:
            dq = q_sc.shape[1]
            for r in range(n_rep):
                q_sc[r * tq:(r + 1) * tq, :] = q_ref[:, r * dq:(r + 1) * dq]

    if variant == "mla":
        k = jnp.concatenate([kn_ref[...], kp_ref[...]], axis=1)
        v_ext = jnp.concatenate([v_ref[...], jnp.ones((tk, dv), BF)], axis=1)
    elif variant == "swa":
        k = kv_ref[...]
        swapped = pltpu.roll(kv_ref[...].astype(F32), dv, 1)
        v_ext = jnp.where(lax.broadcasted_iota(jnp.int32, (tk, LANE), 1) < dv, swapped, 1.0).astype(BF)
    else:
        k = k_ref[...]
        v_ext = jnp.concatenate([v_ref[...], jnp.ones((tk, dv), BF)], axis=1)
    row = lax.broadcasted_iota(jnp.int32, (tq, tk), 0)
    col = lax.broadcasted_iota(jnp.int32, (tq, tk), 1)
    dist = qi * tq - kj * tk + row - col
    mask = dist >= 0
    if window < seq:
        mask = mask & (dist < window)
    if has_sel:
        n_blk = sel_ref.shape[-1]
        blk_of_key = (kj * tk + lax.broadcasted_iota(jnp.int32, (n_blk, tk), 1)) // SLC_BLOCK
        expand = jnp.where(lax.broadcasted_iota(jnp.int32, (n_blk, tk), 0) == blk_of_key, 1.0, 0.0).astype(BF)
        picked = jnp.dot(sel_ref[0].astype(BF), expand, preferred_element_type=F32)
        mask = mask & (picked > 0.5)
    bias = jnp.where(mask, 0.0, NEG)

    def logits(u):
        return lax.dot_general(q_sc[u * rc:(u + 1) * rc, :], k, (((1,), (1,)), ((), ())),
                               preferred_element_type=F32)

    n_units = rows // rc
    s_next = logits(0)
    for u in range(n_units):
        s = s_next
        if u + 1 < n_units:
            s_next = logits(u + 1)
        us = slice(u * rc, (u + 1) * rc)
        b_u = bias[(u * rc) % tq:(u * rc) % tq + rc]
        m_prev = m_sc[us, :]
        m_new = jnp.maximum(m_prev, jnp.max(s + b_u, -1, keepdims=True))
        m_sc[us, :] = m_new
        alpha = jnp.exp2((m_prev - m_new) * c_exp)
        w = jnp.exp2(((s - _lane_tile(m_new, tk // LANE)) + b_u) * c_exp).astype(BF)
        acc_sc[us, :] = (_lane_tile(alpha, acc_sc.shape[1] // LANE) * acc_sc[us, :]
                         + jnp.dot(w, v_ext, preferred_element_type=F32))

    @pl.when(lst_ref[p] == 1)
    def _():
        acc = acc_sc[...]
        if dv % LANE == 0:
            o = acc[:, :dv] / acc[:, dv:]
        else:
            o = acc[:, :dv] / acc[:, dv:dv + 1]
        if gate_col is not None:
            gates = jax.nn.sigmoid(gl_ref[...] + gb_ref[...])
        for r in range(n_rep):
            o_r = o[r * tq:(r + 1) * tq]
            if gate_col is not None:
                o_r = o_r * gates[:, 3 * r + gate_col:3 * r + gate_col + 1]
            if has_add:
                o_r = o_r + add_ref[:, r * dv:(r + 1) * dv].astype(F32)
            o_ref[:, r * dv:(r + 1) * dv] = o_r.astype(o_ref.dtype)


def _attn_call(variant, qkv_args, qkv_specs, *, batch, seq, n_grp, n_rep, dq, dv, scale, window, tq, tk,
               sel=None, sinks=None, gate=None, add=None, name="attn"):
    rows = n_rep * tq
    nq, nk = seq // tq, seq // tk
    pairs, fst, lst = [], [], []
    for i in range(nq):
        lo = max(0, i * tq - (window - 1)) // tk
        hi = (i * tq + tq - 1) // tk
        for j in range(lo, hi + 1):
            pairs.append((i, j))
            fst.append(int(j == lo))
            lst.append(int(j == hi))
    qi = jnp.asarray([a for a, _ in pairs], jnp.int32)
    kj = jnp.asarray([b for _, b in pairs], jnp.int32)

    def q_rows(b, p, qi, kj):
        return (b // n_grp) * nq + qi[p]

    def k_rows(b, p, qi, kj):
        return (b // n_grp) * nk + kj[p]

    in_specs = [mk(q_rows, k_rows) for mk in qkv_specs]
    args = list(qkv_args)
    if sel is not None:
        in_specs.append(pl.BlockSpec((1, tq, sel.shape[-1]), lambda b, p, qi, kj, f, l: (b, qi[p], 0)))
        args.append(sel)
    if sinks is not None:
        m0 = jnp.broadcast_to((sinks.astype(F32) / scale)[:, :, None, None], (n_grp, n_rep, tq, LANE))
        in_specs.append(pl.BlockSpec((1, rows, LANE), lambda b, p, qi, kj, f, l: (b % n_grp, 0, 0)))
        args.append(m0.reshape(n_grp, rows, LANE))
    gate_col = None
    if gate is not None:
        gl, gb, gate_col = gate
        in_specs += [pl.BlockSpec((tq, LANE), lambda b, p, qi, kj, f, l: (q_rows(b, p, qi, kj), b % n_grp)),
                     pl.BlockSpec((1, LANE), lambda b, p, qi, kj, f, l: (0, b % n_grp))]
        args += [gl, gb]
    out_spec = pl.BlockSpec((tq, n_rep * dv), lambda b, p, qi, kj, f, l: (q_rows(b, p, qi, kj), b % n_grp))
    if add is not None:
        in_specs.append(out_spec)
        args.append(add)
    rc = min(tq, UNIT_ROWS)
    kern = functools.partial(_attn_kernel, variant=variant, n_grp=n_grp, n_rep=n_rep, tq=tq, tk=tk, rc=rc, dv=dv,
                             c_exp=scale * LOG2E, window=window, seq=seq, has_sel=sel is not None,
                             has_sink=sinks is not None, gate_col=gate_col, has_add=add is not None)
    return pl.pallas_call(
        kern,
        grid_spec=pltpu.PrefetchScalarGridSpec(
            num_scalar_prefetch=4,
            grid=(batch * n_grp, len(pairs)),
            in_specs=in_specs,
            out_specs=out_spec,
            scratch_shapes=[pltpu.VMEM((rows, dq), BF), pltpu.VMEM((rows, LANE), F32),
                            pltpu.VMEM((rows, 2 * dv), F32)]),
        out_shape=jax.ShapeDtypeStruct((batch * seq, n_grp * n_rep * dv), BF),
        compiler_params=_params("parallel", "arbitrary"),
        name=name,
    )(qi, kj, jnp.asarray(fst, jnp.int32), jnp.asarray(lst, jnp.int32), *args)


def _spec(shape, rows_of, col_fn):
    def make(q_rows, k_rows):
        rf = q_rows if rows_of == "q" else k_rows
        return pl.BlockSpec(shape, lambda b, p, qi, kj, f, l: (rf(b, p, qi, kj), col_fn(b)))
    return make


def _compress_kernel(x_ref, pa_ref, pb_ref, w1a_ref, w1b_ref, w2_ref, o_ref):
    x = x_ref[0]
    a = jnp.dot((x + pa_ref[...]).astype(BF), w1a_ref[...], preferred_element_type=F32)
    b = jnp.dot((x + pb_ref[...]).astype(BF), w1b_ref[...], preferred_element_type=F32)
    h = jax.nn.gelu(a + pltpu.roll(b, b.shape[0] - 1, 0))
    o_ref[0] = jnp.dot(h.astype(BF), w2_ref[...], preferred_element_type=F32).astype(o_ref.dtype)


def _compress(x, pos, w1, w2, name):
    bg, nch, width = x.shape
    d = w2.shape[1]
    hid = w1.shape[1]
    pa = pos[:CMP_STRIDE].reshape(1, width).astype(F32)
    pb = pos[CMP_STRIDE:].reshape(1, width).astype(F32)
    w1 = w1.astype(BF)
    return pl.pallas_call(
        _compress_kernel,
        grid=(bg,),
        in_specs=[pl.BlockSpec((1, nch, width), lambda b: (b, 0, 0)),
                  pl.BlockSpec((1, width), lambda b: (0, 0)),
                  pl.BlockSpec((1, width), lambda b: (0, 0)),
                  pl.BlockSpec((width, hid), lambda b: (0, 0)),
                  pl.BlockSpec((width, hid), lambda b: (1, 0)),
                  pl.BlockSpec((hid, d), lambda b: (0, 0))],
        out_specs=pl.BlockSpec((1, nch, d), lambda b: (b, 0, 0)),
        out_shape=jax.ShapeDtypeStruct((bg, nch, d), BF),
        compiler_params=_params("parallel"),
        name=name,
    )(x, pa, pb, w1, w1, w2.astype(BF))


def _cmp_attn_kernel(q_ref, k_ref, v_ref, ovt_ref, gl_ref, gb_ref, o_ref, sel_ref, *, n_rep, tq, c_exp):
    qi = pl.program_id(1)
    nc, d = k_ref.shape[1], k_ref.shape[2]
    rows = n_rep * tq
    q = jnp.concatenate([q_ref[:, r * d:(r + 1) * d] for r in range(n_rep)], axis=0)
    s = lax.dot_general(q, k_ref[0], (((1,), (1,)), ((), ())), preferred_element_type=F32)
    qpos = qi * tq + lax.broadcasted_iota(jnp.int32, (tq, nc), 0)
    cmp_end = lax.broadcasted_iota(jnp.int32, (tq, nc), 1) * CMP_STRIDE + (CMP_BLOCK - 1)
    cmask = (cmp_end <= qpos)[None]
    s = jnp.where(cmask, s.reshape(n_rep, tq, nc), NEG)
    e = jnp.exp2((s - jnp.max(s, -1, keepdims=True)) * c_exp)
    e = jnp.where(cmask, e, 0.0).reshape(rows, nc)
    ext = jnp.dot(e.astype(BF), v_ref[0], preferred_element_type=F32)
    den = ext[:, d:]
    inv = 1.0 / jnp.where(den > 0.0, den, 1.0)
    gates = jax.nn.sigmoid(gl_ref[...] + gb_ref[...])
    o = ext[:, :d] * inv
    for r in range(n_rep):
        o_ref[:, r * d:(r + 1) * d] = (o[r * tq:(r + 1) * tq] * gates[:, 3 * r:3 * r + 1]).astype(o_ref.dtype)
    inv_c = inv[:, :nc] if nc <= d else _lane_tile(inv, nc // d)
    prob = e * inv_c
    psum = jnp.sum(prob.reshape(n_rep, tq, nc), axis=0)
    p_hi = psum.astype(BF)
    p_lo = (psum - p_hi.astype(F32)).astype(BF)
    contract_last = (((1,), (1,)), ((), ()))
    imp = (lax.dot_general(ovt_ref[...], p_hi, contract_last, preferred_element_type=F32)
           + lax.dot_general(ovt_ref[...], p_lo, contract_last, preferred_element_type=F32))
    ns = imp.shape[0]
    blk = lax.broadcasted_iota(jnp.int32, (ns, tq), 0)
    cur = (qi * tq + lax.broadcasted_iota(jnp.int32, (ns, tq), 1)) // SLC_BLOCK
    eligible = blk <= cur
    forced = (blk == 0) | (blk == cur) | (blk == cur - 1)
    score = jnp.where(eligible, jnp.where(forced, FORCE_SCORE, imp), -1.0)
    beaten_by = jnp.zeros((ns, tq), F32)
    for kk in range(ns):
        sk = score[kk:kk + 1, :]
        beats = (sk > score) | ((sk == score) & (blk > kk))
        beaten_by = beaten_by + jnp.where(beats, 1.0, 0.0)
    n_sel = min(N_SEL, ns)
    sel_t = jnp.where((beaten_by < n_sel) & (score >= 0.0), 1.0, 0.0)
    sel_sq = jnp.concatenate([sel_t, jnp.zeros((tq - ns, tq), F32)], axis=0) if ns < tq else sel_t
    sel_ref[0] = sel_sq.T[:, :ns]


def _cmp_attention(z, kcmp, vcmp, gl, gb, *, batch, seq, n_grp, n_rep, scale, tq):
    bg, nc, d = kcmp.shape
    ns = seq // SLC_BLOCK
    nq = seq // tq
    starts = np.arange(nc) * CMP_STRIDE
    jb = np.arange(ns)
    overlap = ((starts[:, None] < (jb[None, :] + 1) * SLC_BLOCK)
               & (starts[:, None] + CMP_BLOCK > jb[None, :] * SLC_BLOCK)).astype(np.float32)
    v_ext = jnp.concatenate([vcmp, jnp.ones_like(vcmp)], -1)
    tok_spec = pl.BlockSpec((tq, n_rep * d), lambda b, i: ((b // n_grp) * nq + i, b % n_grp))
    return pl.pallas_call(
        functools.partial(_cmp_attn_kernel, n_rep=n_rep, tq=tq, c_exp=scale * LOG2E),
        grid=(bg, nq),
        in_specs=[tok_spec,
                  pl.BlockSpec((1, nc, d), lambda b, i: (b, 0, 0)),
                  pl.BlockSpec((1, nc, 2 * d), lambda b, i: (b, 0, 0)),
                  pl.BlockSpec((ns, nc), lambda b, i: (0, 0)),
                  pl.BlockSpec((tq, LANE), lambda b, i: ((b // n_grp) * nq + i, b % n_grp)),
                  pl.BlockSpec((1, LANE), lambda b, i: (0, b % n_grp))],
        out_specs=[tok_spec, pl.BlockSpec((1, tq, ns), lambda b, i: (b, i, 0))],
        out_shape=[jax.ShapeDtypeStruct((batch * seq, n_grp * n_rep * d), BF),
                   jax.ShapeDtypeStruct((bg, seq, ns), F32)],
        compiler_params=_params("parallel", "parallel"),
        name="nsa_cmp_attn",
    )(z, kcmp, v_ext, jnp.asarray(overlap.T, BF), gl, gb)


def _rope_tables(seq, dim, n_cols, rope_cols):
    inv = 1.0 / (ROPE_THETA ** (jnp.arange(0, dim, 2, dtype=F32) / dim))
    ang = jnp.arange(seq, dtype=F32)[:, None] * inv[None, :]
    cos_h = jnp.concatenate([jnp.cos(ang), jnp.cos(ang)], -1)
    sin_h = jnp.concatenate([-jnp.sin(ang), jnp.sin(ang)], -1)
    cos = jnp.ones((seq, n_cols), F32)
    sin = jnp.zeros((seq, n_cols), F32)
    for a, b in rope_cols:
        reps = (b - a) // dim
        cos = cos.at[:, a:b].set(jnp.tile(cos_h, (1, reps)))
        sin = sin.at[:, a:b].set(jnp.tile(sin_h, (1, reps)))
    return cos, sin


def _pad_cols(w, n):
    return jnp.pad(w, ((0, 0), (0, n - w.shape[1])))


def _even_mixer(x16, batch, seq, w_in, q_norm, w_uq, kv_norm, w_ukv, sinks):
    H, dn, dr, dv = MLA_HEADS, MLA_NOPE, MLA_ROPE, MLA_V
    G, R, ds = SWA_KV_HEADS, SWA_HEADS // SWA_KV_HEADS, SWA_HEAD_DIM
    n_lat = MLA_Q_RANK + MLA_KV_RANK
    z_lat = _proj(x16, w_in[:, :n_lat].astype(BF), name="even_in_latent")
    o_qs = n_lat + dr
    o_ks = o_qs + SWA_HEADS * ds
    o_vs = o_ks + G * ds
    w_kpe = w_in[:, n_lat:o_qs]
    cols = [w_in[:, o_qs:o_ks]]
    for g in range(G):
        cols += [w_in[:, o_ks + g * ds:o_ks + (g + 1) * ds], w_in[:, o_vs + g * ds:o_vs + (g + 1) * ds]]
    cols += [w_kpe, w_kpe]
    w_rg = jnp.concatenate(cols, -1)
    n_q = SWA_HEADS * ds
    n_rg = w_rg.shape[1]
    n_pad = -(-n_rg // 512) * 512
    rope_cols = [(0, n_q)] + [(n_q + 2 * g * ds, n_q + (2 * g + 1) * ds) for g in range(G)] + [(n_q + 2 * G * ds, n_rg)]
    cos, sin = _rope_tables(seq, ds, n_pad, rope_cols)
    z = _proj(x16, _pad_cols(w_rg, n_pad).astype(BF), rope=(cos, sin, ds // 2), seq=seq,
              out_dtype=BF, name="even_in_rope")

    w_uq_p = jnp.concatenate([w_uq.reshape(-1, H, dn + dr)[:, :, :dn].reshape(-1, H * dn),
                              w_uq.reshape(-1, H, dn + dr)[:, :, dn:].reshape(-1, H * dr)], -1)
    w_ukv_p = jnp.concatenate([w_ukv.reshape(-1, H, dn + dv)[:, :, :dn].reshape(-1, H * dn),
                               w_ukv.reshape(-1, H, dn + dv)[:, :, dn:].reshape(-1, H * dv)], -1)
    cos_q, sin_q = _rope_tables(seq, dr, H * (dn + dr), [(H * dn, H * (dn + dr))])
    q = _proj(z_lat, w_uq_p.astype(BF), k_block=0, k_size=MLA_Q_RANK, rms_g=q_norm,
              rope=(cos_q, sin_q, dr // 2), seq=seq, out_dtype=BF, name="mla_q_up")
    kv = _proj(z_lat, w_ukv_p.astype(BF), k_block=1, k_size=MLA_KV_RANK, rms_g=kv_norm, out_dtype=BF,
               name="mla_kv_up")
    tq = _pick(seq, (512, 256, 128))
    tk = _pick(seq, (512, 256, 128))
    kpe_blk = (n_q + 2 * G * ds) // LANE
    o_mla = _attn_call(
        "mla", [q, q, kv, z, kv],
        [_spec((tq, LANE), "q", lambda b: b % H),
         _spec((tq, LANE), "q", lambda b: H * dn // LANE + (b % H) // 2),
         _spec((tk, LANE), "k", lambda b: b % H),
         _spec((tk, LANE), "k", lambda b: kpe_blk),
         _spec((tk, LANE), "k", lambda b: H * dn // LANE + b % H)],
        batch=batch, seq=seq, n_grp=H, n_rep=1, dq=2 * LANE, dv=dv, scale=(dn + dr) ** -0.5, window=seq,
        tq=tq, tk=tk, name="mla_attn")

    tq = _pick(seq, (256, 128))
    o_swa = _attn_call(
        "swa", [z, z],
        [_spec((tq, R * ds), "q", lambda b: b % G),
         _spec((tq, LANE), "k", lambda b: n_q // LANE + b % G)],
        batch=batch, seq=seq, n_grp=G, n_rep=R, dq=LANE, dv=ds, scale=ds ** -0.5, window=SWA_WINDOW,
        tq=tq, tk=tq, sinks=sinks.reshape(G, R), name="swa_attn")
    return jnp.concatenate([o_mla, o_swa], -1)


def _odd_mixer(x16, batch, seq, w_in, gate_b, cmp_pos_k, cmp_pos_v, k_w1, k_w2, v_w1, v_w2):
    G, d = NSA_KV_GROUPS, NSA_HEAD_DIM
    H = NSA_HEADS
    R = H // G
    kvw = G * d
    o_kc = H * d
    offs = [o_kc + i * kvw for i in range(7)]
    cos, sin = _rope_tables(seq, d, offs[6], [(0, offs[1]), (offs[2], offs[3]), (offs[4], offs[5])])
    z = _proj(x16, w_in[:, :offs[6]].astype(BF), rope=(cos, sin, d // 2), seq=seq, out_dtype=BF, name="odd_in")
    w_gl = jnp.concatenate([_pad_cols(w_in[:, offs[6] + g * 3 * R:offs[6] + (g + 1) * 3 * R], LANE)
                            for g in range(G)], -1)
    gl = _proj(x16, w_gl.astype(BF), name="odd_in_gate")
    gb = jnp.concatenate([jnp.pad(gate_b[g * 3 * R:(g + 1) * 3 * R].astype(F32), (0, LANE - 3 * R))
                          for g in range(G)]).reshape(1, G * LANE)

    def chunks(a):
        a = a.astype(F32).reshape(batch, seq // CMP_STRIDE, CMP_STRIDE, G, d).transpose(0, 3, 1, 2, 4)
        return a.reshape(batch * G, seq // CMP_STRIDE, CMP_STRIDE * d)

    scale = d ** -0.5
    kcmp = _compress(chunks(z[:, offs[0]:offs[1]]), cmp_pos_k, k_w1, k_w2, "nsa_compress_k")
    vcmp = _compress(chunks(z[:, offs[1]:offs[2]]), cmp_pos_v, v_w1, v_w2, "nsa_compress_v")
    tq = _pick(seq, (256, 128))
    o, sel = _cmp_attention(z, kcmp, vcmp, gl, gb, batch=batch, seq=seq, n_grp=G, n_rep=R, scale=scale, tq=tq)

    def branch(k_off, v_off, **kw):
        return _attn_call(
            "gqa", [z, z, z],
            [_spec((tq, R * d), "q", lambda b: b % G),
             _spec((kw["tk"], d), "k", lambda b: k_off // d + b % G),
             _spec((kw["tk"], d), "k", lambda b: v_off // d + b % G)],
            batch=batch, seq=seq, n_grp=G, n_rep=R, dq=d, dv=d, scale=scale, tq=tq, **kw)

    o = branch(offs[2], offs[3], window=seq, tk=_pick(seq, (512, 256, 128)), sel=sel, gate=(gl, gb, 1), add=o,
               name="nsa_slc_attn")
    o = branch(offs[4], offs[5], window=NSA_WINDOW, tk=_pick(seq, (256, 128)), gate=(gl, gb, 2), add=o,
               name="nsa_win_attn")
    return o


def kernel(x, ln_g, ln_b, ffn_w_gate, ffn_w_up, ffn_w_down, even_w_in, mla_q_norm, mla_w_uq, mla_kv_norm, mla_w_ukv, swa_sinks, even_w_out, odd_w_in, nsa_gate_b, nsa_cmp_pos_k, nsa_cmp_pos_v, nsa_cmp_k_w1, nsa_cmp_k_w2, nsa_cmp_v_w1, nsa_cmp_v_w2, odd_w_out):
    batch, seq, dm = x.shape
    depth = ln_g.shape[0]
    alpha = float((2 * depth) ** 0.25)
    x32 = x.reshape(batch * seq, dm)
    x16 = x32.astype(BF)
    w_down16 = ffn_w_down.astype(BF)
    even_out16 = even_w_out.astype(BF)
    odd_out16 = odd_w_out.astype(BF)

    def ffn(x32, x16, l, idx):
        h = _ffn_up(x16, ffn_w_gate, ffn_w_up, l, idx)
        return _mm_res_ln(h, w_down16, (l, idx), x32, ln_g[l, 2 * idx], ln_b[l, 2 * idx], alpha=alpha, coef=0.5,
                          name="ffn_down_ln")

    for l in range(depth):
        x32, x16 = ffn(x32, x16, l, 0)
        j = l // 2
        if l % 2 == 0:
            o = _even_mixer(x16, batch, seq, even_w_in[j], mla_q_norm[j], mla_w_uq[j], mla_kv_norm[j],
                            mla_w_ukv[j], swa_sinks[j])
            w_out = even_out16
        else:
            o = _odd_mixer(x16, batch, seq, odd_w_in[j], nsa_gate_b[j], nsa_cmp_pos_k[j], nsa_cmp_pos_v[j],
                           nsa_cmp_k_w1[j], nsa_cmp_k_w2[j], nsa_cmp_v_w1[j], nsa_cmp_v_w2[j])
            w_out = odd_out16
        x32, x16 = _mm_res_ln(o, w_out, (j,), x32, ln_g[l, 1], ln_b[l, 1], alpha=alpha, coef=1.0,
                              name="mixer_out_ln")
        x32, x16 = ffn(x32, x16, l, 1)
    return x32.reshape(batch, seq, dm)
```

```python
import functools
import math

import numpy as np
import jax
import jax.numpy as jnp
from jax import lax
from jax.experimental import pallas as pl
from jax.experimental.pallas import tpu as pltpu

F32 = jnp.float32
BF = jnp.bfloat16

ROPE_THETA = 10000.0
LN_EPS = 1e-5
RMS_EPS = 1e-6
NEG = -1e30
FORCE_SCORE = 1e9
LOG2E = math.log2(math.e)

MLA_HEADS, MLA_NOPE, MLA_ROPE, MLA_V = 8, 128, 64, 128
MLA_Q_RANK, MLA_KV_RANK = 512, 512
SWA_HEADS, SWA_KV_HEADS, SWA_HEAD_DIM, SWA_WINDOW = 16, 2, 64, 128
NSA_HEADS, NSA_KV_GROUPS, NSA_HEAD_DIM = 16, 2, 128
CMP_BLOCK, CMP_STRIDE, CMP_HIDDEN = 32, 16, 256
SLC_BLOCK, N_SEL, NSA_WINDOW = 64, 8, 512

LANE = 128
UNIT_ROWS = 128
VMEM_LIMIT_BYTES = 52 * 1024 * 1024


def _params(*sem):
    return pltpu.CompilerParams(dimension_semantics=sem, vmem_limit_bytes=VMEM_LIMIT_BYTES)


def _pick(n, candidates):
    for c in candidates:
        if n % c == 0:
            return c
    raise ValueError(f"no tile in {candidates} divides {n}")


def _lane_tile(x, n):
    return x if n == 1 else jnp.concatenate([x] * n, axis=1)


def _proj_kernel(*refs, has_rms, groups, half):
    it = iter(refs)
    x_ref, w_ref = next(it), next(it)
    g_ref = next(it) if has_rms else None
    cos_ref, sin_ref = (next(it), next(it)) if groups else (None, None)
    o_ref = next(it)
    x = x_ref[...]
    if has_rms:
        xf = x.astype(F32)
        x = xf * lax.rsqrt(jnp.mean(xf * xf, -1, keepdims=True) + RMS_EPS) * g_ref[...]
    acc = jnp.dot(x.astype(BF), w_ref[...], preferred_element_type=F32)
    if not groups:
        o_ref[...] = acc.astype(o_ref.dtype)
        return
    cos, sin = cos_ref[...], sin_ref[...]
    lane = lax.broadcasted_iota(jnp.int32, cos.shape, 1)
    low = lane < LANE // 2
    cos_l, sin_l = jnp.where(low, cos, 1.0), jnp.where(low, sin, 0.0)
    for c, kind in enumerate(groups):
        a = acc[:, c * LANE:(c + 1) * LANE]
        if kind != "N":
            if 2 * half == LANE:
                partner = pltpu.roll(a, half, 1)
            else:
                partner = jnp.where(lane % (2 * half) < half, pltpu.roll(a, LANE - half, 1), pltpu.roll(a, half, 1))
            a = a * cos + partner * sin if kind == "R" else a * cos_l + partner * sin_l
        o_ref[:, c * LANE:(c + 1) * LANE] = a.astype(o_ref.dtype)


def _proj(x, w, *, k_block=0, k_size=None, rms_g=None, rope=None, seq=None, out_dtype=F32, name="proj"):
    m = x.shape[0]
    k_size = x.shape[1] if k_size is None else k_size
    n = w.shape[1]
    out_bytes = jnp.dtype(out_dtype).itemsize
    for tm in (512, 256, 128):
        est = k_size * n * 2 + 2 * tm * k_size * x.dtype.itemsize + 2 * tm * n * out_bytes + 3 * tm * n * 4
        if m % tm == 0 and est <= VMEM_LIMIT_BYTES * 3 // 4:
            break
    in_specs = [pl.BlockSpec((tm, k_size), lambda i: (i, k_block)),
                pl.BlockSpec((k_size, n), lambda i: (0, 0), pipeline_mode=pl.Buffered(1))]
    args = [x, w]
    if rms_g is not None:
        in_specs.append(pl.BlockSpec((1, k_size), lambda i: (0, 0)))
        args.append(rms_g.reshape(1, k_size).astype(F32))
    half, groups = 0, ()
    if rope is not None:
        cos, sin, half, groups = rope
        sb = seq // tm
        in_specs += [pl.BlockSpec((tm, LANE), lambda i: (i % sb, 0))] * 2
        args += [cos, sin]
    return pl.pallas_call(
        functools.partial(_proj_kernel, has_rms=rms_g is not None, groups=tuple(groups), half=half),
        grid=(m // tm,),
        in_specs=in_specs,
        out_specs=pl.BlockSpec((tm, n), lambda i: (i, 0)),
        out_shape=jax.ShapeDtypeStruct((m, n), out_dtype),
        compiler_params=_params("parallel"),
        name=name,
    )(*args)


def _ffn_up_kernel(x_ref, wg_ref, wu_ref, o_ref, wg16, wu16):
    @pl.when(pl.program_id(1) == 0)
    def _():
        wg16[...] = wg_ref[...].astype(BF)
        wu16[...] = wu_ref[...].astype(BF)

    x = x_ref[...]
    g = jnp.dot(x, wg16[...], preferred_element_type=F32)
    u = jnp.dot(x, wu16[...], preferred_element_type=F32)
    o_ref[...] = (g * jax.nn.sigmoid(g) * u).astype(o_ref.dtype)


def _ffn_up(x16, wg_all, wu_all, layer, idx):
    m, k = x16.shape
    n = wg_all.shape[-1]
    tm = _pick(m, (1024, 512, 256, 128))
    tn = _pick(n, (512, 256, 128))
    w_spec = pl.BlockSpec((None, None, k, tn), lambda j, i: (layer, idx, 0, j))
    return pl.pallas_call(
        _ffn_up_kernel,
        grid=(n // tn, m // tm),
        in_specs=[pl.BlockSpec((tm, k), lambda j, i: (i, 0)), w_spec, w_spec],
        out_specs=pl.BlockSpec((tm, tn), lambda j, i: (i, j)),
        out_shape=jax.ShapeDtypeStruct((m, n), BF),
        scratch_shapes=[pltpu.VMEM((k, tn), BF), pltpu.VMEM((k, tn), BF)],
        compiler_params=_params("parallel", "arbitrary"),
        name="ffn_up",
    )(x16, wg_all, wu_all)


def _mm_res_ln_kernel(*refs, n_x, alpha, coef):
    x_refs = refs[:n_x]
    w_ref, res_ref, g_ref, b_ref, o32_ref, o16_ref = refs[n_x:]
    x = x_refs[0][...] if n_x == 1 else jnp.concatenate([r[...] for r in x_refs], axis=1)
    y = alpha * res_ref[...] + coef * jnp.dot(x, w_ref[...], preferred_element_type=F32)
    mu = jnp.mean(y, -1, keepdims=True)
    yc = y - mu
    var = jnp.mean(yc * yc, -1, keepdims=True)
    out = yc * lax.rsqrt(var + LN_EPS) * g_ref[...] + b_ref[...]
    o32_ref[...] = out
    o16_ref[...] = out.astype(BF)


def _mm_res_ln(x_parts, w_all, w_idx, res, g, b, *, alpha, coef, name):
    m = x_parts[0].shape[0]
    k, n = w_all.shape[-2:]
    tm = _pick(m, (256, 128))
    lead = (None,) * len(w_idx)
    return pl.pallas_call(
        functools.partial(_mm_res_ln_kernel, n_x=len(x_parts), alpha=alpha, coef=coef),
        grid=(m // tm,),
        in_specs=[pl.BlockSpec((tm, xp.shape[1]), lambda i: (i, 0)) for xp in x_parts] + [
                  pl.BlockSpec(lead + (k, n), lambda i: tuple(w_idx) + (0, 0), pipeline_mode=pl.Buffered(1)),
                  pl.BlockSpec((tm, n), lambda i: (i, 0)),
                  pl.BlockSpec((1, n), lambda i: (0, 0)),
                  pl.BlockSpec((1, n), lambda i: (0, 0))],
        out_specs=[pl.BlockSpec((tm, n), lambda i: (i, 0)),
                   pl.BlockSpec((tm, n), lambda i: (i, 0))],
        out_shape=[jax.ShapeDtypeStruct((m, n), F32), jax.ShapeDtypeStruct((m, n), BF)],
        compiler_params=_params("parallel"),
        name=name,
    )(*x_parts, w_all, res, g.reshape(1, n).astype(F32), b.reshape(1, n).astype(F32))


def _attn_kernel(qi_ref, kj_ref, fst_ref, lst_ref, msk_ref, *refs, variant, n_grp, n_rep, tq, tk, rc, dv, c_exp,
                 window, seq, has_sel, has_sink, gate_col, has_add, two_paths):
    it = iter(refs)
    if variant == "mla":
        qn_ref, qr_ref, kn_ref, kp_ref, v_ref = (next(it) for _ in range(5))
    elif variant == "swa":
        q_ref, kv_ref = next(it), next(it)
    else:
        q_ref, k_ref, v_ref = next(it), next(it), next(it)
    sel_ref = next(it) if has_sel else None
    sink_ref = next(it) if has_sink else None
    gl_ref, gb_ref = (next(it), next(it)) if gate_col is not None else (None, None)
    add_ref = next(it) if has_add else None
    o_ref = next(it)
    q_sc, m_sc, acc_sc = next(it), next(it), next(it)
    p = pl.program_id(1)
    qi = qi_ref[p]
    kj = kj_ref[p]
    rows = n_rep * tq

    @pl.when(fst_ref[p] == 1)
    def _():
        if has_sink:
            m_sc[...] = sink_ref[0]
            lane = lax.broadcasted_iota(jnp.int32, acc_sc.shape, 1)
            acc_sc[...] = jnp.where(lane >= dv, 1.0, 0.0)
        else:
            m_sc[...] = jnp.full_like(m_sc, NEG)
            acc_sc[...] = jnp.zeros_like(acc_sc)
        if variant == "mla":
            half_of_lane = lax.broadcasted_iota(jnp.int32, (tq, LANE), 1) // MLA_ROPE
            mine = half_of_lane == (pl.program_id(0) % n_grp) % 2
            q_sc[:, :LANE] = (qn_ref[...].astype(F32) * c_exp).astype(BF)
            q_sc[:, LANE:] = jnp.where(mine, qr_ref[...].astype(F32) * c_exp, 0.0).astype(BF)
        elif variant == "swa":
            low = lax.broadcasted_iota(jnp.int32, (tq, LANE), 1) < dv
            for r in range(n_rep):
                chunk = q_ref[:, (r // 2) * LANE:(r // 2 + 1) * LANE].astype(F32)
                if r % 2 == 1:
                    chunk = pltpu.roll(chunk, dv, 1)
                q_sc[r * tq:(r + 1) * tq, :] = jnp.where(low, chunk * c_exp, 0.0).astype(BF)
        else:
            dq = q_sc.shape[1]
            for r in range(n_rep):
                q_sc[r * tq:(r + 1) * tq, :] = (q_ref[:, r * dq:(r + 1) * dq].astype(F32) * c_exp).astype(BF)

    if variant == "mla":
        k = jnp.concatenate([kn_ref[...], kp_ref[...]], axis=1)
        v_ext = jnp.concatenate([v_ref[...], jnp.ones((tk, dv), BF)], axis=1)
    elif variant == "swa":
        k = kv_ref[...]
        swapped = pltpu.roll(kv_ref[...].astype(F32), dv, 1)
        v_ext = jnp.where(lax.broadcasted_iota(jnp.int32, (tk, LANE), 1) < dv, swapped, 1.0).astype(BF)
    else:
        k = k_ref[...]
        v_ext = jnp.concatenate([v_ref[...], jnp.ones((tk, dv), BF)], axis=1)
    def tile_bias():
        row = lax.broadcasted_iota(jnp.int32, (tq, tk), 0)
        col = lax.broadcasted_iota(jnp.int32, (tq, tk), 1)
        dist = qi * tq - kj * tk + row - col
        mask = dist >= 0
        if window < seq:
            mask = mask & (dist < window)
        if has_sel:
            n_blk = sel_ref.shape[-1]
            blk_of_key = (kj * tk + lax.broadcasted_iota(jnp.int32, (n_blk, tk), 1)) // SLC_BLOCK
            expand = jnp.where(lax.broadcasted_iota(jnp.int32, (n_blk, tk), 0) == blk_of_key, 1.0, 0.0).astype(BF)
            picked = jnp.dot(sel_ref[0].astype(BF), expand, preferred_element_type=F32)
            mask = mask & (picked > 0.5)
        return jnp.where(mask, 0.0, NEG)

    def logits(u):
        return lax.dot_general(q_sc[u * rc:(u + 1) * rc, :], k, (((1,), (1,)), ((), ())),
                               preferred_element_type=F32)

    def online_softmax(masked):
        bias = tile_bias() if masked else None
        n_units = rows // rc
        s_next = logits(0)
        for u in range(n_units):
            s = s_next
            if u + 1 < n_units:
                s_next = logits(u + 1)
            us = slice(u * rc, (u + 1) * rc)
            m_prev = m_sc[us, :]
            if masked:
                b_u = bias[(u * rc) % tq:(u * rc) % tq + rc]
                m_new = jnp.maximum(m_prev, jnp.max(s + b_u, -1, keepdims=True))
                w = jnp.exp2((s - _lane_tile(m_new, tk // LANE)) + b_u)
            else:
                m_new = jnp.maximum(m_prev, jnp.max(s, -1, keepdims=True))
                w = jnp.exp2(s - _lane_tile(m_new, tk // LANE))
            m_sc[us, :] = m_new
            alpha = jnp.exp2(m_prev - m_new)
            acc_sc[us, :] = (_lane_tile(alpha, acc_sc.shape[1] // LANE) * acc_sc[us, :]
                             + jnp.dot(w.astype(BF), v_ext, preferred_element_type=F32))

    if not two_paths:
        online_softmax(True)
    else:
        pl.when(msk_ref[p] == 1)(lambda: online_softmax(True))
        pl.when(msk_ref[p] == 0)(lambda: online_softmax(False))

    @pl.when(lst_ref[p] == 1)
    def _():
        acc = acc_sc[...]
        if dv % LANE == 0:
            o = acc[:, :dv] / acc[:, dv:]
        else:
            o = acc[:, :dv] / acc[:, dv:dv + 1]
        if gate_col is not None:
            gates = jax.nn.sigmoid(gl_ref[...] + gb_ref[...])
        for r in range(n_rep):
            o_r = o[r * tq:(r + 1) * tq]
            if gate_col is not None:
                o_r = o_r * gates[:, 3 * r + gate_col:3 * r + gate_col + 1]
            if has_add:
                o_r = o_r + add_ref[:, r * dv:(r + 1) * dv].astype(F32)
            o_ref[:, r * dv:(r + 1) * dv] = o_r.astype(o_ref.dtype)


def _attn_call(variant, qkv_args, qkv_specs, *, batch, seq, n_grp, n_rep, dq, dv, scale, window, tq, tk,
               sel=None, sinks=None, gate=None, add=None, name="attn"):
    rows = n_rep * tq
    nq, nk = seq // tq, seq // tk
    pairs, fst, lst, msk = [], [], [], []
    for i in range(nq):
        lo = max(0, i * tq - (window - 1)) // tk
        hi = (i * tq + tq - 1) // tk
        for j in range(lo, hi + 1):
            pairs.append((i, j))
            fst.append(int(j == lo))
            lst.append(int(j == hi))
            inside = i * tq - (j * tk + tk - 1) >= 0 and i * tq + tq - 1 - j * tk < window
            msk.append(int(sel is not None or not inside))
    two_paths = 0 < sum(msk) < len(msk)
    qi = jnp.asarray([a for a, _ in pairs], jnp.int32)
    kj = jnp.asarray([b for _, b in pairs], jnp.int32)

    def q_rows(b, p, qi, kj):
        return (b // n_grp) * nq + qi[p]

    def k_rows(b, p, qi, kj):
        return (b // n_grp) * nk + kj[p]

    in_specs = [mk(q_rows, k_rows) for mk in qkv_specs]
    args = list(qkv_args)
    if sel is not None:
        in_specs.append(pl.BlockSpec((1, tq, sel.shape[-1]), lambda b, p, qi, kj, *_: (b, qi[p], 0)))
        args.append(sel)
    if sinks is not None:
        m0 = jnp.broadcast_to((sinks.astype(F32) * LOG2E)[:, :, None, None], (n_grp, n_rep, tq, LANE))
        in_specs.append(pl.BlockSpec((1, rows, LANE), lambda b, p, qi, kj, *_: (b % n_grp, 0, 0)))
        args.append(m0.reshape(n_grp, rows, LANE))
    gate_col = None
    if gate is not None:
        gl, gb, gate_col = gate
        in_specs += [pl.BlockSpec((tq, LANE), lambda b, p, qi, kj, *_: (q_rows(b, p, qi, kj), b % n_grp)),
                     pl.BlockSpec((1, LANE), lambda b, p, qi, kj, *_: (0, b % n_grp))]
        args += [gl, gb]
    out_spec = pl.BlockSpec((tq, n_rep * dv), lambda b, p, qi, kj, *_: (q_rows(b, p, qi, kj), b % n_grp))
    if add is not None:
        in_specs.append(out_spec)
        args.append(add)
    rc = min(tq, UNIT_ROWS)
    kern = functools.partial(_attn_kernel, variant=variant, n_grp=n_grp, n_rep=n_rep, tq=tq, tk=tk, rc=rc, dv=dv,
                             c_exp=scale * LOG2E, window=window, seq=seq, has_sel=sel is not None,
                             has_sink=sinks is not None, gate_col=gate_col, has_add=add is not None,
                             two_paths=two_paths)
    return pl.pallas_call(
        kern,
        grid_spec=pltpu.PrefetchScalarGridSpec(
            num_scalar_prefetch=5,
            grid=(batch * n_grp, len(pairs)),
            in_specs=in_specs,
            out_specs=out_spec,
            scratch_shapes=[pltpu.VMEM((rows, dq), BF), pltpu.VMEM((rows, LANE), F32),
                            pltpu.VMEM((rows, 2 * dv), F32)]),
        out_shape=jax.ShapeDtypeStruct((batch * seq, n_grp * n_rep * dv), BF),
        compiler_params=_params("parallel", "arbitrary"),
        name=name,
    )(qi, kj, jnp.asarray(fst, jnp.int32), jnp.asarray(lst, jnp.int32), jnp.asarray(msk, jnp.int32), *args)


def _spec(shape, rows_of, col_fn):
    def make(q_rows, k_rows):
        rf = q_rows if rows_of == "q" else k_rows
        return pl.BlockSpec(shape, lambda b, p, qi, kj, *_: (rf(b, p, qi, kj), col_fn(b)))
    return make


def _compress_kernel(x_ref, pa_ref, pb_ref, w1a_ref, w1b_ref, w2_ref, o_ref):
    x = x_ref[0]
    a = jnp.dot((x + pa_ref[...]).astype(BF), w1a_ref[...], preferred_element_type=F32)
    b = jnp.dot((x + pb_ref[...]).astype(BF), w1b_ref[...], preferred_element_type=F32)
    h = jax.nn.gelu(a + pltpu.roll(b, b.shape[0] - 1, 0))
    o_ref[0] = jnp.dot(h.astype(BF), w2_ref[...], preferred_element_type=F32).astype(o_ref.dtype)


def _compress(x, pos, w1, w2, name):
    bg, nch, width = x.shape
    d = w2.shape[1]
    hid = w1.shape[1]
    pa = pos[:CMP_STRIDE].reshape(1, width).astype(F32)
    pb = pos[CMP_STRIDE:].reshape(1, width).astype(F32)
    w1 = w1.astype(BF)
    return pl.pallas_call(
        _compress_kernel,
        grid=(bg,),
        in_specs=[pl.BlockSpec((1, nch, width), lambda b: (b, 0, 0)),
                  pl.BlockSpec((1, width), lambda b: (0, 0)),
                  pl.BlockSpec((1, width), lambda b: (0, 0)),
                  pl.BlockSpec((width, hid), lambda b: (0, 0)),
                  pl.BlockSpec((width, hid), lambda b: (1, 0)),
                  pl.BlockSpec((hid, d), lambda b: (0, 0))],
        out_specs=pl.BlockSpec((1, nch, d), lambda b: (b, 0, 0)),
        out_shape=jax.ShapeDtypeStruct((bg, nch, d), BF),
        compiler_params=_params("parallel"),
        name=name,
    )(x, pa, pb, w1, w1, w2.astype(BF))


def _cmp_attn_kernel(q_ref, k_ref, v_ref, ovt_ref, gl_ref, gb_ref, o_ref, sel_ref, *, n_rep, tq, c_exp):
    qi = pl.program_id(1)
    nc, d = k_ref.shape[1], k_ref.shape[2]
    rows = n_rep * tq
    q = jnp.concatenate([q_ref[:, r * d:(r + 1) * d] for r in range(n_rep)], axis=0)
    s = lax.dot_general(q, k_ref[0], (((1,), (1,)), ((), ())), preferred_element_type=F32)
    qpos = qi * tq + lax.broadcasted_iota(jnp.int32, (tq, nc), 0)
    cmp_end = lax.broadcasted_iota(jnp.int32, (tq, nc), 1) * CMP_STRIDE + (CMP_BLOCK - 1)
    cmask = (cmp_end <= qpos)[None]
    s = jnp.where(cmask, s.reshape(n_rep, tq, nc), NEG)
    e = jnp.exp2((s - jnp.max(s, -1, keepdims=True)) * c_exp)
    e = jnp.where(cmask, e, 0.0).reshape(rows, nc)
    ext = jnp.dot(e.astype(BF), v_ref[0], preferred_element_type=F32)
    den = ext[:, d:]
    inv = 1.0 / jnp.where(den > 0.0, den, 1.0)
    gates = jax.nn.sigmoid(gl_ref[...] + gb_ref[...])
    o = ext[:, :d] * inv
    for r in range(n_rep):
        o_ref[:, r * d:(r + 1) * d] = (o[r * tq:(r + 1) * tq] * gates[:, 3 * r:3 * r + 1]).astype(o_ref.dtype)
    inv_c = inv[:, :nc] if nc <= d else _lane_tile(inv, nc // d)
    prob = e * inv_c
    psum = jnp.sum(prob.reshape(n_rep, tq, nc), axis=0)
    p_hi = psum.astype(BF)
    p_lo = (psum - p_hi.astype(F32)).astype(BF)
    contract_last = (((1,), (1,)), ((), ()))
    imp = (lax.dot_general(ovt_ref[...], p_hi, contract_last, preferred_element_type=F32)
           + lax.dot_general(ovt_ref[...], p_lo, contract_last, preferred_element_type=F32))
    ns = imp.shape[0]
    blk = lax.broadcasted_iota(jnp.int32, (ns, tq), 0)
    cur = (qi * tq + lax.broadcasted_iota(jnp.int32, (ns, tq), 1)) // SLC_BLOCK
    eligible = blk <= cur
    forced = (blk == 0) | (blk == cur) | (blk == cur - 1)
    score = jnp.where(eligible, jnp.where(forced, FORCE_SCORE, imp), -1.0)
    beaten_by = jnp.zeros((ns, tq), F32)
    for kk in range(ns):
        sk = score[kk:kk + 1, :]
        beats = (sk > score) | ((sk == score) & (blk > kk))
        beaten_by = beaten_by + jnp.where(beats, 1.0, 0.0)
    n_sel = min(N_SEL, ns)
    sel_t = jnp.where((beaten_by < n_sel) & (score >= 0.0), 1.0, 0.0)
    sel_sq = jnp.concatenate([sel_t, jnp.zeros((tq - ns, tq), F32)], axis=0) if ns < tq else sel_t
    sel_ref[0] = sel_sq.T[:, :ns]


def _cmp_attention(z, kcmp, vcmp, gl, gb, *, batch, seq, n_grp, n_rep, scale, tq):
    bg, nc, d = kcmp.shape
    ns = seq // SLC_BLOCK
    nq = seq // tq
    starts = np.arange(nc) * CMP_STRIDE
    jb = np.arange(ns)
    overlap = ((starts[:, None] < (jb[None, :] + 1) * SLC_BLOCK)
               & (starts[:, None] + CMP_BLOCK > jb[None, :] * SLC_BLOCK)).astype(np.float32)
    v_ext = jnp.concatenate([vcmp, jnp.ones_like(vcmp)], -1)
    tok_spec = pl.BlockSpec((tq, n_rep * d), lambda b, i: ((b // n_grp) * nq + i, b % n_grp))
    return pl.pallas_call(
        functools.partial(_cmp_attn_kernel, n_rep=n_rep, tq=tq, c_exp=scale * LOG2E),
        grid=(bg, nq),
        in_specs=[tok_spec,
                  pl.BlockSpec((1, nc, d), lambda b, i: (b, 0, 0)),
                  pl.BlockSpec((1, nc, 2 * d), lambda b, i: (b, 0, 0)),
                  pl.BlockSpec((ns, nc), lambda b, i: (0, 0)),
                  pl.BlockSpec((tq, LANE), lambda b, i: ((b // n_grp) * nq + i, b % n_grp)),
                  pl.BlockSpec((1, LANE), lambda b, i: (0, b % n_grp))],
        out_specs=[tok_spec, pl.BlockSpec((1, tq, ns), lambda b, i: (b, i, 0))],
        out_shape=[jax.ShapeDtypeStruct((batch * seq, n_grp * n_rep * d), BF),
                   jax.ShapeDtypeStruct((bg, seq, ns), F32)],
        compiler_params=_params("parallel", "parallel"),
        name="nsa_cmp_attn",
    )(z, kcmp, v_ext, jnp.asarray(overlap.T, BF), gl, gb)


def _rope_tables(seq, dim):
    inv = 1.0 / (ROPE_THETA ** (jnp.arange(0, dim, 2, dtype=F32) / dim))
    ang = jnp.arange(seq, dtype=F32)[:, None] * inv[None, :]
    cos_h = jnp.concatenate([jnp.cos(ang), jnp.cos(ang)], -1)
    sin_h = jnp.concatenate([-jnp.sin(ang), jnp.sin(ang)], -1)
    return jnp.tile(cos_h, (1, LANE // dim)), jnp.tile(sin_h, (1, LANE // dim))


def _pad_cols(w, n):
    return jnp.pad(w, ((0, 0), (0, n - w.shape[1])))


def _even_mixer(x16, batch, seq, w_in, q_norm, w_uq, kv_norm, w_ukv, sinks):
    H, dn, dr, dv = MLA_HEADS, MLA_NOPE, MLA_ROPE, MLA_V
    G, R, ds = SWA_KV_HEADS, SWA_HEADS // SWA_KV_HEADS, SWA_HEAD_DIM
    n_lat = MLA_Q_RANK + MLA_KV_RANK
    z_lat = _proj(x16, w_in[:, :n_lat].astype(BF), name="even_in_latent")
    o_qs = n_lat + dr
    o_ks = o_qs + SWA_HEADS * ds
    o_vs = o_ks + G * ds
    w_kpe = w_in[:, n_lat:o_qs]
    cols = [w_in[:, o_qs:o_ks]]
    for g in range(G):
        cols += [w_in[:, o_ks + g * ds:o_ks + (g + 1) * ds], w_in[:, o_vs + g * ds:o_vs + (g + 1) * ds]]
    cols += [w_kpe, w_kpe]
    w_rg = jnp.concatenate(cols, -1)
    n_q = SWA_HEADS * ds
    cos, sin = _rope_tables(seq, ds)
    groups = "R" * (n_q // LANE) + "L" * G + "R"
    z = _proj(x16, w_rg.astype(BF), rope=(cos, sin, ds // 2, groups), seq=seq, out_dtype=BF, name="even_in_rope")

    w_uq_p = jnp.concatenate([w_uq.reshape(-1, H, dn + dr)[:, :, :dn].reshape(-1, H * dn),
                              w_uq.reshape(-1, H, dn + dr)[:, :, dn:].reshape(-1, H * dr)], -1)
    w_ukv_p = jnp.concatenate([w_ukv.reshape(-1, H, dn + dv)[:, :, :dn].reshape(-1, H * dn),
                               w_ukv.reshape(-1, H, dn + dv)[:, :, dn:].reshape(-1, H * dv)], -1)
    cos_q, sin_q = _rope_tables(seq, dr)
    q = _proj(z_lat, w_uq_p.astype(BF), k_block=0, k_size=MLA_Q_RANK, rms_g=q_norm,
              rope=(cos_q, sin_q, dr // 2, "N" * (H * dn // LANE) + "R" * (H * dr // LANE)), seq=seq, out_dtype=BF,
              name="mla_q_up")
    kv = _proj(z_lat, w_ukv_p.astype(BF), k_block=1, k_size=MLA_KV_RANK, rms_g=kv_norm, out_dtype=BF,
               name="mla_kv_up")
    tq = _pick(seq, (512, 256, 128))
    tk = _pick(seq, (512, 256, 128))
    kpe_blk = (n_q + 2 * G * ds) // LANE
    o_mla = _attn_call(
        "mla", [q, q, kv, z, kv],
        [_spec((tq, LANE), "q", lambda b: b % H),
         _spec((tq, LANE), "q", lambda b: H * dn // LANE + (b % H) // 2),
         _spec((tk, LANE), "k", lambda b: b % H),
         _spec((tk, LANE), "k", lambda b: kpe_blk),
         _spec((tk, LANE), "k", lambda b: H * dn // LANE + b % H)],
        batch=batch, seq=seq, n_grp=H, n_rep=1, dq=2 * LANE, dv=dv, scale=(dn + dr) ** -0.5, window=seq,
        tq=tq, tk=tk, name="mla_attn")

    tq = _pick(seq, (256, 128))
    o_swa = _attn_call(
        "swa", [z, z],
        [_spec((tq, R * ds), "q", lambda b: b % G),
         _spec((tq, LANE), "k", lambda b: n_q // LANE + b % G)],
        batch=batch, seq=seq, n_grp=G, n_rep=R, dq=LANE, dv=ds, scale=ds ** -0.5, window=SWA_WINDOW,
        tq=tq, tk=tq, sinks=sinks.reshape(G, R), name="swa_attn")
    return [o_mla, o_swa]


def _odd_mixer(x16, batch, seq, w_in, gate_b, cmp_pos_k, cmp_pos_v, k_w1, k_w2, v_w1, v_w2):
    G, d = NSA_KV_GROUPS, NSA_HEAD_DIM
    H = NSA_HEADS
    R = H // G
    kvw = G * d
    o_kc = H * d
    offs = [o_kc + i * kvw for i in range(7)]
    cos, sin = _rope_tables(seq, d)
    groups = "R" * H + ("R" * G + "N" * G) * 3
    z = _proj(x16, w_in[:, :offs[6]].astype(BF), rope=(cos, sin, d // 2, groups), seq=seq, out_dtype=BF,
              name="odd_in")
    w_gl = jnp.concatenate([_pad_cols(w_in[:, offs[6] + g * 3 * R:offs[6] + (g + 1) * 3 * R], LANE)
                            for g in range(G)], -1)
    gl = _proj(x16, w_gl.astype(BF), name="odd_in_gate")
    gb = jnp.concatenate([jnp.pad(gate_b[g * 3 * R:(g + 1) * 3 * R].astype(F32), (0, LANE - 3 * R))
                          for g in range(G)]).reshape(1, G * LANE)

    def chunks(a):
        a = a.astype(F32).reshape(batch, seq // CMP_STRIDE, CMP_STRIDE, G, d).transpose(0, 3, 1, 2, 4)
        return a.reshape(batch * G, seq // CMP_STRIDE, CMP_STRIDE * d)

    scale = d ** -0.5
    kcmp = _compress(chunks(z[:, offs[0]:offs[1]]), cmp_pos_k, k_w1, k_w2, "nsa_compress_k")
    vcmp = _compress(chunks(z[:, offs[1]:offs[2]]), cmp_pos_v, v_w1, v_w2, "nsa_compress_v")
    tq = _pick(seq, (256, 128))
    o, sel = _cmp_attention(z, kcmp, vcmp, gl, gb, batch=batch, seq=seq, n_grp=G, n_rep=R, scale=scale, tq=tq)

    def branch(k_off, v_off, **kw):
        return _attn_call(
            "gqa", [z, z, z],
            [_spec((tq, R * d), "q", lambda b: b % G),
             _spec((kw["tk"], d), "k", lambda b: k_off // d + b % G),
             _spec((kw["tk"], d), "k", lambda b: v_off // d + b % G)],
            batch=batch, seq=seq, n_grp=G, n_rep=R, dq=d, dv=d, scale=scale, tq=tq, **kw)

    o = branch(offs[2], offs[3], window=seq, tk=_pick(seq, (512, 256, 128)), sel=sel, gate=(gl, gb, 1), add=o,
               name="nsa_slc_attn")
    o = branch(offs[4], offs[5], window=NSA_WINDOW, tk=_pick(seq, (256, 128)), gate=(gl, gb, 2), add=o,
               name="nsa_win_attn")
    return [o]


def kernel(x, ln_g, ln_b, ffn_w_gate, ffn_w_up, ffn_w_down, even_w_in, mla_q_norm, mla_w_uq, mla_kv_norm, mla_w_ukv, swa_sinks, even_w_out, odd_w_in, nsa_gate_b, nsa_cmp_pos_k, nsa_cmp_pos_v, nsa_cmp_k_w1, nsa_cmp_k_w2, nsa_cmp_v_w1, nsa_cmp_v_w2, odd_w_out):
    batch, seq, dm = x.shape
    depth = ln_g.shape[0]
    alpha = float((2 * depth) ** 0.25)
    x32 = x.reshape(batch * seq, dm)
    x16 = x32.astype(BF)
    w_down16 = ffn_w_down.astype(BF)
    even_out16 = even_w_out.astype(BF)
    odd_out16 = odd_w_out.astype(BF)

    def ffn(x32, x16, l, idx):
        h = _ffn_up(x16, ffn_w_gate, ffn_w_up, l, idx)
        return _mm_res_ln([h], w_down16, (l, idx), x32, ln_g[l, 2 * idx], ln_b[l, 2 * idx], alpha=alpha, coef=0.5,
                          name="ffn_down_ln")

    for l in range(depth):
        x32, x16 = ffn(x32, x16, l, 0)
        j = l // 2
        if l % 2 == 0:
            o = _even_mixer(x16, batch, seq, even_w_in[j], mla_q_norm[j], mla_w_uq[j], mla_kv_norm[j],
                            mla_w_ukv[j], swa_sinks[j])
            w_out = even_out16
        else:
            o = _odd_mixer(x16, batch, seq, odd_w_in[j], nsa_gate_b[j], nsa_cmp_pos_k[j], nsa_cmp_pos_v[j],
                           nsa_cmp_k_w1[j], nsa_cmp_k_w2[j], nsa_cmp_v_w1[j], nsa_cmp_v_w2[j])
            w_out = odd_out16
        x32, x16 = _mm_res_ln(o, w_out, (j,), x32, ln_g[l, 1], ln_b[l, 1], alpha=alpha, coef=1.0,
                              name="mixer_out_ln")
        x32, x16 = ffn(x32, x16, l, 1)
    return x32.reshape(batch, seq, dm)
```

```python
import functools
import math

import numpy as np
import jax
import jax.numpy as jnp
from jax import lax
from jax.experimental import pallas as pl
from jax.experimental.pallas import tpu as pltpu

F32 = jnp.float32
BF = jnp.bfloat16

ROPE_THETA = 10000.0
LN_EPS = 1e-5
RMS_EPS = 1e-6
NEG = -1e30
FORCE_SCORE = 1e9
LOG2E = math.log2(math.e)

MLA_HEADS, MLA_NOPE, MLA_ROPE, MLA_V = 8, 128, 64, 128
MLA_Q_RANK, MLA_KV_RANK = 512, 512
SWA_HEADS, SWA_KV_HEADS, SWA_HEAD_DIM, SWA_WINDOW = 16, 2, 64, 128
NSA_HEADS, NSA_KV_GROUPS, NSA_HEAD_DIM = 16, 2, 128
CMP_BLOCK, CMP_STRIDE, CMP_HIDDEN = 32, 16, 256
SLC_BLOCK, N_SEL, NSA_WINDOW = 64, 8, 512

LANE = 128
UNIT_ROWS = 128
MLA_HEADS_PER_STEP = 4
VMEM_LIMIT_BYTES = 52 * 1024 * 1024


def _params(*sem):
    return pltpu.CompilerParams(dimension_semantics=sem, vmem_limit_bytes=VMEM_LIMIT_BYTES)


def _pick(n, candidates):
    for c in candidates:
        if n % c == 0:
            return c
    raise ValueError(f"no tile in {candidates} divides {n}")


def _lane_tile(x, n):
    return x if n == 1 else jnp.concatenate([x] * n, axis=1)


def _proj_kernel(*refs, has_rms, groups, half):
    it = iter(refs)
    x_ref, w_ref = next(it), next(it)
    g_ref = next(it) if has_rms else None
    cos_ref, sin_ref = (next(it), next(it)) if groups else (None, None)
    o_ref = next(it)
    x = x_ref[...]
    if has_rms:
        xf = x.astype(F32)
        x = xf * lax.rsqrt(jnp.mean(xf * xf, -1, keepdims=True) + RMS_EPS) * g_ref[...]
    acc = jnp.dot(x.astype(BF), w_ref[...], preferred_element_type=F32)
    if not groups:
        o_ref[...] = acc.astype(o_ref.dtype)
        return
    cos, sin = cos_ref[...], sin_ref[...]
    lane = lax.broadcasted_iota(jnp.int32, cos.shape, 1)
    low = lane < LANE // 2
    cos_l, sin_l = jnp.where(low, cos, 1.0), jnp.where(low, sin, 0.0)
    for c, kind in enumerate(groups):
        a = acc[:, c * LANE:(c + 1) * LANE]
        if kind != "N":
            if 2 * half == LANE:
                partner = pltpu.roll(a, half, 1)
            else:
                partner = jnp.where(lane % (2 * half) < half, pltpu.roll(a, LANE - half, 1), pltpu.roll(a, half, 1))
            a = a * cos + partner * sin if kind == "R" else a * cos_l + partner * sin_l
        o_ref[:, c * LANE:(c + 1) * LANE] = a.astype(o_ref.dtype)


def _proj(x, w, *, k_block=0, k_size=None, rms_g=None, rope=None, seq=None, out_dtype=F32, name="proj"):
    m = x.shape[0]
    k_size = x.shape[1] if k_size is None else k_size
    n = w.shape[1]
    out_bytes = jnp.dtype(out_dtype).itemsize
    for tm in (512, 256, 128):
        est = k_size * n * 2 + 2 * tm * k_size * x.dtype.itemsize + 2 * tm * n * out_bytes + 3 * tm * n * 4
        if m % tm == 0 and est <= VMEM_LIMIT_BYTES * 3 // 4:
            break
    in_specs = [pl.BlockSpec((tm, k_size), lambda i: (i, k_block)),
                pl.BlockSpec((k_size, n), lambda i: (0, 0), pipeline_mode=pl.Buffered(1))]
    args = [x, w]
    if rms_g is not None:
        in_specs.append(pl.BlockSpec((1, k_size), lambda i: (0, 0)))
        args.append(rms_g.reshape(1, k_size).astype(F32))
    half, groups = 0, ()
    if rope is not None:
        cos, sin, half, groups = rope
        sb = seq // tm
        in_specs += [pl.BlockSpec((tm, LANE), lambda i: (i % sb, 0))] * 2
        args += [cos, sin]
    return pl.pallas_call(
        functools.partial(_proj_kernel, has_rms=rms_g is not None, groups=tuple(groups), half=half),
        grid=(m // tm,),
        in_specs=in_specs,
        out_specs=pl.BlockSpec((tm, n), lambda i: (i, 0)),
        out_shape=jax.ShapeDtypeStruct((m, n), out_dtype),
        compiler_params=_params("parallel"),
        name=name,
    )(*args)


def _ffn_up_kernel(x_ref, wg_ref, wu_ref, wd_ref, o_ref, wd16_ref, wg16, wu16):
    @pl.when(pl.program_id(1) == 0)
    def _():
        wg16[...] = wg_ref[...].astype(BF)
        wu16[...] = wu_ref[...].astype(BF)
        wd16_ref[...] = wd_ref[...].astype(BF)

    x = x_ref[...]
    g = jnp.dot(x, wg16[...], preferred_element_type=F32)
    u = jnp.dot(x, wu16[...], preferred_element_type=F32)
    o_ref[...] = (g * jax.nn.sigmoid(g) * u).astype(o_ref.dtype)


def _ffn_up(x16, wg_all, wu_all, wd_all, layer, idx):
    m, k = x16.shape
    n = wg_all.shape[-1]
    n_out = wd_all.shape[-1]
    tm = _pick(m, (1024, 512, 256, 128))
    tn = _pick(n, (512, 256, 128))
    w_spec = pl.BlockSpec((None, None, k, tn), lambda j, i: (layer, idx, 0, j))
    return pl.pallas_call(
        _ffn_up_kernel,
        grid=(n // tn, m // tm),
        in_specs=[pl.BlockSpec((tm, k), lambda j, i: (i, 0)), w_spec, w_spec,
                  pl.BlockSpec((None, None, tn, n_out), lambda j, i: (layer, idx, j, 0))],
        out_specs=[pl.BlockSpec((tm, tn), lambda j, i: (i, j)),
                   pl.BlockSpec((tn, n_out), lambda j, i: (j, 0))],
        out_shape=[jax.ShapeDtypeStruct((m, n), BF), jax.ShapeDtypeStruct((n, n_out), BF)],
        scratch_shapes=[pltpu.VMEM((k, tn), BF), pltpu.VMEM((k, tn), BF)],
        compiler_params=_params("parallel", "arbitrary"),
        name="ffn_up",
    )(x16, wg_all, wu_all, wd_all)


def _mm_res_ln_kernel(*refs, n_x, alpha, coef):
    x_refs = refs[:n_x]
    w_ref, res_ref, g_ref, b_ref, o32_ref, o16_ref = refs[n_x:]
    x = x_refs[0][...] if n_x == 1 else jnp.concatenate([r[...] for r in x_refs], axis=1)
    y = alpha * res_ref[...] + coef * jnp.dot(x, w_ref[...], preferred_element_type=F32)
    mu = jnp.mean(y, -1, keepdims=True)
    yc = y - mu
    var = jnp.mean(yc * yc, -1, keepdims=True)
    out = yc * lax.rsqrt(var + LN_EPS) * g_ref[...] + b_ref[...]
    o32_ref[...] = out
    o16_ref[...] = out.astype(BF)


def _mm_res_ln(x_parts, w_all, w_idx, res, g, b, *, alpha, coef, name):
    m = x_parts[0].shape[0]
    k, n = w_all.shape[-2:]
    tm = _pick(m, (256, 128))
    lead = (None,) * len(w_idx)
    return pl.pallas_call(
        functools.partial(_mm_res_ln_kernel, n_x=len(x_parts), alpha=alpha, coef=coef),
        grid=(m // tm,),
        in_specs=[pl.BlockSpec((tm, xp.shape[1]), lambda i: (i, 0)) for xp in x_parts] + [
                  pl.BlockSpec(lead + (k, n), lambda i: tuple(w_idx) + (0, 0), pipeline_mode=pl.Buffered(1)),
                  pl.BlockSpec((tm, n), lambda i: (i, 0)),
                  pl.BlockSpec((1, n), lambda i: (0, 0)),
                  pl.BlockSpec((1, n), lambda i: (0, 0))],
        out_specs=[pl.BlockSpec((tm, n), lambda i: (i, 0)),
                   pl.BlockSpec((tm, n), lambda i: (i, 0))],
        out_shape=[jax.ShapeDtypeStruct((m, n), F32), jax.ShapeDtypeStruct((m, n), BF)],
        compiler_params=_params("parallel"),
        name=name,
    )(*x_parts, w_all, res, g.reshape(1, n).astype(F32), b.reshape(1, n).astype(F32))


def _attn_kernel(qi_ref, kj_ref, fst_ref, lst_ref, msk_ref, *refs, variant, n_grp, n_rep, tq, tk, rc, dv, c_exp,
                 window, seq, has_sel, has_sink, gate_col, has_add, two_paths):
    it = iter(refs)
    if variant == "mla":
        qn_ref, qr_ref, kn_ref, kp_ref, v_ref = (next(it) for _ in range(5))
    elif variant == "swa":
        q_ref, kv_ref = next(it), next(it)
    else:
        q_ref, k_ref, v_ref = next(it), next(it), next(it)
    sel_ref = next(it) if has_sel else None
    sink_ref = next(it) if has_sink else None
    gl_ref, gb_ref = (next(it), next(it)) if gate_col is not None else (None, None)
    add_ref = next(it) if has_add else None
    o_ref = next(it)
    q_sc, m_sc, acc_sc, bias_sc = next(it), next(it), next(it), next(it)
    p = pl.program_id(1)
    qi = qi_ref[p]
    kj = kj_ref[p]
    rows = n_rep * tq

    @pl.when(fst_ref[p] == 1)
    def _():
        if has_sink:
            m_sc[...] = sink_ref[0]
            lane = lax.broadcasted_iota(jnp.int32, acc_sc.shape, 1)
            acc_sc[...] = jnp.where(lane >= dv, 1.0, 0.0)
        else:
            m_sc[...] = jnp.full_like(m_sc, NEG)
            acc_sc[...] = jnp.zeros_like(acc_sc)
        if variant == "mla":
            half_of_lane = lax.broadcasted_iota(jnp.int32, (tq, LANE), 1) // MLA_ROPE
            for r in range(n_rep):
                rope_pair = qr_ref[:, (r // 2) * LANE:(r // 2 + 1) * LANE].astype(F32) * c_exp
                q_sc[r * tq:(r + 1) * tq, :LANE] = (qn_ref[:, r * LANE:(r + 1) * LANE].astype(F32) * c_exp).astype(BF)
                q_sc[r * tq:(r + 1) * tq, LANE:] = jnp.where(half_of_lane == r % 2, rope_pair, 0.0).astype(BF)
        elif variant == "swa":
            low = lax.broadcasted_iota(jnp.int32, (tq, LANE), 1) < dv
            for r in range(n_rep):
                chunk = q_ref[:, (r // 2) * LANE:(r // 2 + 1) * LANE].astype(F32)
                if r % 2 == 1:
                    chunk = pltpu.roll(chunk, dv, 1)
                q_sc[r * tq:(r + 1) * tq, :] = jnp.where(low, chunk * c_exp, 0.0).astype(BF)
        else:
            dq = q_sc.shape[1]
            for r in range(n_rep):
                q_sc[r * tq:(r + 1) * tq, :] = (q_ref[:, r * dq:(r + 1) * dq].astype(F32) * c_exp).astype(BF)

    ones = jnp.ones((tk, dv), BF)
    if variant == "mla":
        ks = [jnp.concatenate([kn_ref[:, r * LANE:(r + 1) * LANE], kp_ref[...]], axis=1) for r in range(n_rep)]
        vs = [jnp.concatenate([v_ref[:, r * dv:(r + 1) * dv], ones], axis=1) for r in range(n_rep)]
    elif variant == "swa":
        swapped = pltpu.roll(kv_ref[...].astype(F32), dv, 1)
        ks = [kv_ref[...]]
        vs = [jnp.where(lax.broadcasted_iota(jnp.int32, (tk, LANE), 1) < dv, swapped, 1.0).astype(BF)]
    else:
        ks = [k_ref[...]]
        vs = [jnp.concatenate([v_ref[...], ones], axis=1)]

    def kv_of(u):
        return (u * rc) // tq if len(ks) > 1 else 0

    def tile_bias():
        row = lax.broadcasted_iota(jnp.int32, (tq, tk), 0)
        col = lax.broadcasted_iota(jnp.int32, (tq, tk), 1)
        dist = qi * tq - kj * tk + row - col
        mask = dist >= 0
        if window < seq:
            mask = mask & (dist < window)
        if has_sel:
            n_blk = sel_ref.shape[-1]
            blk_of_key = (kj * tk + lax.broadcasted_iota(jnp.int32, (n_blk, tk), 1)) // SLC_BLOCK
            expand = jnp.where(lax.broadcasted_iota(jnp.int32, (n_blk, tk), 0) == blk_of_key, 1.0, 0.0).astype(BF)
            picked = jnp.dot(sel_ref[0].astype(BF), expand, preferred_element_type=F32)
            mask = mask & (picked > 0.5)
        return jnp.where(mask, 0.0, NEG)

    def logits(u):
        return lax.dot_general(q_sc[u * rc:(u + 1) * rc, :], ks[kv_of(u)], (((1,), (1,)), ((), ())),
                               preferred_element_type=F32)

    def online_softmax(masked):
        if masked and has_sel:
            bias_sc[...] = tile_bias()
        if masked and not has_sel:
            rel = (lax.broadcasted_iota(jnp.int32, (rc, tk), 0) - lax.broadcasted_iota(jnp.int32, (rc, tk), 1))
        n_units = rows // rc
        s_next = logits(0)
        for u in range(n_units):
            s = s_next
            if u + 1 < n_units:
                s_next = logits(u + 1)
            us = slice(u * rc, (u + 1) * rc)
            m_prev = m_sc[us, :]
            if masked and has_sel:
                s = s + bias_sc[(u * rc) % tq:(u * rc) % tq + rc, :]
            elif masked:
                base = qi * tq - kj * tk + (u * rc) % tq
                visible = rel >= -base
                if window < seq:
                    visible = visible & (rel < window - base)
                s = jnp.where(visible, s, NEG)
            m_new = jnp.maximum(m_prev, jnp.max(s, -1, keepdims=True))
            w = jnp.exp2(s - _lane_tile(m_new, tk // LANE))
            m_sc[us, :] = m_new
            alpha = jnp.exp2(m_prev - m_new)
            acc_sc[us, :] = (_lane_tile(alpha, acc_sc.shape[1] // LANE) * acc_sc[us, :]
                             + jnp.dot(w.astype(BF), vs[kv_of(u)], preferred_element_type=F32))

    if not two_paths:
        online_softmax(True)
    else:
        pl.when(msk_ref[p] == 1)(lambda: online_softmax(True))
        pl.when(msk_ref[p] == 0)(lambda: online_softmax(False))

    @pl.when(lst_ref[p] == 1)
    def _():
        acc = acc_sc[...]
        if dv % LANE == 0:
            o = acc[:, :dv] / acc[:, dv:]
        else:
            o = acc[:, :dv] / acc[:, dv:dv + 1]
        if gate_col is not None:
            gates = jax.nn.sigmoid(gl_ref[...] + gb_ref[...])
        for r in range(n_rep):
            o_r = o[r * tq:(r + 1) * tq]
            if gate_col is not None:
                o_r = o_r * gates[:, 3 * r + gate_col:3 * r + gate_col + 1]
            if has_add:
                o_r = o_r + add_ref[:, r * dv:(r + 1) * dv].astype(F32)
            o_ref[:, r * dv:(r + 1) * dv] = o_r.astype(o_ref.dtype)


def _attn_call(variant, qkv_args, qkv_specs, *, batch, seq, n_grp, n_rep, dq, dv, scale, window, tq, tk,
               sel=None, sinks=None, gate=None, add=None, name="attn"):
    rows = n_rep * tq
    nq, nk = seq // tq, seq // tk
    pairs, fst, lst, msk = [], [], [], []
    for i in range(nq):
        lo = max(0, i * tq - (window - 1)) // tk
        hi = (i * tq + tq - 1) // tk
        for j in range(lo, hi + 1):
            pairs.append((i, j))
            fst.append(int(j == lo))
            lst.append(int(j == hi))
            inside = i * tq - (j * tk + tk - 1) >= 0 and i * tq + tq - 1 - j * tk < window
            msk.append(int(sel is not None or not inside))
    two_paths = 0 < sum(msk) < len(msk)
    qi = jnp.asarray([a for a, _ in pairs], jnp.int32)
    kj = jnp.asarray([b for _, b in pairs], jnp.int32)

    def q_rows(b, p, qi, kj):
        return (b // n_grp) * nq + qi[p]

    def k_rows(b, p, qi, kj):
        return (b // n_grp) * nk + kj[p]

    in_specs = [mk(q_rows, k_rows) for mk in qkv_specs]
    args = list(qkv_args)
    if sel is not None:
        in_specs.append(pl.BlockSpec((1, tq, sel.shape[-1]), lambda b, p, qi, kj, *_: (b, qi[p], 0)))
        args.append(sel)
    if sinks is not None:
        m0 = jnp.broadcast_to((sinks.astype(F32) * LOG2E)[:, :, None, None], (n_grp, n_rep, tq, LANE))
        in_specs.append(pl.BlockSpec((1, rows, LANE), lambda b, p, qi, kj, *_: (b % n_grp, 0, 0)))
        args.append(m0.reshape(n_grp, rows, LANE))
    gate_col = None
    if gate is not None:
        gl, gb, gate_col = gate
        in_specs += [pl.BlockSpec((tq, LANE), lambda b, p, qi, kj, *_: (q_rows(b, p, qi, kj), b % n_grp)),
                     pl.BlockSpec((1, LANE), lambda b, p, qi, kj, *_: (0, b % n_grp))]
        args += [gl, gb]
    out_spec = pl.BlockSpec((tq, n_rep * dv), lambda b, p, qi, kj, *_: (q_rows(b, p, qi, kj), b % n_grp))
    if add is not None:
        in_specs.append(out_spec)
        args.append(add)
    rc = min(tq, UNIT_ROWS)
    kern = functools.partial(_attn_kernel, variant=variant, n_grp=n_grp, n_rep=n_rep, tq=tq, tk=tk, rc=rc, dv=dv,
                             c_exp=scale * LOG2E, window=window, seq=seq, has_sel=sel is not None,
                             has_sink=sinks is not None, gate_col=gate_col, has_add=add is not None,
                             two_paths=two_paths)
    return pl.pallas_call(
        kern,
        grid_spec=pltpu.PrefetchScalarGridSpec(
            num_scalar_prefetch=5,
            grid=(batch * n_grp, len(pairs)),
            in_specs=in_specs,
            out_specs=out_spec,
            scratch_shapes=[pltpu.VMEM((rows, dq), BF), pltpu.VMEM((rows, LANE), F32),
                            pltpu.VMEM((rows, 2 * dv), F32), pltpu.VMEM((tq, tk), F32)]),
        out_shape=jax.ShapeDtypeStruct((batch * seq, n_grp * n_rep * dv), BF),
        compiler_params=_params("parallel", "arbitrary"),
        name=name,
    )(qi, kj, jnp.asarray(fst, jnp.int32), jnp.asarray(lst, jnp.int32), jnp.asarray(msk, jnp.int32), *args)


def _spec(shape, rows_of, col_fn):
    def make(q_rows, k_rows):
        rf = q_rows if rows_of == "q" else k_rows
        return pl.BlockSpec(shape, lambda b, p, qi, kj, *_: (rf(b, p, qi, kj), col_fn(b)))
    return make


def _compress_kernel(x_ref, pa_ref, pb_ref, w1a_ref, w1b_ref, w2_ref, o_ref):
    x = x_ref[0]
    a = jnp.dot((x + pa_ref[...]).astype(BF), w1a_ref[...], preferred_element_type=F32)
    b = jnp.dot((x + pb_ref[...]).astype(BF), w1b_ref[...], preferred_element_type=F32)
    h = jax.nn.gelu(a + pltpu.roll(b, b.shape[0] - 1, 0))
    o_ref[0] = jnp.dot(h.astype(BF), w2_ref[...], preferred_element_type=F32).astype(o_ref.dtype)


def _compress(x, pos, w1, w2, name):
    bg, nch, width = x.shape
    d = w2.shape[1]
    hid = w1.shape[1]
    pa = pos[:CMP_STRIDE].reshape(1, width).astype(F32)
    pb = pos[CMP_STRIDE:].reshape(1, width).astype(F32)
    w1 = w1.astype(BF)
    return pl.pallas_call(
        _compress_kernel,
        grid=(bg,),
        in_specs=[pl.BlockSpec((1, nch, width), lambda b: (b, 0, 0)),
                  pl.BlockSpec((1, width), lambda b: (0, 0)),
                  pl.BlockSpec((1, width), lambda b: (0, 0)),
                  pl.BlockSpec((width, hid), lambda b: (0, 0)),
                  pl.BlockSpec((width, hid), lambda b: (1, 0)),
                  pl.BlockSpec((hid, d), lambda b: (0, 0))],
        out_specs=pl.BlockSpec((1, nch, d), lambda b: (b, 0, 0)),
        out_shape=jax.ShapeDtypeStruct((bg, nch, d), BF),
        compiler_params=_params("parallel"),
        name=name,
    )(x, pa, pb, w1, w1, w2.astype(BF))


def _cmp_attn_kernel(q_ref, k_ref, v_ref, ovt_ref, gl_ref, gb_ref, o_ref, sel_ref, *, n_rep, tq, c_exp):
    qi = pl.program_id(1)
    nc, d = k_ref.shape[1], k_ref.shape[2]
    rows = n_rep * tq
    q = jnp.concatenate([q_ref[:, r * d:(r + 1) * d] for r in range(n_rep)], axis=0)
    s = lax.dot_general(q, k_ref[0], (((1,), (1,)), ((), ())), preferred_element_type=F32)
    qpos = qi * tq + lax.broadcasted_iota(jnp.int32, (tq, nc), 0)
    cmp_end = lax.broadcasted_iota(jnp.int32, (tq, nc), 1) * CMP_STRIDE + (CMP_BLOCK - 1)
    cmask = (cmp_end <= qpos)[None]
    s = jnp.where(cmask, s.reshape(n_rep, tq, nc), NEG)
    e = jnp.exp2((s - jnp.max(s, -1, keepdims=True)) * c_exp)
    e = jnp.where(cmask, e, 0.0).reshape(rows, nc)
    ext = jnp.dot(e.astype(BF), v_ref[0], preferred_element_type=F32)
    den = ext[:, d:]
    inv = 1.0 / jnp.where(den > 0.0, den, 1.0)
    gates = jax.nn.sigmoid(gl_ref[...] + gb_ref[...])
    o = ext[:, :d] * inv
    for r in range(n_rep):
        o_ref[:, r * d:(r + 1) * d] = (o[r * tq:(r + 1) * tq] * gates[:, 3 * r:3 * r + 1]).astype(o_ref.dtype)
    inv_c = inv[:, :nc] if nc <= d else _lane_tile(inv, nc // d)
    prob = e * inv_c
    psum = jnp.sum(prob.reshape(n_rep, tq, nc), axis=0)
    p_hi = psum.astype(BF)
    p_lo = (psum - p_hi.astype(F32)).astype(BF)
    contract_last = (((1,), (1,)), ((), ()))
    imp = (lax.dot_general(ovt_ref[...], p_hi, contract_last, preferred_element_type=F32)
           + lax.dot_general(ovt_ref[...], p_lo, contract_last, preferred_element_type=F32))
    ns = imp.shape[0]
    blk = lax.broadcasted_iota(jnp.int32, (ns, tq), 0)
    cur = (qi * tq + lax.broadcasted_iota(jnp.int32, (ns, tq), 1)) // SLC_BLOCK
    eligible = blk <= cur
    forced = (blk == 0) | (blk == cur) | (blk == cur - 1)
    score = jnp.where(eligible, jnp.where(forced, FORCE_SCORE, imp), -1.0)
    beaten_by = jnp.zeros((ns, tq), F32)
    for kk in range(ns):
        sk = score[kk:kk + 1, :]
        beats = (sk > score) | ((sk == score) & (blk > kk))
        beaten_by = beaten_by + jnp.where(beats, 1.0, 0.0)
    n_sel = min(N_SEL, ns)
    sel_t = jnp.where((beaten_by < n_sel) & (score >= 0.0), 1.0, 0.0)
    sel_sq = jnp.concatenate([sel_t, jnp.zeros((tq - ns, tq), F32)], axis=0) if ns < tq else sel_t
    sel_ref[0] = sel_sq.T[:, :ns]


def _cmp_attention(z, kcmp, vcmp, gl, gb, *, batch, seq, n_grp, n_rep, scale, tq):
    bg, nc, d = kcmp.shape
    ns = seq // SLC_BLOCK
    nq = seq // tq
    starts = np.arange(nc) * CMP_STRIDE
    jb = np.arange(ns)
    overlap = ((starts[:, None] < (jb[None, :] + 1) * SLC_BLOCK)
               & (starts[:, None] + CMP_BLOCK > jb[None, :] * SLC_BLOCK)).astype(np.float32)
    v_ext = jnp.concatenate([vcmp, jnp.ones_like(vcmp)], -1)
    tok_spec = pl.BlockSpec((tq, n_rep * d), lambda b, i: ((b // n_grp) * nq + i, b % n_grp))
    return pl.pallas_call(
        functools.partial(_cmp_attn_kernel, n_rep=n_rep, tq=tq, c_exp=scale * LOG2E),
        grid=(bg, nq),
        in_specs=[tok_spec,
                  pl.BlockSpec((1, nc, d), lambda b, i: (b, 0, 0)),
                  pl.BlockSpec((1, nc, 2 * d), lambda b, i: (b, 0, 0)),
                  pl.BlockSpec((ns, nc), lambda b, i: (0, 0)),
                  pl.BlockSpec((tq, LANE), lambda b, i: ((b // n_grp) * nq + i, b % n_grp)),
                  pl.BlockSpec((1, LANE), lambda b, i: (0, b % n_grp))],
        out_specs=[tok_spec, pl.BlockSpec((1, tq, ns), lambda b, i: (b, i, 0))],
        out_shape=[jax.ShapeDtypeStruct((batch * seq, n_grp * n_rep * d), BF),
                   jax.ShapeDtypeStruct((bg, seq, ns), F32)],
        compiler_params=_params("parallel", "parallel"),
        name="nsa_cmp_attn",
    )(z, kcmp, v_ext, jnp.asarray(overlap.T, BF), gl, gb)


def _rope_tables(seq, dim):
    inv = 1.0 / (ROPE_THETA ** (jnp.arange(0, dim, 2, dtype=F32) / dim))
    ang = jnp.arange(seq, dtype=F32)[:, None] * inv[None, :]
    cos_h = jnp.concatenate([jnp.cos(ang), jnp.cos(ang)], -1)
    sin_h = jnp.concatenate([-jnp.sin(ang), jnp.sin(ang)], -1)
    return jnp.tile(cos_h, (1, LANE // dim)), jnp.tile(sin_h, (1, LANE // dim))


def _pad_cols(w, n):
    return jnp.pad(w, ((0, 0), (0, n - w.shape[1])))


def _even_mixer(x16, batch, seq, w_in, q_norm, w_uq, kv_norm, w_ukv, sinks):
    H, dn, dr, dv = MLA_HEADS, MLA_NOPE, MLA_ROPE, MLA_V
    G, R, ds = SWA_KV_HEADS, SWA_HEADS // SWA_KV_HEADS, SWA_HEAD_DIM
    n_lat = MLA_Q_RANK + MLA_KV_RANK
    z_lat = _proj(x16, w_in[:, :n_lat].astype(BF), name="even_in_latent")
    o_qs = n_lat + dr
    o_ks = o_qs + SWA_HEADS * ds
    o_vs = o_ks + G * ds
    w_kpe = w_in[:, n_lat:o_qs]
    cols = [w_in[:, o_qs:o_ks]]
    for g in range(G):
        cols += [w_in[:, o_ks + g * ds:o_ks + (g + 1) * ds], w_in[:, o_vs + g * ds:o_vs + (g + 1) * ds]]
    cols += [w_kpe, w_kpe]
    w_rg = jnp.concatenate(cols, -1)
    n_q = SWA_HEADS * ds
    cos, sin = _rope_tables(seq, ds)
    groups = "R" * (n_q // LANE) + "L" * G + "R"
    z = _proj(x16, w_rg.astype(BF), rope=(cos, sin, ds // 2, groups), seq=seq, out_dtype=BF, name="even_in_rope")

    w_uq_p = jnp.concatenate([w_uq.reshape(-1, H, dn + dr)[:, :, :dn].reshape(-1, H * dn),
                              w_uq.reshape(-1, H, dn + dr)[:, :, dn:].reshape(-1, H * dr)], -1)
    w_ukv_p = jnp.concatenate([w_ukv.reshape(-1, H, dn + dv)[:, :, :dn].reshape(-1, H * dn),
                               w_ukv.reshape(-1, H, dn + dv)[:, :, dn:].reshape(-1, H * dv)], -1)
    cos_q, sin_q = _rope_tables(seq, dr)
    q = _proj(z_lat, w_uq_p.astype(BF), k_block=0, k_size=MLA_Q_RANK, rms_g=q_norm,
              rope=(cos_q, sin_q, dr // 2, "N" * (H * dn // LANE) + "R" * (H * dr // LANE)), seq=seq, out_dtype=BF,
              name="mla_q_up")
    kv = _proj(z_lat, w_ukv_p.astype(BF), k_block=1, k_size=MLA_KV_RANK, rms_g=kv_norm, out_dtype=BF,
               name="mla_kv_up")
    tq = _pick(seq, (512, 256, 128))
    tk = _pick(seq, (512, 256, 128))
    kpe_blk = (n_q + 2 * G * ds) // LANE
    hp = MLA_HEADS_PER_STEP
    n_hg = H // hp
    o_mla = _attn_call(
        "mla", [q, q, kv, z, kv],
        [_spec((tq, hp * dn), "q", lambda b: b % n_hg),
         _spec((tq, hp * dr), "q", lambda b: H * dn // (hp * dr) + b % n_hg),
         _spec((tk, hp * dn), "k", lambda b: b % n_hg),
         _spec((tk, LANE), "k", lambda b: kpe_blk),
         _spec((tk, hp * dv), "k", lambda b: H * dn // (hp * dv) + b % n_hg)],
        batch=batch, seq=seq, n_grp=n_hg, n_rep=hp, dq=2 * LANE, dv=dv, scale=(dn + dr) ** -0.5, window=seq,
        tq=tq, tk=tk, name="mla_attn")

    tq = _pick(seq, (256, 128))
    o_swa = _attn_call(
        "swa", [z, z],
        [_spec((tq, R * ds), "q", lambda b: b % G),
         _spec((tq, LANE), "k", lambda b: n_q // LANE + b % G)],
        batch=batch, seq=seq, n_grp=G, n_rep=R, dq=LANE, dv=ds, scale=ds ** -0.5, window=SWA_WINDOW,
        tq=tq, tk=tq, sinks=sinks.reshape(G, R), name="swa_attn")
    return [o_mla, o_swa]


def _odd_mixer(x16, batch, seq, w_in, gate_b, cmp_pos_k, cmp_pos_v, k_w1, k_w2, v_w1, v_w2):
    G, d = NSA_KV_GROUPS, NSA_HEAD_DIM
    H = NSA_HEADS
    R = H // G
    kvw = G * d
    o_kc = H * d
    offs = [o_kc + i * kvw for i in range(7)]
    cos, sin = _rope_tables(seq, d)
    groups = "R" * H + ("R" * G + "N" * G) * 3
    z = _proj(x16, w_in[:, :offs[6]].astype(BF), rope=(cos, sin, d // 2, groups), seq=seq, out_dtype=BF,
              name="odd_in")
    w_gl = jnp.concatenate([_pad_cols(w_in[:, offs[6] + g * 3 * R:offs[6] + (g + 1) * 3 * R], LANE)
                            for g in range(G)], -1)
    gl = _proj(x16, w_gl.astype(BF), name="odd_in_gate")
    gb = jnp.concatenate([jnp.pad(gate_b[g * 3 * R:(g + 1) * 3 * R].astype(F32), (0, LANE - 3 * R))
                          for g in range(G)]).reshape(1, G * LANE)

    def chunks(a):
        a = a.astype(F32).reshape(batch, seq // CMP_STRIDE, CMP_STRIDE, G, d).transpose(0, 3, 1, 2, 4)
        return a.reshape(batch * G, seq // CMP_STRIDE, CMP_STRIDE * d)

    scale = d ** -0.5
    kcmp = _compress(chunks(z[:, offs[0]:offs[1]]), cmp_pos_k, k_w1, k_w2, "nsa_compress_k")
    vcmp = _compress(chunks(z[:, offs[1]:offs[2]]), cmp_pos_v, v_w1, v_w2, "nsa_compress_v")
    tq = _pick(seq, (256, 128))
    o, sel = _cmp_attention(z, kcmp, vcmp, gl, gb, batch=batch, seq=seq, n_grp=G, n_rep=R, scale=scale, tq=tq)

    def branch(k_off, v_off, **kw):
        return _attn_call(
            "gqa", [z, z, z],
            [_spec((tq, R * d), "q", lambda b: b % G),
             _spec((kw["tk"], d), "k", lambda b: k_off // d + b % G),
             _spec((kw["tk"], d), "k", lambda b: v_off // d + b % G)],
            batch=batch, seq=seq, n_grp=G, n_rep=R, dq=d, dv=d, scale=scale, tq=tq, **kw)

    o = branch(offs[2], offs[3], window=seq, tk=_pick(seq, (512, 256, 128)), sel=sel, gate=(gl, gb, 1), add=o,
               name="nsa_slc_attn")
    o = branch(offs[4], offs[5], window=NSA_WINDOW, tk=_pick(seq, (256, 128)), gate=(gl, gb, 2), add=o,
               name="nsa_win_attn")
    return [o]


def kernel(x, ln_g, ln_b, ffn_w_gate, ffn_w_up, ffn_w_down, even_w_in, mla_q_norm, mla_w_uq, mla_kv_norm, mla_w_ukv, swa_sinks, even_w_out, odd_w_in, nsa_gate_b, nsa_cmp_pos_k, nsa_cmp_pos_v, nsa_cmp_k_w1, nsa_cmp_k_w2, nsa_cmp_v_w1, nsa_cmp_v_w2, odd_w_out):
    batch, seq, dm = x.shape
    depth = ln_g.shape[0]
    alpha = float((2 * depth) ** 0.25)
    x32 = x.reshape(batch * seq, dm)
    x16 = x32.astype(BF)
    even_out16 = even_w_out.astype(BF)
    odd_out16 = odd_w_out.astype(BF)

    def ffn(x32, x16, l, idx):
        h, w_down16 = _ffn_up(x16, ffn_w_gate, ffn_w_up, ffn_w_down, l, idx)
        return _mm_res_ln([h], w_down16, (), x32, ln_g[l, 2 * idx], ln_b[l, 2 * idx], alpha=alpha, coef=0.5,
                          name="ffn_down_ln")

    for l in range(depth):
        x32, x16 = ffn(x32, x16, l, 0)
        j = l // 2
        if l % 2 == 0:
            o = _even_mixer(x16, batch, seq, even_w_in[j], mla_q_norm[j], mla_w_uq[j], mla_kv_norm[j],
                            mla_w_ukv[j], swa_sinks[j])
            w_out = even_out16
        else:
            o = _odd_mixer(x16, batch, seq, odd_w_in[j], nsa_gate_b[j], nsa_cmp_pos_k[j], nsa_cmp_pos_v[j],
                           nsa_cmp_k_w1[j], nsa_cmp_k_w2[j], nsa_cmp_v_w1[j], nsa_cmp_v_w2[j])
            w_out = odd_out16
        x32, x16 = _mm_res_ln(o, w_out, (j,), x32, ln_g[l, 1], ln_b[l, 1], alpha=alpha, coef=1.0,
                              name="mixer_out_ln")
        x32, x16 = ffn(x32, x16, l, 1)
    return x32.reshape(batch, seq, dm)
```

```python
import functools
import math

import numpy as np
import jax
import jax.numpy as jnp
from jax import lax
from jax.experimental import pallas as pl
from jax.experimental.pallas import tpu as pltpu

F32 = jnp.float32
BF = jnp.bfloat16

ROPE_THETA = 10000.0
LN_EPS = 1e-5
RMS_EPS = 1e-6
NEG = -1e30
FORCE_SCORE = 1e9
LOG2E = math.log2(math.e)

MLA_HEADS, MLA_NOPE, MLA_ROPE, MLA_V = 8, 128, 64, 128
MLA_Q_RANK, MLA_KV_RANK = 512, 512
SWA_HEADS, SWA_KV_HEADS, SWA_HEAD_DIM, SWA_WINDOW = 16, 2, 64, 128
NSA_HEADS, NSA_KV_GROUPS, NSA_HEAD_DIM = 16, 2, 128
CMP_BLOCK, CMP_STRIDE, CMP_HIDDEN = 32, 16, 256
SLC_BLOCK, N_SEL, NSA_WINDOW = 64, 8, 512

LANE = 128
UNIT_ROWS = 128
MLA_HEADS_PER_STEP = 4
VMEM_LIMIT_BYTES = 52 * 1024 * 1024


def _params(*sem):
    return pltpu.CompilerParams(dimension_semantics=sem, vmem_limit_bytes=VMEM_LIMIT_BYTES)


def _pick(n, candidates):
    for c in candidates:
        if n % c == 0:
            return c
    raise ValueError(f"no tile in {candidates} divides {n}")


def _lane_tile(x, n):
    return x if n == 1 else jnp.concatenate([x] * n, axis=1)


def _proj_kernel(*refs, has_rms, groups, half, n_tail):
    it = iter(refs)
    x_ref, w_ref = next(it), next(it)
    g_ref = next(it) if has_rms else None
    cos_ref, sin_ref = (next(it), next(it)) if groups else (None, None)
    o_ref = next(it)
    x = x_ref[...]
    if has_rms:
        xf = x.astype(F32)
        x = xf * lax.rsqrt(jnp.mean(xf * xf, -1, keepdims=True) + RMS_EPS) * g_ref[...]
    acc = jnp.dot(x.astype(BF), w_ref[...], preferred_element_type=F32)
    if n_tail:
        tail_ref = next(it)
        tail_ref[...] = acc[:, acc.shape[1] - n_tail:]
        acc = acc[:, :acc.shape[1] - n_tail]
    if not groups:
        o_ref[...] = acc.astype(o_ref.dtype)
        return
    cos, sin = cos_ref[...], sin_ref[...]
    lane = lax.broadcasted_iota(jnp.int32, cos.shape, 1)
    low = lane < LANE // 2
    cos_l, sin_l = jnp.where(low, cos, 1.0), jnp.where(low, sin, 0.0)
    for c, kind in enumerate(groups):
        a = acc[:, c * LANE:(c + 1) * LANE]
        if kind != "N":
            if 2 * half == LANE:
                partner = pltpu.roll(a, half, 1)
            else:
                partner = jnp.where(lane % (2 * half) < half, pltpu.roll(a, LANE - half, 1), pltpu.roll(a, half, 1))
            a = a * cos + partner * sin if kind == "R" else a * cos_l + partner * sin_l
        o_ref[:, c * LANE:(c + 1) * LANE] = a.astype(o_ref.dtype)


def _proj(x, w, *, k_block=0, k_size=None, rms_g=None, rope=None, seq=None, out_dtype=F32, n_tail=0, name="proj"):
    m = x.shape[0]
    k_size = x.shape[1] if k_size is None else k_size
    n = w.shape[1]
    n_main = n - n_tail
    out_bytes = jnp.dtype(out_dtype).itemsize
    for tm in (512, 256, 128):
        est = k_size * n * 2 + 2 * tm * k_size * x.dtype.itemsize + 2 * tm * n * out_bytes + 3 * tm * n * 4
        if m % tm == 0 and est <= VMEM_LIMIT_BYTES * 3 // 4:
            break
    in_specs = [pl.BlockSpec((tm, k_size), lambda i: (i, k_block)),
                pl.BlockSpec((k_size, n), lambda i: (0, 0), pipeline_mode=pl.Buffered(1))]
    args = [x, w]
    if rms_g is not None:
        in_specs.append(pl.BlockSpec((1, k_size), lambda i: (0, 0)))
        args.append(rms_g.reshape(1, k_size).astype(F32))
    half, groups = 0, ()
    if rope is not None:
        cos, sin, half, groups = rope
        sb = seq // tm
        in_specs += [pl.BlockSpec((tm, LANE), lambda i: (i % sb, 0))] * 2
        args += [cos, sin]
    out_specs = [pl.BlockSpec((tm, n_main), lambda i: (i, 0))]
    out_shape = [jax.ShapeDtypeStruct((m, n_main), out_dtype)]
    if n_tail:
        out_specs.append(pl.BlockSpec((tm, n_tail), lambda i: (i, 0)))
        out_shape.append(jax.ShapeDtypeStruct((m, n_tail), F32))
    out = pl.pallas_call(
        functools.partial(_proj_kernel, has_rms=rms_g is not None, groups=tuple(groups), half=half, n_tail=n_tail),
        grid=(m // tm,),
        in_specs=in_specs,
        out_specs=out_specs,
        out_shape=out_shape,
        compiler_params=_params("parallel"),
        name=name,
    )(*args)
    return out if n_tail else out[0]


def _ffn_up_kernel(x_ref, wg_ref, wu_ref, o_ref, wg16, wu16):
    @pl.when(pl.program_id(1) == 0)
    def _():
        wg16[...] = wg_ref[...].astype(BF)
        wu16[...] = wu_ref[...].astype(BF)

    x = x_ref[...]
    g = jnp.dot(x, wg16[...], preferred_element_type=F32)
    u = jnp.dot(x, wu16[...], preferred_element_type=F32)
    o_ref[...] = (g * jax.nn.sigmoid(g) * u).astype(o_ref.dtype)


def _ffn_up(x16, wg_all, wu_all, layer, idx):
    m, k = x16.shape
    n = wg_all.shape[-1]
    tm = _pick(m, (1024, 512, 256, 128))
    tn = _pick(n, (512, 256, 128))
    w_spec = pl.BlockSpec((None, None, k, tn), lambda j, i: (layer, idx, 0, j))
    return pl.pallas_call(
        _ffn_up_kernel,
        grid=(n // tn, m // tm),
        in_specs=[pl.BlockSpec((tm, k), lambda j, i: (i, 0)), w_spec, w_spec],
        out_specs=pl.BlockSpec((tm, tn), lambda j, i: (i, j)),
        out_shape=jax.ShapeDtypeStruct((m, n), BF),
        scratch_shapes=[pltpu.VMEM((k, tn), BF), pltpu.VMEM((k, tn), BF)],
        compiler_params=_params("parallel", "arbitrary"),
        name="ffn_up",
    )(x16, wg_all, wu_all)


def _mm_res_ln_kernel(*refs, n_x, alpha, coef):
    x_refs = refs[:n_x]
    w_ref, res_ref, g_ref, b_ref, o32_ref, o16_ref = refs[n_x:]
    x = x_refs[0][...] if n_x == 1 else jnp.concatenate([r[...] for r in x_refs], axis=1)
    y = alpha * res_ref[...] + coef * jnp.dot(x, w_ref[...], preferred_element_type=F32)
    mu = jnp.mean(y, -1, keepdims=True)
    yc = y - mu
    var = jnp.mean(yc * yc, -1, keepdims=True)
    out = yc * lax.rsqrt(var + LN_EPS) * g_ref[...] + b_ref[...]
    o32_ref[...] = out
    o16_ref[...] = out.astype(BF)


def _mm_res_ln(x_parts, w_all, w_idx, res, g, b, *, alpha, coef, name):
    m = x_parts[0].shape[0]
    k, n = w_all.shape[-2:]
    tm = _pick(m, (256, 128))
    lead = (None,) * len(w_idx)
    return pl.pallas_call(
        functools.partial(_mm_res_ln_kernel, n_x=len(x_parts), alpha=alpha, coef=coef),
        grid=(m // tm,),
        in_specs=[pl.BlockSpec((tm, xp.shape[1]), lambda i: (i, 0)) for xp in x_parts] + [
                  pl.BlockSpec(lead + (k, n), lambda i: tuple(w_idx) + (0, 0), pipeline_mode=pl.Buffered(1)),
                  pl.BlockSpec((tm, n), lambda i: (i, 0)),
                  pl.BlockSpec((1, n), lambda i: (0, 0)),
                  pl.BlockSpec((1, n), lambda i: (0, 0))],
        out_specs=[pl.BlockSpec((tm, n), lambda i: (i, 0)),
                   pl.BlockSpec((tm, n), lambda i: (i, 0))],
        out_shape=[jax.ShapeDtypeStruct((m, n), F32), jax.ShapeDtypeStruct((m, n), BF)],
        compiler_params=_params("parallel"),
        name=name,
    )(*x_parts, w_all, res, g.reshape(1, n).astype(F32), b.reshape(1, n).astype(F32))


def _attn_kernel(qi_ref, kj_ref, fst_ref, lst_ref, msk_ref, *refs, variant, n_grp, n_rep, tq, tk, rc, dv, c_exp,
                 window, seq, has_sel, has_sink, gate_col, has_add, two_paths):
    it = iter(refs)
    if variant == "mla":
        qn_ref, qr_ref, kn_ref, kp_ref, v_ref = (next(it) for _ in range(5))
    elif variant == "swa":
        q_ref, kv_ref = next(it), next(it)
    else:
        q_ref, k_ref, v_ref = next(it), next(it), next(it)
    sel_ref = next(it) if has_sel else None
    sink_ref = next(it) if has_sink else None
    gl_ref, gb_ref = (next(it), next(it)) if gate_col is not None else (None, None)
    add_ref = next(it) if has_add else None
    o_ref = next(it)
    q_sc, m_sc, acc_sc, bias_sc = next(it), next(it), next(it), next(it)
    p = pl.program_id(1)
    qi = qi_ref[p]
    kj = kj_ref[p]
    rows = n_rep * tq

    @pl.when(fst_ref[p] == 1)
    def _():
        if has_sink:
            m_sc[...] = sink_ref[0]
            lane = lax.broadcasted_iota(jnp.int32, acc_sc.shape, 1)
            acc_sc[...] = jnp.where(lane >= dv, 1.0, 0.0)
        else:
            m_sc[...] = jnp.full_like(m_sc, NEG)
            acc_sc[...] = jnp.zeros_like(acc_sc)
        if variant == "mla":
            half_of_lane = lax.broadcasted_iota(jnp.int32, (tq, LANE), 1) // MLA_ROPE
            for r in range(n_rep):
                rope_pair = qr_ref[:, (r // 2) * LANE:(r // 2 + 1) * LANE].astype(F32) * c_exp
                q_sc[r * tq:(r + 1) * tq, :LANE] = (qn_ref[:, r * LANE:(r + 1) * LANE].astype(F32) * c_exp).astype(BF)
                q_sc[r * tq:(r + 1) * tq, LANE:] = jnp.where(half_of_lane == r % 2, rope_pair, 0.0).astype(BF)
        elif variant == "swa":
            low = lax.broadcasted_iota(jnp.int32, (tq, LANE), 1) < dv
            for r in range(n_rep):
                chunk = q_ref[:, (r // 2) * LANE:(r // 2 + 1) * LANE].astype(F32)
                if r % 2 == 1:
                    chunk = pltpu.roll(chunk, dv, 1)
                q_sc[r * tq:(r + 1) * tq, :] = jnp.where(low, chunk * c_exp, 0.0).astype(BF)
        else:
            dq = q_sc.shape[1]
            for r in range(n_rep):
                q_sc[r * tq:(r + 1) * tq, :] = (q_ref[:, r * dq:(r + 1) * dq].astype(F32) * c_exp).astype(BF)

    ones = jnp.ones((tk, dv), BF)
    if variant == "mla":
        ks = [jnp.concatenate([kn_ref[:, r * LANE:(r + 1) * LANE], kp_ref[...]], axis=1) for r in range(n_rep)]
        vs = [jnp.concatenate([v_ref[:, r * dv:(r + 1) * dv], ones], axis=1) for r in range(n_rep)]
    elif variant == "swa":
        swapped = pltpu.roll(kv_ref[...].astype(F32), dv, 1)
        ks = [kv_ref[...]]
        vs = [jnp.where(lax.broadcasted_iota(jnp.int32, (tk, LANE), 1) < dv, swapped, 1.0).astype(BF)]
    else:
        ks = [k_ref[...]]
        vs = [jnp.concatenate([v_ref[...], ones], axis=1)]

    def kv_of(u):
        return (u * rc) // tq if len(ks) > 1 else 0

    def tile_bias():
        row = lax.broadcasted_iota(jnp.int32, (tq, tk), 0)
        col = lax.broadcasted_iota(jnp.int32, (tq, tk), 1)
        dist = qi * tq - kj * tk + row - col
        mask = dist >= 0
        if window < seq:
            mask = mask & (dist < window)
        if has_sel:
            n_blk = sel_ref.shape[-1]
            blk_of_key = (kj * tk + lax.broadcasted_iota(jnp.int32, (n_blk, tk), 1)) // SLC_BLOCK
            expand = jnp.where(lax.broadcasted_iota(jnp.int32, (n_blk, tk), 0) == blk_of_key, 1.0, 0.0).astype(BF)
            picked = jnp.dot(sel_ref[0].astype(BF), expand, preferred_element_type=F32)
            mask = mask & (picked > 0.5)
        return jnp.where(mask, 0.0, NEG)

    def logits(u):
        return lax.dot_general(q_sc[u * rc:(u + 1) * rc, :], ks[kv_of(u)], (((1,), (1,)), ((), ())),
                               preferred_element_type=F32)

    def online_softmax(masked):
        if masked and has_sel:
            bias_sc[...] = tile_bias()
        if masked and not has_sel:
            rel = (lax.broadcasted_iota(jnp.int32, (rc, tk), 0) - lax.broadcasted_iota(jnp.int32, (rc, tk), 1))
        n_units = rows // rc
        s_next = logits(0)
        for u in range(n_units):
            s = s_next
            if u + 1 < n_units:
                s_next = logits(u + 1)
            us = slice(u * rc, (u + 1) * rc)
            m_prev = m_sc[us, :]
            if masked and has_sel:
                s = s + bias_sc[(u * rc) % tq:(u * rc) % tq + rc, :]
            elif masked:
                base = qi * tq - kj * tk + (u * rc) % tq
                visible = rel >= -base
                if window < seq:
                    visible = visible & (rel < window - base)
                s = jnp.where(visible, s, NEG)
            m_new = jnp.maximum(m_prev, jnp.max(s, -1, keepdims=True))
            w = jnp.exp2(s - _lane_tile(m_new, tk // LANE))
            m_sc[us, :] = m_new
            alpha = jnp.exp2(m_prev - m_new)
            acc_sc[us, :] = (_lane_tile(alpha, acc_sc.shape[1] // LANE) * acc_sc[us, :]
                             + jnp.dot(w.astype(BF), vs[kv_of(u)], preferred_element_type=F32))

    if not two_paths:
        online_softmax(True)
    else:
        pl.when(msk_ref[p] == 1)(lambda: online_softmax(True))
        pl.when(msk_ref[p] == 0)(lambda: online_softmax(False))

    @pl.when(lst_ref[p] == 1)
    def _():
        acc = acc_sc[...]
        if dv % LANE == 0:
            o = acc[:, :dv] / acc[:, dv:]
        else:
            o = acc[:, :dv] / acc[:, dv:dv + 1]
        if gate_col is not None:
            gates = jax.nn.sigmoid(gl_ref[...] + gb_ref[...])
        for r in range(n_rep):
            o_r = o[r * tq:(r + 1) * tq]
            if gate_col is not None:
                o_r = o_r * gates[:, 3 * r + gate_col:3 * r + gate_col + 1]
            if has_add:
                o_r = o_r + add_ref[:, r * dv:(r + 1) * dv].astype(F32)
            o_ref[:, r * dv:(r + 1) * dv] = o_r.astype(o_ref.dtype)


def _attn_call(variant, qkv_args, qkv_specs, *, batch, seq, n_grp, n_rep, dq, dv, scale, window, tq, tk,
               sel=None, sinks=None, gate=None, add=None, name="attn"):
    rows = n_rep * tq
    nq, nk = seq // tq, seq // tk
    pairs, fst, lst, msk = [], [], [], []
    for i in range(nq):
        lo = max(0, i * tq - (window - 1)) // tk
        hi = (i * tq + tq - 1) // tk
        for j in range(lo, hi + 1):
            pairs.append((i, j))
            fst.append(int(j == lo))
            lst.append(int(j == hi))
            inside = i * tq - (j * tk + tk - 1) >= 0 and i * tq + tq - 1 - j * tk < window
            msk.append(int(sel is not None or not inside))
    two_paths = 0 < sum(msk) < len(msk)
    qi = jnp.asarray([a for a, _ in pairs], jnp.int32)
    kj = jnp.asarray([b for _, b in pairs], jnp.int32)

    def q_rows(b, p, qi, kj):
        return (b // n_grp) * nq + qi[p]

    def k_rows(b, p, qi, kj):
        return (b // n_grp) * nk + kj[p]

    in_specs = [mk(q_rows, k_rows) for mk in qkv_specs]
    args = list(qkv_args)
    if sel is not None:
        in_specs.append(pl.BlockSpec((1, tq, sel.shape[-1]), lambda b, p, qi, kj, *_: (b, qi[p], 0)))
        args.append(sel)
    if sinks is not None:
        m0 = jnp.broadcast_to((sinks.astype(F32) * LOG2E)[:, :, None, None], (n_grp, n_rep, tq, LANE))
        in_specs.append(pl.BlockSpec((1, rows, LANE), lambda b, p, qi, kj, *_: (b % n_grp, 0, 0)))
        args.append(m0.reshape(n_grp, rows, LANE))
    gate_col = None
    if gate is not None:
        gl, gb, gate_col = gate
        in_specs += [pl.BlockSpec((tq, LANE), lambda b, p, qi, kj, *_: (q_rows(b, p, qi, kj), b % n_grp)),
                     pl.BlockSpec((1, LANE), lambda b, p, qi, kj, *_: (0, b % n_grp))]
        args += [gl, gb]
    out_spec = pl.BlockSpec((tq, n_rep * dv), lambda b, p, qi, kj, *_: (q_rows(b, p, qi, kj), b % n_grp))
    if add is not None:
        in_specs.append(out_spec)
        args.append(add)
    rc = min(tq, UNIT_ROWS)
    kern = functools.partial(_attn_kernel, variant=variant, n_grp=n_grp, n_rep=n_rep, tq=tq, tk=tk, rc=rc, dv=dv,
                             c_exp=scale * LOG2E, window=window, seq=seq, has_sel=sel is not None,
                             has_sink=sinks is not None, gate_col=gate_col, has_add=add is not None,
                             two_paths=two_paths)
    return pl.pallas_call(
        kern,
        grid_spec=pltpu.PrefetchScalarGridSpec(
            num_scalar_prefetch=5,
            grid=(batch * n_grp, len(pairs)),
            in_specs=in_specs,
            out_specs=out_spec,
            scratch_shapes=[pltpu.VMEM((rows, dq), BF), pltpu.VMEM((rows, LANE), F32),
                            pltpu.VMEM((rows, 2 * dv), F32), pltpu.VMEM((tq, tk), F32)]),
        out_shape=jax.ShapeDtypeStruct((batch * seq, n_grp * n_rep * dv), BF),
        compiler_params=_params("parallel", "arbitrary"),
        name=name,
    )(qi, kj, jnp.asarray(fst, jnp.int32), jnp.asarray(lst, jnp.int32), jnp.asarray(msk, jnp.int32), *args)


def _spec(shape, rows_of, col_fn):
    def make(q_rows, k_rows):
        rf = q_rows if rows_of == "q" else k_rows
        return pl.BlockSpec(shape, lambda b, p, qi, kj, *_: (rf(b, p, qi, kj), col_fn(b)))
    return make


def _compress_kernel(x_ref, pa_ref, pb_ref, w1a_ref, w1b_ref, w2_ref, o_ref):
    x = x_ref[0]
    a = jnp.dot((x + pa_ref[...]).astype(BF), w1a_ref[...], preferred_element_type=F32)
    b = jnp.dot((x + pb_ref[...]).astype(BF), w1b_ref[...], preferred_element_type=F32)
    h = jax.nn.gelu(a + pltpu.roll(b, b.shape[0] - 1, 0))
    o_ref[0] = jnp.dot(h.astype(BF), w2_ref[...], preferred_element_type=F32).astype(o_ref.dtype)


def _compress(x, pos, w1, w2, name):
    bg, nch, width = x.shape
    d = w2.shape[1]
    hid = w1.shape[1]
    pa = pos[:CMP_STRIDE].reshape(1, width).astype(F32)
    pb = pos[CMP_STRIDE:].reshape(1, width).astype(F32)
    w1 = w1.astype(BF)
    return pl.pallas_call(
        _compress_kernel,
        grid=(bg,),
        in_specs=[pl.BlockSpec((1, nch, width), lambda b: (b, 0, 0)),
                  pl.BlockSpec((1, width), lambda b: (0, 0)),
                  pl.BlockSpec((1, width), lambda b: (0, 0)),
                  pl.BlockSpec((width, hid), lambda b: (0, 0)),
                  pl.BlockSpec((width, hid), lambda b: (1, 0)),
                  pl.BlockSpec((hid, d), lambda b: (0, 0))],
        out_specs=pl.BlockSpec((1, nch, d), lambda b: (b, 0, 0)),
        out_shape=jax.ShapeDtypeStruct((bg, nch, d), BF),
        compiler_params=_params("parallel"),
        name=name,
    )(x, pa, pb, w1, w1, w2.astype(BF))


def _cmp_attn_kernel(q_ref, k_ref, v_ref, ovt_ref, gl_ref, gb_ref, o_ref, sel_ref, *, n_rep, tq, c_exp):
    qi = pl.program_id(1)
    nc, d = k_ref.shape[1], k_ref.shape[2]
    rows = n_rep * tq
    q = jnp.concatenate([q_ref[:, r * d:(r + 1) * d] for r in range(n_rep)], axis=0)
    s = lax.dot_general(q, k_ref[0], (((1,), (1,)), ((), ())), preferred_element_type=F32)
    qpos = qi * tq + lax.broadcasted_iota(jnp.int32, (tq, nc), 0)
    cmp_end = lax.broadcasted_iota(jnp.int32, (tq, nc), 1) * CMP_STRIDE + (CMP_BLOCK - 1)
    cmask = (cmp_end <= qpos)[None]
    s = jnp.where(cmask, s.reshape(n_rep, tq, nc), NEG)
    e = jnp.exp2((s - jnp.max(s, -1, keepdims=True)) * c_exp)
    e = jnp.where(cmask, e, 0.0).reshape(rows, nc)
    ext = jnp.dot(e.astype(BF), v_ref[0], preferred_element_type=F32)
    den = ext[:, d:]
    inv = 1.0 / jnp.where(den > 0.0, den, 1.0)
    gates = jax.nn.sigmoid(gl_ref[...] + gb_ref[...])
    o = ext[:, :d] * inv
    for r in range(n_rep):
        o_ref[:, r * d:(r + 1) * d] = (o[r * tq:(r + 1) * tq] * gates[:, 3 * r:3 * r + 1]).astype(o_ref.dtype)
    inv_c = inv[:, :nc] if nc <= d else _lane_tile(inv, nc // d)
    prob = e * inv_c
    psum = jnp.sum(prob.reshape(n_rep, tq, nc), axis=0)
    p_hi = psum.astype(BF)
    p_lo = (psum - p_hi.astype(F32)).astype(BF)
    contract_last = (((1,), (1,)), ((), ()))
    imp = (lax.dot_general(ovt_ref[...], p_hi, contract_last, preferred_element_type=F32)
           + lax.dot_general(ovt_ref[...], p_lo, contract_last, preferred_element_type=F32))
    ns = imp.shape[0]
    blk = lax.broadcasted_iota(jnp.int32, (ns, tq), 0)
    cur = (qi * tq + lax.broadcasted_iota(jnp.int32, (ns, tq), 1)) // SLC_BLOCK
    eligible = blk <= cur
    forced = (blk == 0) | (blk == cur) | (blk == cur - 1)
    score = jnp.where(eligible, jnp.where(forced, FORCE_SCORE, imp), -1.0)
    beaten_by = jnp.zeros((ns, tq), F32)
    for kk in range(ns):
        sk = score[kk:kk + 1, :]
        beats = (sk > score) | ((sk == score) & (blk > kk))
        beaten_by = beaten_by + jnp.where(beats, 1.0, 0.0)
    n_sel = min(N_SEL, ns)
    sel_t = jnp.where((beaten_by < n_sel) & (score >= 0.0), 1.0, 0.0)
    sel_sq = jnp.concatenate([sel_t, jnp.zeros((tq - ns, tq), F32)], axis=0) if ns < tq else sel_t
    sel_ref[0] = sel_sq.T[:, :ns]


def _cmp_attention(z, kcmp, vcmp, gl, gb, *, batch, seq, n_grp, n_rep, scale, tq):
    bg, nc, d = kcmp.shape
    ns = seq // SLC_BLOCK
    nq = seq // tq
    starts = np.arange(nc) * CMP_STRIDE
    jb = np.arange(ns)
    overlap = ((starts[:, None] < (jb[None, :] + 1) * SLC_BLOCK)
               & (starts[:, None] + CMP_BLOCK > jb[None, :] * SLC_BLOCK)).astype(np.float32)
    v_ext = jnp.concatenate([vcmp, jnp.ones_like(vcmp)], -1)
    tok_spec = pl.BlockSpec((tq, n_rep * d), lambda b, i: ((b // n_grp) * nq + i, b % n_grp))
    return pl.pallas_call(
        functools.partial(_cmp_attn_kernel, n_rep=n_rep, tq=tq, c_exp=scale * LOG2E),
        grid=(bg, nq),
        in_specs=[tok_spec,
                  pl.BlockSpec((1, nc, d), lambda b, i: (b, 0, 0)),
                  pl.BlockSpec((1, nc, 2 * d), lambda b, i: (b, 0, 0)),
                  pl.BlockSpec((ns, nc), lambda b, i: (0, 0)),
                  pl.BlockSpec((tq, LANE), lambda b, i: ((b // n_grp) * nq + i, b % n_grp)),
                  pl.BlockSpec((1, LANE), lambda b, i: (0, b % n_grp))],
        out_specs=[tok_spec, pl.BlockSpec((1, tq, ns), lambda b, i: (b, i, 0))],
        out_shape=[jax.ShapeDtypeStruct((batch * seq, n_grp * n_rep * d), BF),
                   jax.ShapeDtypeStruct((bg, seq, ns), F32)],
        compiler_params=_params("parallel", "parallel"),
        name="nsa_cmp_attn",
    )(z, kcmp, v_ext, jnp.asarray(overlap.T, BF), gl, gb)


def _rope_tables(seq, dim):
    inv = 1.0 / (ROPE_THETA ** (jnp.arange(0, dim, 2, dtype=F32) / dim))
    ang = jnp.arange(seq, dtype=F32)[:, None] * inv[None, :]
    cos_h = jnp.concatenate([jnp.cos(ang), jnp.cos(ang)], -1)
    sin_h = jnp.concatenate([-jnp.sin(ang), jnp.sin(ang)], -1)
    return jnp.tile(cos_h, (1, LANE // dim)), jnp.tile(sin_h, (1, LANE // dim))


def _pad_cols(w, n):
    return jnp.pad(w, ((0, 0), (0, n - w.shape[1])))


def _even_mixer(x16, batch, seq, w_in, q_norm, w_uq, kv_norm, w_ukv, sinks):
    H, dn, dr, dv = MLA_HEADS, MLA_NOPE, MLA_ROPE, MLA_V
    G, R, ds = SWA_KV_HEADS, SWA_HEADS // SWA_KV_HEADS, SWA_HEAD_DIM
    n_lat = MLA_Q_RANK + MLA_KV_RANK
    o_qs = n_lat + dr
    o_ks = o_qs + SWA_HEADS * ds
    o_vs = o_ks + G * ds
    w_kpe = w_in[:, n_lat:o_qs]
    cols = [w_in[:, o_qs:o_ks]]
    for g in range(G):
        cols += [w_in[:, o_ks + g * ds:o_ks + (g + 1) * ds], w_in[:, o_vs + g * ds:o_vs + (g + 1) * ds]]
    cols += [w_kpe, w_kpe]
    w_rg = jnp.concatenate(cols, -1)
    n_q = SWA_HEADS * ds
    cos, sin = _rope_tables(seq, ds)
    groups = "R" * (n_q // LANE) + "L" * G + "R"
    z, z_lat = _proj(x16, jnp.concatenate([w_rg, w_in[:, :n_lat]], -1).astype(BF),
                     rope=(cos, sin, ds // 2, groups), seq=seq, out_dtype=BF, n_tail=n_lat, name="even_in")

    w_uq_p = jnp.concatenate([w_uq.reshape(-1, H, dn + dr)[:, :, :dn].reshape(-1, H * dn),
                              w_uq.reshape(-1, H, dn + dr)[:, :, dn:].reshape(-1, H * dr)], -1)
    w_ukv_p = jnp.concatenate([w_ukv.reshape(-1, H, dn + dv)[:, :, :dn].reshape(-1, H * dn),
                               w_ukv.reshape(-1, H, dn + dv)[:, :, dn:].reshape(-1, H * dv)], -1)
    cos_q, sin_q = _rope_tables(seq, dr)
    q = _proj(z_lat, w_uq_p.astype(BF), k_block=0, k_size=MLA_Q_RANK, rms_g=q_norm,
              rope=(cos_q, sin_q, dr // 2, "N" * (H * dn // LANE) + "R" * (H * dr // LANE)), seq=seq, out_dtype=BF,
              name="mla_q_up")
    kv = _proj(z_lat, w_ukv_p.astype(BF), k_block=1, k_size=MLA_KV_RANK, rms_g=kv_norm, out_dtype=BF,
               name="mla_kv_up")
    tq = _pick(seq, (512, 256, 128))
    tk = _pick(seq, (512, 256, 128))
    kpe_blk = (n_q + 2 * G * ds) // LANE
    hp = MLA_HEADS_PER_STEP
    n_hg = H // hp
    o_mla = _attn_call(
        "mla", [q, q, kv, z, kv],
        [_spec((tq, hp * dn), "q", lambda b: b % n_hg),
         _spec((tq, hp * dr), "q", lambda b: H * dn // (hp * dr) + b % n_hg),
         _spec((tk, hp * dn), "k", lambda b: b % n_hg),
         _spec((tk, LANE), "k", lambda b: kpe_blk),
         _spec((tk, hp * dv), "k", lambda b: H * dn // (hp * dv) + b % n_hg)],
        batch=batch, seq=seq, n_grp=n_hg, n_rep=hp, dq=2 * LANE, dv=dv, scale=(dn + dr) ** -0.5, window=seq,
        tq=tq, tk=tk, name="mla_attn")

    tq = _pick(seq, (256, 128))
    o_swa = _attn_call(
        "swa", [z, z],
        [_spec((tq, R * ds), "q", lambda b: b % G),
         _spec((tq, LANE), "k", lambda b: n_q // LANE + b % G)],
        batch=batch, seq=seq, n_grp=G, n_rep=R, dq=LANE, dv=ds, scale=ds ** -0.5, window=SWA_WINDOW,
        tq=tq, tk=tq, sinks=sinks.reshape(G, R), name="swa_attn")
    return [o_mla, o_swa]


def _odd_mixer(x16, batch, seq, w_in, gate_b, cmp_pos_k, cmp_pos_v, k_w1, k_w2, v_w1, v_w2):
    G, d = NSA_KV_GROUPS, NSA_HEAD_DIM
    H = NSA_HEADS
    R = H // G
    kvw = G * d
    o_kc = H * d
    offs = [o_kc + i * kvw for i in range(7)]
    cos, sin = _rope_tables(seq, d)
    groups = "R" * H + ("R" * G + "N" * G) * 3
    w_gl = jnp.concatenate([_pad_cols(w_in[:, offs[6] + g * 3 * R:offs[6] + (g + 1) * 3 * R], LANE)
                            for g in range(G)], -1)
    z, gl = _proj(x16, jnp.concatenate([w_in[:, :offs[6]], w_gl], -1).astype(BF),
                  rope=(cos, sin, d // 2, groups), seq=seq, out_dtype=BF, n_tail=G * LANE, name="odd_in")
    gb = jnp.concatenate([jnp.pad(gate_b[g * 3 * R:(g + 1) * 3 * R].astype(F32), (0, LANE - 3 * R))
                          for g in range(G)]).reshape(1, G * LANE)

    def chunks(a):
        a = a.astype(F32).reshape(batch, seq // CMP_STRIDE, CMP_STRIDE, G, d).transpose(0, 3, 1, 2, 4)
        return a.reshape(batch * G, seq // CMP_STRIDE, CMP_STRIDE * d)

    scale = d ** -0.5
    kcmp = _compress(chunks(z[:, offs[0]:offs[1]]), cmp_pos_k, k_w1, k_w2, "nsa_compress_k")
    vcmp = _compress(chunks(z[:, offs[1]:offs[2]]), cmp_pos_v, v_w1, v_w2, "nsa_compress_v")
    tq = _pick(seq, (256, 128))
    o, sel = _cmp_attention(z, kcmp, vcmp, gl, gb, batch=batch, seq=seq, n_grp=G, n_rep=R, scale=scale, tq=tq)

    def branch(k_off, v_off, **kw):
        return _attn_call(
            "gqa", [z, z, z],
            [_spec((kw["tq"], R * d), "q", lambda b: b % G),
             _spec((kw["tk"], d), "k", lambda b: k_off // d + b % G),
             _spec((kw["tk"], d), "k", lambda b: v_off // d + b % G)],
            batch=batch, seq=seq, n_grp=G, n_rep=R, dq=d, dv=d, scale=scale, **kw)

    o = branch(offs[2], offs[3], window=seq, tq=_pick(seq, (512, 256, 128)), tk=_pick(seq, (512, 256, 128)),
               sel=sel, gate=(gl, gb, 1), add=o, name="nsa_slc_attn")
    o = branch(offs[4], offs[5], window=NSA_WINDOW, tq=tq, tk=_pick(seq, (256, 128)), gate=(gl, gb, 2), add=o,
               name="nsa_win_attn")
    return [o]


def kernel(x, ln_g, ln_b, ffn_w_gate, ffn_w_up, ffn_w_down, even_w_in, mla_q_norm, mla_w_uq, mla_kv_norm, mla_w_ukv, swa_sinks, even_w_out, odd_w_in, nsa_gate_b, nsa_cmp_pos_k, nsa_cmp_pos_v, nsa_cmp_k_w1, nsa_cmp_k_w2, nsa_cmp_v_w1, nsa_cmp_v_w2, odd_w_out):
    batch, seq, dm = x.shape
    depth = ln_g.shape[0]
    alpha = float((2 * depth) ** 0.25)
    x32 = x.reshape(batch * seq, dm)
    x16 = x32.astype(BF)
    w_down16 = ffn_w_down.astype(BF)
    even_out16 = even_w_out.astype(BF)
    odd_out16 = odd_w_out.astype(BF)

    def ffn(x32, x16, l, idx):
        h = _ffn_up(x16, ffn_w_gate, ffn_w_up, l, idx)
        return _mm_res_ln([h], w_down16, (l, idx), x32, ln_g[l, 2 * idx], ln_b[l, 2 * idx], alpha=alpha, coef=0.5,
                          name="ffn_down_ln")

    for l in range(depth):
        x32, x16 = ffn(x32, x16, l, 0)
        j = l // 2
        if l % 2 == 0:
            o = _even_mixer(x16, batch, seq, even_w_in[j], mla_q_norm[j], mla_w_uq[j], mla_kv_norm[j],
                            mla_w_ukv[j], swa_sinks[j])
            w_out = even_out16
        else:
            o = _odd_mixer(x16, batch, seq, odd_w_in[j], nsa_gate_b[j], nsa_cmp_pos_k[j], nsa_cmp_pos_v[j],
                           nsa_cmp_k_w1[j], nsa_cmp_k_w2[j], nsa_cmp_v_w1[j], nsa_cmp_v_w2[j])
            w_out = odd_out16
        x32, x16 = _mm_res_ln(o, w_out, (j,), x32, ln_g[l, 1], ln_b[l, 1], alpha=alpha, coef=1.0,
                              name="mixer_out_ln")
        x32, x16 = ffn(x32, x16, l, 1)
    return x32.reshape(batch, seq, dm)
```

```python
import functools
import math

import numpy as np
import jax
import jax.numpy as jnp
from jax import lax
from jax.experimental import pallas as pl
from jax.experimental.pallas import tpu as pltpu

F32 = jnp.float32
BF = jnp.bfloat16

ROPE_THETA = 10000.0
LN_EPS = 1e-5
RMS_EPS = 1e-6
NEG = -1e30
FORCE_SCORE = 1e9
LOG2E = math.log2(math.e)

MLA_HEADS, MLA_NOPE, MLA_ROPE, MLA_V = 8, 128, 64, 128
MLA_Q_RANK, MLA_KV_RANK = 512, 512
SWA_HEADS, SWA_KV_HEADS, SWA_HEAD_DIM, SWA_WINDOW = 16, 2, 64, 128
NSA_HEADS, NSA_KV_GROUPS, NSA_HEAD_DIM = 16, 2, 128
CMP_BLOCK, CMP_STRIDE, CMP_HIDDEN = 32, 16, 256
SLC_BLOCK, N_SEL, NSA_WINDOW = 64, 8, 512

LANE = 128
UNIT_ROWS = 128
MLA_HEADS_PER_STEP = 4
VMEM_LIMIT_BYTES = 52 * 1024 * 1024


def _params(*sem):
    return pltpu.CompilerParams(dimension_semantics=sem, vmem_limit_bytes=VMEM_LIMIT_BYTES)


def _pick(n, candidates):
    for c in candidates:
        if n % c == 0:
            return c
    raise ValueError(f"no tile in {candidates} divides {n}")


def _lane_tile(x, n):
    return x if n == 1 else jnp.concatenate([x] * n, axis=1)


def _proj_kernel(*refs, has_rms, groups, half, n_tail):
    it = iter(refs)
    x_ref, w_ref = next(it), next(it)
    g_ref = next(it) if has_rms else None
    cos_ref, sin_ref = (next(it), next(it)) if groups else (None, None)
    o_ref = next(it)
    x = x_ref[...]
    if has_rms:
        xf = x.astype(F32)
        x = xf * lax.rsqrt(jnp.mean(xf * xf, -1, keepdims=True) + RMS_EPS) * g_ref[...]
    acc = jnp.dot(x.astype(BF), w_ref[...], preferred_element_type=F32)
    if n_tail:
        tail_ref = next(it)
        tail_ref[...] = acc[:, acc.shape[1] - n_tail:]
        acc = acc[:, :acc.shape[1] - n_tail]
    if not groups:
        o_ref[...] = acc.astype(o_ref.dtype)
        return
    cos, sin = cos_ref[...], sin_ref[...]
    lane = lax.broadcasted_iota(jnp.int32, cos.shape, 1)
    low = lane < LANE // 2
    cos_l, sin_l = jnp.where(low, cos, 1.0), jnp.where(low, sin, 0.0)
    for c, kind in enumerate(groups):
        a = acc[:, c * LANE:(c + 1) * LANE]
        if kind != "N":
            if 2 * half == LANE:
                partner = pltpu.roll(a, half, 1)
            else:
                partner = jnp.where(lane % (2 * half) < half, pltpu.roll(a, LANE - half, 1), pltpu.roll(a, half, 1))
            a = a * cos + partner * sin if kind == "R" else a * cos_l + partner * sin_l
        o_ref[:, c * LANE:(c + 1) * LANE] = a.astype(o_ref.dtype)


def _proj(x, w, *, k_block=0, k_size=None, rms_g=None, rope=None, seq=None, out_dtype=F32, n_tail=0, name="proj"):
    m = x.shape[0]
    k_size = x.shape[1] if k_size is None else k_size
    n = w.shape[1]
    n_main = n - n_tail
    out_bytes = jnp.dtype(out_dtype).itemsize
    for tm in (512, 256, 128):
        est = k_size * n * 2 + 2 * tm * k_size * x.dtype.itemsize + 2 * tm * n * out_bytes + 3 * tm * n * 4
        if m % tm == 0 and est <= VMEM_LIMIT_BYTES * 3 // 4:
            break
    in_specs = [pl.BlockSpec((tm, k_size), lambda i: (i, k_block)),
                pl.BlockSpec((k_size, n), lambda i: (0, 0), pipeline_mode=pl.Buffered(1))]
    args = [x, w]
    if rms_g is not None:
        in_specs.append(pl.BlockSpec((1, k_size), lambda i: (0, 0)))
        args.append(rms_g.reshape(1, k_size).astype(F32))
    half, groups = 0, ()
    if rope is not None:
        cos, sin, half, groups = rope
        sb = seq // tm
        in_specs += [pl.BlockSpec((tm, LANE), lambda i: (i % sb, 0))] * 2
        args += [cos, sin]
    out_specs = [pl.BlockSpec((tm, n_main), lambda i: (i, 0))]
    out_shape = [jax.ShapeDtypeStruct((m, n_main), out_dtype)]
    if n_tail:
        out_specs.append(pl.BlockSpec((tm, n_tail), lambda i: (i, 0)))
        out_shape.append(jax.ShapeDtypeStruct((m, n_tail), F32))
    out = pl.pallas_call(
        functools.partial(_proj_kernel, has_rms=rms_g is not None, groups=tuple(groups), half=half, n_tail=n_tail),
        grid=(m // tm,),
        in_specs=in_specs,
        out_specs=out_specs,
        out_shape=out_shape,
        compiler_params=_params("parallel"),
        name=name,
    )(*args)
    return out if n_tail else out[0]


def _ffn_up_kernel(x_ref, wg_ref, wu_ref, wd_ref, o_ref, wd16_ref, wg16, wu16):
    @pl.when(pl.program_id(1) == 0)
    def _():
        wg16[...] = wg_ref[...].astype(BF)
        wu16[...] = wu_ref[...].astype(BF)

    wd16_ref[...] = wd_ref[...].astype(BF)
    x = x_ref[...]
    g = jnp.dot(x, wg16[...], preferred_element_type=F32)
    u = jnp.dot(x, wu16[...], preferred_element_type=F32)
    o_ref[...] = (g * jax.nn.sigmoid(g) * u).astype(o_ref.dtype)


def _ffn_up(x16, wg_all, wu_all, wd_all, layer, idx):
    m, k = x16.shape
    n = wg_all.shape[-1]
    n_out = wd_all.shape[-1]
    tm = _pick(m, (1024, 512, 256, 128))
    tn = _pick(n, (512, 256, 128))
    ni = m // tm
    slab = n // ((n // tn) * ni)
    assert slab * (n // tn) * ni == n and slab % 16 == 0
    w_spec = pl.BlockSpec((None, None, k, tn), lambda j, i: (layer, idx, 0, j))
    return pl.pallas_call(
        _ffn_up_kernel,
        grid=(n // tn, ni),
        in_specs=[pl.BlockSpec((tm, k), lambda j, i: (i, 0)), w_spec, w_spec,
                  pl.BlockSpec((None, None, slab, n_out), lambda j, i: (layer, idx, j * ni + i, 0))],
        out_specs=[pl.BlockSpec((tm, tn), lambda j, i: (i, j)),
                   pl.BlockSpec((slab, n_out), lambda j, i: (j * ni + i, 0))],
        out_shape=[jax.ShapeDtypeStruct((m, n), BF), jax.ShapeDtypeStruct((n, n_out), BF)],
        scratch_shapes=[pltpu.VMEM((k, tn), BF), pltpu.VMEM((k, tn), BF)],
        compiler_params=_params("parallel", "arbitrary"),
        name="ffn_up",
    )(x16, wg_all, wu_all, wd_all)


def _mm_res_ln_kernel(*refs, n_x, alpha, coef):
    x_refs = refs[:n_x]
    w_ref, res_ref, g_ref, b_ref, o32_ref, o16_ref = refs[n_x:]
    x = x_refs[0][...] if n_x == 1 else jnp.concatenate([r[...] for r in x_refs], axis=1)
    y = alpha * res_ref[...] + coef * jnp.dot(x, w_ref[...], preferred_element_type=F32)
    mu = jnp.mean(y, -1, keepdims=True)
    yc = y - mu
    var = jnp.mean(yc * yc, -1, keepdims=True)
    out = yc * lax.rsqrt(var + LN_EPS) * g_ref[...] + b_ref[...]
    o32_ref[...] = out
    o16_ref[...] = out.astype(BF)


def _mm_res_ln(x_parts, w_all, w_idx, res, g, b, *, alpha, coef, name):
    m = x_parts[0].shape[0]
    k, n = w_all.shape[-2:]
    tm = _pick(m, (256, 128))
    lead = (None,) * len(w_idx)
    return pl.pallas_call(
        functools.partial(_mm_res_ln_kernel, n_x=len(x_parts), alpha=alpha, coef=coef),
        grid=(m // tm,),
        in_specs=[pl.BlockSpec((tm, xp.shape[1]), lambda i: (i, 0)) for xp in x_parts] + [
                  pl.BlockSpec(lead + (k, n), lambda i: tuple(w_idx) + (0, 0), pipeline_mode=pl.Buffered(1)),
                  pl.BlockSpec((tm, n), lambda i: (i, 0)),
                  pl.BlockSpec((1, n), lambda i: (0, 0)),
                  pl.BlockSpec((1, n), lambda i: (0, 0))],
        out_specs=[pl.BlockSpec((tm, n), lambda i: (i, 0)),
                   pl.BlockSpec((tm, n), lambda i: (i, 0))],
        out_shape=[jax.ShapeDtypeStruct((m, n), F32), jax.ShapeDtypeStruct((m, n), BF)],
        compiler_params=_params("parallel"),
        name=name,
    )(*x_parts, w_all, res, g.reshape(1, n).astype(F32), b.reshape(1, n).astype(F32))


def _attn_kernel(qi_ref, kj_ref, fst_ref, lst_ref, msk_ref, *refs, variant, n_grp, n_rep, tq, tk, rc, dv, c_exp,
                 window, seq, has_sel, has_sink, gate_col, has_add, two_paths):
    it = iter(refs)
    if variant == "mla":
        qn_ref, qr_ref, kn_ref, kp_ref, v_ref = (next(it) for _ in range(5))
    elif variant == "swa":
        q_ref, kv_ref = next(it), next(it)
    else:
        q_ref, k_ref, v_ref = next(it), next(it), next(it)
    sel_ref = next(it) if has_sel else None
    sink_ref = next(it) if has_sink else None
    gl_ref, gb_ref = (next(it), next(it)) if gate_col is not None else (None, None)
    add_ref = next(it) if has_add else None
    o_ref = next(it)
    q_sc, m_sc, acc_sc, bias_sc = next(it), next(it), next(it), next(it)
    p = pl.program_id(1)
    qi = qi_ref[p]
    kj = kj_ref[p]
    rows = n_rep * tq

    @pl.when(fst_ref[p] == 1)
    def _():
        if has_sink:
            m_sc[...] = sink_ref[0]
            lane = lax.broadcasted_iota(jnp.int32, acc_sc.shape, 1)
            acc_sc[...] = jnp.where(lane >= dv, 1.0, 0.0)
        else:
            m_sc[...] = jnp.full_like(m_sc, NEG)
            acc_sc[...] = jnp.zeros_like(acc_sc)
        if variant == "mla":
            half_of_lane = lax.broadcasted_iota(jnp.int32, (tq, LANE), 1) // MLA_ROPE
            for r in range(n_rep):
                rope_pair = qr_ref[:, (r // 2) * LANE:(r // 2 + 1) * LANE].astype(F32) * c_exp
                q_sc[r * tq:(r + 1) * tq, :LANE] = (qn_ref[:, r * LANE:(r + 1) * LANE].astype(F32) * c_exp).astype(BF)
                q_sc[r * tq:(r + 1) * tq, LANE:] = jnp.where(half_of_lane == r % 2, rope_pair, 0.0).astype(BF)
        elif variant == "swa":
            low = lax.broadcasted_iota(jnp.int32, (tq, LANE), 1) < dv
            for r in range(n_rep):
                chunk = q_ref[:, (r // 2) * LANE:(r // 2 + 1) * LANE].astype(F32)
                if r % 2 == 1:
                    chunk = pltpu.roll(chunk, dv, 1)
                q_sc[r * tq:(r + 1) * tq, :] = jnp.where(low, chunk * c_exp, 0.0).astype(BF)
        else:
            dq = q_sc.shape[1]
            for r in range(n_rep):
                q_sc[r * tq:(r + 1) * tq, :] = (q_ref[:, r * dq:(r + 1) * dq].astype(F32) * c_exp).astype(BF)

    ones = jnp.ones((tk, dv), BF)
    if variant == "mla":
        ks = [jnp.concatenate([kn_ref[:, r * LANE:(r + 1) * LANE], kp_ref[...]], axis=1) for r in range(n_rep)]
        vs = [jnp.concatenate([v_ref[:, r * dv:(r + 1) * dv], ones], axis=1) for r in range(n_rep)]
    elif variant == "swa":
        swapped = pltpu.roll(kv_ref[...].astype(F32), dv, 1)
        ks = [kv_ref[...]]
        vs = [jnp.where(lax.broadcasted_iota(jnp.int32, (tk, LANE), 1) < dv, swapped, 1.0).astype(BF)]
    else:
        ks = [k_ref[...]]
        vs = [jnp.concatenate([v_ref[...], ones], axis=1)]

    def kv_of(u):
        return (u * rc) // tq if len(ks) > 1 else 0

    def tile_bias():
        row = lax.broadcasted_iota(jnp.int32, (tq, tk), 0)
        col = lax.broadcasted_iota(jnp.int32, (tq, tk), 1)
        dist = qi * tq - kj * tk + row - col
        mask = dist >= 0
        if window < seq:
            mask = mask & (dist < window)
        if has_sel:
            n_blk = sel_ref.shape[-1]
            blk_of_key = (kj * tk + lax.broadcasted_iota(jnp.int32, (n_blk, tk), 1)) // SLC_BLOCK
            expand = jnp.where(lax.broadcasted_iota(jnp.int32, (n_blk, tk), 0) == blk_of_key, 1.0, 0.0).astype(BF)
            picked = jnp.dot(sel_ref[0].astype(BF), expand, preferred_element_type=F32)
            mask = mask & (picked > 0.5)
        return jnp.where(mask, 0.0, NEG)

    def logits(u):
        return lax.dot_general(q_sc[u * rc:(u + 1) * rc, :], ks[kv_of(u)], (((1,), (1,)), ((), ())),
                               preferred_element_type=F32)

    def online_softmax(masked):
        if masked and has_sel:
            bias_sc[...] = tile_bias()
        if masked and not has_sel:
            rel = (lax.broadcasted_iota(jnp.int32, (rc, tk), 0) - lax.broadcasted_iota(jnp.int32, (rc, tk), 1))
        n_units = rows // rc
        s_next = logits(0)
        for u in range(n_units):
            s = s_next
            if u + 1 < n_units:
                s_next = logits(u + 1)
            us = slice(u * rc, (u + 1) * rc)
            m_prev = m_sc[us, :]
            if masked and has_sel:
                s = s + bias_sc[(u * rc) % tq:(u * rc) % tq + rc, :]
            elif masked:
                base = qi * tq - kj * tk + (u * rc) % tq
                visible = rel >= -base
                if window < seq:
                    visible = visible & (rel < window - base)
                s = jnp.where(visible, s, NEG)
            m_new = jnp.maximum(m_prev, jnp.max(s, -1, keepdims=True))
            w = jnp.exp2(s - _lane_tile(m_new, tk // LANE))
            m_sc[us, :] = m_new
            alpha = jnp.exp2(m_prev - m_new)
            acc_sc[us, :] = (_lane_tile(alpha, acc_sc.shape[1] // LANE) * acc_sc[us, :]
                             + jnp.dot(w.astype(BF), vs[kv_of(u)], preferred_element_type=F32))

    if not two_paths:
        online_softmax(True)
    else:
        pl.when(msk_ref[p] == 1)(lambda: online_softmax(True))
        pl.when(msk_ref[p] == 0)(lambda: online_softmax(False))

    @pl.when(lst_ref[p] == 1)
    def _():
        acc = acc_sc[...]
        if dv % LANE == 0:
            o = acc[:, :dv] / acc[:, dv:]
        else:
            o = acc[:, :dv] / acc[:, dv:dv + 1]
        if gate_col is not None:
            gates = jax.nn.sigmoid(gl_ref[...] + gb_ref[...])
        for r in range(n_rep):
            o_r = o[r * tq:(r + 1) * tq]
            if gate_col is not None:
                o_r = o_r * gates[:, 3 * r + gate_col:3 * r + gate_col + 1]
            if has_add:
                o_r = o_r + add_ref[:, r * dv:(r + 1) * dv].astype(F32)
            o_ref[:, r * dv:(r + 1) * dv] = o_r.astype(o_ref.dtype)


def _attn_call(variant, qkv_args, qkv_specs, *, batch, seq, n_grp, n_rep, dq, dv, scale, window, tq, tk,
               sel=None, sinks=None, gate=None, add=None, name="attn"):
    rows = n_rep * tq
    nq, nk = seq // tq, seq // tk
    pairs, fst, lst, msk = [], [], [], []
    for i in range(nq):
        lo = max(0, i * tq - (window - 1)) // tk
        hi = (i * tq + tq - 1) // tk
        for j in range(lo, hi + 1):
            pairs.append((i, j))
            fst.append(int(j == lo))
            lst.append(int(j == hi))
            inside = i * tq - (j * tk + tk - 1) >= 0 and i * tq + tq - 1 - j * tk < window
            msk.append(int(sel is not None or not inside))
    two_paths = 0 < sum(msk) < len(msk)
    qi = jnp.asarray([a for a, _ in pairs], jnp.int32)
    kj = jnp.asarray([b for _, b in pairs], jnp.int32)

    def q_rows(b, p, qi, kj):
        return (b // n_grp) * nq + qi[p]

    def k_rows(b, p, qi, kj):
        return (b // n_grp) * nk + kj[p]

    in_specs = [mk(q_rows, k_rows) for mk in qkv_specs]
    args = list(qkv_args)
    if sel is not None:
        in_specs.append(pl.BlockSpec((1, tq, sel.shape[-1]), lambda b, p, qi, kj, *_: (b, qi[p], 0)))
        args.append(sel)
    if sinks is not None:
        m0 = jnp.broadcast_to((sinks.astype(F32) * LOG2E)[:, :, None, None], (n_grp, n_rep, tq, LANE))
        in_specs.append(pl.BlockSpec((1, rows, LANE), lambda b, p, qi, kj, *_: (b % n_grp, 0, 0)))
        args.append(m0.reshape(n_grp, rows, LANE))
    gate_col = None
    if gate is not None:
        gl, gb, gate_col = gate
        in_specs += [pl.BlockSpec((tq, LANE), lambda b, p, qi, kj, *_: (q_rows(b, p, qi, kj), b % n_grp)),
                     pl.BlockSpec((1, LANE), lambda b, p, qi, kj, *_: (0, b % n_grp))]
        args += [gl, gb]
    out_spec = pl.BlockSpec((tq, n_rep * dv), lambda b, p, qi, kj, *_: (q_rows(b, p, qi, kj), b % n_grp))
    if add is not None:
        in_specs.append(out_spec)
        args.append(add)
    rc = min(tq, UNIT_ROWS)
    kern = functools.partial(_attn_kernel, variant=variant, n_grp=n_grp, n_rep=n_rep, tq=tq, tk=tk, rc=rc, dv=dv,
                             c_exp=scale * LOG2E, window=window, seq=seq, has_sel=sel is not None,
                             has_sink=sinks is not None, gate_col=gate_col, has_add=add is not None,
                             two_paths=two_paths)
    return pl.pallas_call(
        kern,
        grid_spec=pltpu.PrefetchScalarGridSpec(
            num_scalar_prefetch=5,
            grid=(batch * n_grp, len(pairs)),
            in_specs=in_specs,
            out_specs=out_spec,
            scratch_shapes=[pltpu.VMEM((rows, dq), BF), pltpu.VMEM((rows, LANE), F32),
                            pltpu.VMEM((rows, 2 * dv), F32), pltpu.VMEM((tq, tk), F32)]),
        out_shape=jax.ShapeDtypeStruct((batch * seq, n_grp * n_rep * dv), BF),
        compiler_params=_params("parallel", "arbitrary"),
        name=name,
    )(qi, kj, jnp.asarray(fst, jnp.int32), jnp.asarray(lst, jnp.int32), jnp.asarray(msk, jnp.int32), *args)


def _spec(shape, rows_of, col_fn):
    def make(q_rows, k_rows):
        rf = q_rows if rows_of == "q" else k_rows
        return pl.BlockSpec(shape, lambda b, p, qi, kj, *_: (rf(b, p, qi, kj), col_fn(b)))
    return make


def _band_kernel(*refs, n_rep, t, n_kt, rc, dv, c_exp, window, gate_col, has_add):
    it = iter(refs)
    q_ref = next(it)
    k_refs = [next(it) for _ in range(n_kt)]
    v_refs = [next(it) for _ in range(n_kt)]
    gl_ref, gb_ref = (next(it), next(it)) if gate_col is not None else (None, None)
    add_ref = next(it) if has_add else None
    o_ref = next(it)
    i = pl.program_id(1)
    n_keys = n_kt * t
    k = jnp.concatenate([r[...] for r in k_refs], axis=0)
    d = k.shape[1]
    v_ext = jnp.concatenate([jnp.concatenate([r[...] for r in v_refs], axis=0), jnp.ones((n_keys, dv), BF)], axis=1)
    col = lax.broadcasted_iota(jnp.int32, (rc, n_keys), 1)
    rel = lax.broadcasted_iota(jnp.int32, (rc, n_keys), 0) - col
    in_seq = col >= (n_kt - 1 - i) * t
    if gate_col is not None:
        gates = jax.nn.sigmoid(gl_ref[...] + gb_ref[...])

    def q_block(u):
        r, i0 = (u * rc) // t, (u * rc) % t
        return (q_ref[i0:i0 + rc, r * d:(r + 1) * d].astype(F32) * c_exp).astype(BF)

    def logits(u):
        return lax.dot_general(q_block(u), k, (((1,), (1,)), ((), ())), preferred_element_type=F32)

    n_units = n_rep * t // rc
    s_next = logits(0)
    for u in range(n_units):
        s = s_next
        if u + 1 < n_units:
            s_next = logits(u + 1)
        r, i0 = (u * rc) // t, (u * rc) % t
        base = (n_kt - 1) * t + i0
        visible = (rel >= -base) & (rel < window - base) & in_seq
        s = jnp.where(visible, s, NEG)
        w = jnp.exp2(s - jnp.max(s, -1, keepdims=True))
        ext = jnp.dot(w.astype(BF), v_ext, preferred_element_type=F32)
        o_u = ext[:, :dv] / ext[:, dv:]
        if gate_col is not None:
            o_u = o_u * gates[i0:i0 + rc, 3 * r + gate_col:3 * r + gate_col + 1]
        if has_add:
            o_u = o_u + add_ref[i0:i0 + rc, r * dv:(r + 1) * dv].astype(F32)
        o_ref[i0:i0 + rc, r * dv:(r + 1) * dv] = o_u.astype(o_ref.dtype)


def _band_call(z, q_col, k_col, v_col, *, batch, seq, n_grp, n_rep, d, scale, window, t, gate=None, add=None,
               name="band_attn"):
    dv = d
    n_kt = -(-(window - 1) // t) + 1
    nq = seq // t
    rc = min(t, UNIT_ROWS)

    def q_row(b, i):
        return (b // n_grp) * nq + i

    in_specs = [pl.BlockSpec((t, n_rep * d), lambda b, i: (q_row(b, i), q_col // (n_rep * d) + b % n_grp))]
    args = [z]
    for col in (k_col, v_col):
        for j in range(n_kt):
            in_specs.append(pl.BlockSpec(
                (t, d), lambda b, i, j=j, col=col: ((b // n_grp) * nq + jnp.maximum(i - (n_kt - 1) + j, 0),
                                                     col // d + b % n_grp)))
            args.append(z)
    gate_col = None
    if gate is not None:
        gl, gb, gate_col = gate
        in_specs += [pl.BlockSpec((t, LANE), lambda b, i: (q_row(b, i), b % n_grp)),
                     pl.BlockSpec((1, LANE), lambda b, i: (0, b % n_grp))]
        args += [gl, gb]
    out_spec = pl.BlockSpec((t, n_rep * dv), lambda b, i: (q_row(b, i), b % n_grp))
    if add is not None:
        in_specs.append(out_spec)
        args.append(add)
    kern = functools.partial(_band_kernel, n_rep=n_rep, t=t, n_kt=n_kt, rc=rc, dv=dv, c_exp=scale * LOG2E,
                             window=window, gate_col=gate_col, has_add=add is not None)
    return pl.pallas_call(
        kern,
        grid=(batch * n_grp, nq),
        in_specs=in_specs,
        out_specs=out_spec,
        out_shape=jax.ShapeDtypeStruct((batch * seq, n_grp * n_rep * dv), BF),
        compiler_params=_params("parallel", "parallel"),
        name=name,
    )(*args)


def _compress_kernel(*refs):
    x_refs = refs[:CMP_STRIDE]
    pa_ref, pb_ref, w1a_ref, w1b_ref, w2_ref, o_ref = refs[CMP_STRIDE:]
    x = jnp.concatenate([r[...] for r in x_refs], axis=1).astype(F32)
    a = jnp.dot((x + pa_ref[...]).astype(BF), w1a_ref[...], preferred_element_type=F32)
    b = jnp.dot((x + pb_ref[...]).astype(BF), w1b_ref[...], preferred_element_type=F32)
    h = jax.nn.gelu(a + pltpu.roll(b, b.shape[0] - 1, 0))
    o_ref[0] = jnp.dot(h.astype(BF), w2_ref[...], preferred_element_type=F32).astype(o_ref.dtype)


def _compress(z, col, pos, w1, w2, *, batch, seq, n_grp, name):
    d = w2.shape[1]
    hid = w1.shape[1]
    nch = seq // CMP_STRIDE
    width = CMP_STRIDE * d
    blk_per_tok = z.shape[1] // d
    zc = z.reshape(batch * nch, CMP_STRIDE * z.shape[1])
    pa = pos[:CMP_STRIDE].reshape(1, width).astype(F32)
    pb = pos[CMP_STRIDE:].reshape(1, width).astype(F32)
    w1 = w1.astype(BF)

    def tok_spec(l):
        return pl.BlockSpec((nch, d), lambda b: (b // n_grp, l * blk_per_tok + col // d + b % n_grp))

    return pl.pallas_call(
        _compress_kernel,
        grid=(batch * n_grp,),
        in_specs=[tok_spec(l) for l in range(CMP_STRIDE)] + [
                  pl.BlockSpec((1, width), lambda b: (0, 0)),
                  pl.BlockSpec((1, width), lambda b: (0, 0)),
                  pl.BlockSpec((width, hid), lambda b: (0, 0)),
                  pl.BlockSpec((width, hid), lambda b: (1, 0)),
                  pl.BlockSpec((hid, d), lambda b: (0, 0))],
        out_specs=pl.BlockSpec((1, nch, d), lambda b: (b, 0, 0)),
        out_shape=jax.ShapeDtypeStruct((batch * n_grp, nch, d), BF),
        compiler_params=_params("parallel"),
        name=name,
    )(*([zc] * CMP_STRIDE), pa, pb, w1, w1, w2.astype(BF))


def _cmp_attn_kernel(q_ref, k_ref, v_ref, ovt_ref, gl_ref, gb_ref, o_ref, sel_ref, *, n_rep, tq, c_exp):
    qi = pl.program_id(1)
    nc, d = k_ref.shape[1], k_ref.shape[2]
    rows = n_rep * tq
    q = jnp.concatenate([q_ref[:, r * d:(r + 1) * d] for r in range(n_rep)], axis=0)
    s = lax.dot_general(q, k_ref[0], (((1,), (1,)), ((), ())), preferred_element_type=F32)
    qpos = qi * tq + lax.broadcasted_iota(jnp.int32, (tq, nc), 0)
    cmp_end = lax.broadcasted_iota(jnp.int32, (tq, nc), 1) * CMP_STRIDE + (CMP_BLOCK - 1)
    cmask = (cmp_end <= qpos)[None]
    s = jnp.where(cmask, s.reshape(n_rep, tq, nc), NEG)
    e = jnp.exp2((s - jnp.max(s, -1, keepdims=True)) * c_exp)
    e = jnp.where(cmask, e, 0.0).reshape(rows, nc)
    ext = jnp.dot(e.astype(BF), v_ref[0], preferred_element_type=F32)
    den = ext[:, d:]
    inv = 1.0 / jnp.where(den > 0.0, den, 1.0)
    gates = jax.nn.sigmoid(gl_ref[...] + gb_ref[...])
    o = ext[:, :d] * inv
    for r in range(n_rep):
        o_ref[:, r * d:(r + 1) * d] = (o[r * tq:(r + 1) * tq] * gates[:, 3 * r:3 * r + 1]).astype(o_ref.dtype)
    inv_c = inv[:, :nc] if nc <= d else _lane_tile(inv, nc // d)
    prob = e * inv_c
    psum = jnp.sum(prob.reshape(n_rep, tq, nc), axis=0)
    p_hi = psum.astype(BF)
    p_lo = (psum - p_hi.astype(F32)).astype(BF)
    contract_last = (((1,), (1,)), ((), ()))
    imp = (lax.dot_general(ovt_ref[...], p_hi, contract_last, preferred_element_type=F32)
           + lax.dot_general(ovt_ref[...], p_lo, contract_last, preferred_element_type=F32))
    ns = imp.shape[0]
    blk = lax.broadcasted_iota(jnp.int32, (ns, tq), 0)
    cur = (qi * tq + lax.broadcasted_iota(jnp.int32, (ns, tq), 1)) // SLC_BLOCK
    eligible = blk <= cur
    forced = (blk == 0) | (blk == cur) | (blk == cur - 1)
    score = jnp.where(eligible, jnp.where(forced, FORCE_SCORE, imp), -1.0)
    beaten_by = jnp.zeros((ns, tq), F32)
    for kk in range(ns):
        sk = score[kk:kk + 1, :]
        beats = (sk > score) | ((sk == score) & (blk > kk))
        beaten_by = beaten_by + jnp.where(beats, 1.0, 0.0)
    n_sel = min(N_SEL, ns)
    sel_t = jnp.where((beaten_by < n_sel) & (score >= 0.0), 1.0, 0.0)
    sel_sq = jnp.concatenate([sel_t, jnp.zeros((tq - ns, tq), F32)], axis=0) if ns < tq else sel_t
    sel_ref[0] = sel_sq.T[:, :ns]


def _cmp_attention(z, kcmp, vcmp, gl, gb, *, batch, seq, n_grp, n_rep, scale, tq):
    bg, nc, d = kcmp.shape
    ns = seq // SLC_BLOCK
    nq = seq // tq
    starts = np.arange(nc) * CMP_STRIDE
    jb = np.arange(ns)
    overlap = ((starts[:, None] < (jb[None, :] + 1) * SLC_BLOCK)
               & (starts[:, None] + CMP_BLOCK > jb[None, :] * SLC_BLOCK)).astype(np.float32)
    v_ext = jnp.concatenate([vcmp, jnp.ones_like(vcmp)], -1)
    tok_spec = pl.BlockSpec((tq, n_rep * d), lambda b, i: ((b // n_grp) * nq + i, b % n_grp))
    return pl.pallas_call(
        functools.partial(_cmp_attn_kernel, n_rep=n_rep, tq=tq, c_exp=scale * LOG2E),
        grid=(bg, nq),
        in_specs=[tok_spec,
                  pl.BlockSpec((1, nc, d), lambda b, i: (b, 0, 0)),
                  pl.BlockSpec((1, nc, 2 * d), lambda b, i: (b, 0, 0)),
                  pl.BlockSpec((ns, nc), lambda b, i: (0, 0)),
                  pl.BlockSpec((tq, LANE), lambda b, i: ((b // n_grp) * nq + i, b % n_grp)),
                  pl.BlockSpec((1, LANE), lambda b, i: (0, b % n_grp))],
        out_specs=[tok_spec, pl.BlockSpec((1, tq, ns), lambda b, i: (b, i, 0))],
        out_shape=[jax.ShapeDtypeStruct((batch * seq, n_grp * n_rep * d), BF),
                   jax.ShapeDtypeStruct((bg, seq, ns), F32)],
        compiler_params=_params("parallel", "parallel"),
        name="nsa_cmp_attn",
    )(z, kcmp, v_ext, jnp.asarray(overlap.T, BF), gl, gb)


def _rope_tables(seq, dim):
    inv = 1.0 / (ROPE_THETA ** (jnp.arange(0, dim, 2, dtype=F32) / dim))
    ang = jnp.arange(seq, dtype=F32)[:, None] * inv[None, :]
    cos_h = jnp.concatenate([jnp.cos(ang), jnp.cos(ang)], -1)
    sin_h = jnp.concatenate([-jnp.sin(ang), jnp.sin(ang)], -1)
    return jnp.tile(cos_h, (1, LANE // dim)), jnp.tile(sin_h, (1, LANE // dim))


def _pad_cols(w, n):
    return jnp.pad(w, ((0, 0), (0, n - w.shape[1])))


def _even_mixer(x16, batch, seq, w_in, q_norm, w_uq, kv_norm, w_ukv, sinks):
    H, dn, dr, dv = MLA_HEADS, MLA_NOPE, MLA_ROPE, MLA_V
    G, R, ds = SWA_KV_HEADS, SWA_HEADS // SWA_KV_HEADS, SWA_HEAD_DIM
    n_lat = MLA_Q_RANK + MLA_KV_RANK
    o_qs = n_lat + dr
    o_ks = o_qs + SWA_HEADS * ds
    o_vs = o_ks + G * ds
    w_kpe = w_in[:, n_lat:o_qs]
    cols = [w_in[:, o_qs:o_ks]]
    for g in range(G):
        cols += [w_in[:, o_ks + g * ds:o_ks + (g + 1) * ds], w_in[:, o_vs + g * ds:o_vs + (g + 1) * ds]]
    cols += [w_kpe, w_kpe]
    w_rg = jnp.concatenate(cols, -1)
    n_q = SWA_HEADS * ds
    cos, sin = _rope_tables(seq, ds)
    groups = "R" * (n_q // LANE) + "L" * G + "R"
    z, z_lat = _proj(x16, jnp.concatenate([w_rg, w_in[:, :n_lat]], -1).astype(BF),
                     rope=(cos, sin, ds // 2, groups), seq=seq, out_dtype=BF, n_tail=n_lat, name="even_in")

    w_uq_p = jnp.concatenate([w_uq.reshape(-1, H, dn + dr)[:, :, :dn].reshape(-1, H * dn),
                              w_uq.reshape(-1, H, dn + dr)[:, :, dn:].reshape(-1, H * dr)], -1)
    w_ukv_p = jnp.concatenate([w_ukv.reshape(-1, H, dn + dv)[:, :, :dn].reshape(-1, H * dn),
                               w_ukv.reshape(-1, H, dn + dv)[:, :, dn:].reshape(-1, H * dv)], -1)
    cos_q, sin_q = _rope_tables(seq, dr)
    q = _proj(z_lat, w_uq_p.astype(BF), k_block=0, k_size=MLA_Q_RANK, rms_g=q_norm,
              rope=(cos_q, sin_q, dr // 2, "N" * (H * dn // LANE) + "R" * (H * dr // LANE)), seq=seq, out_dtype=BF,
              name="mla_q_up")
    kv = _proj(z_lat, w_ukv_p.astype(BF), k_block=1, k_size=MLA_KV_RANK, rms_g=kv_norm, out_dtype=BF,
               name="mla_kv_up")
    tq = _pick(seq, (512, 256, 128))
    tk = _pick(seq, (512, 256, 128))
    kpe_blk = (n_q + 2 * G * ds) // LANE
    hp = MLA_HEADS_PER_STEP
    n_hg = H // hp
    o_mla = _attn_call(
        "mla", [q, q, kv, z, kv],
        [_spec((tq, hp * dn), "q", lambda b: b % n_hg),
         _spec((tq, hp * dr), "q", lambda b: H * dn // (hp * dr) + b % n_hg),
         _spec((tk, hp * dn), "k", lambda b: b % n_hg),
         _spec((tk, LANE), "k", lambda b: kpe_blk),
         _spec((tk, hp * dv), "k", lambda b: H * dn // (hp * dv) + b % n_hg)],
        batch=batch, seq=seq, n_grp=n_hg, n_rep=hp, dq=2 * LANE, dv=dv, scale=(dn + dr) ** -0.5, window=seq,
        tq=tq, tk=tk, name="mla_attn")

    tq = _pick(seq, (256, 128))
    o_swa = _attn_call(
        "swa", [z, z],
        [_spec((tq, R * ds), "q", lambda b: b % G),
         _spec((tq, LANE), "k", lambda b: n_q // LANE + b % G)],
        batch=batch, seq=seq, n_grp=G, n_rep=R, dq=LANE, dv=ds, scale=ds ** -0.5, window=SWA_WINDOW,
        tq=tq, tk=tq, sinks=sinks.reshape(G, R), name="swa_attn")
    return [o_mla, o_swa]


def _odd_mixer(x16, batch, seq, w_in, gate_b, cmp_pos_k, cmp_pos_v, k_w1, k_w2, v_w1, v_w2):
    G, d = NSA_KV_GROUPS, NSA_HEAD_DIM
    H = NSA_HEADS
    R = H // G
    kvw = G * d
    o_kc = H * d
    offs = [o_kc + i * kvw for i in range(7)]
    cos, sin = _rope_tables(seq, d)
    groups = "R" * H + ("R" * G + "N" * G) * 3
    w_gl = jnp.concatenate([_pad_cols(w_in[:, offs[6] + g * 3 * R:offs[6] + (g + 1) * 3 * R], LANE)
                            for g in range(G)], -1)
    z, gl = _proj(x16, jnp.concatenate([w_in[:, :offs[6]], w_gl], -1).astype(BF),
                  rope=(cos, sin, d // 2, groups), seq=seq, out_dtype=BF, n_tail=G * LANE, name="odd_in")
    gb = jnp.concatenate([jnp.pad(gate_b[g * 3 * R:(g + 1) * 3 * R].astype(F32), (0, LANE - 3 * R))
                          for g in range(G)]).reshape(1, G * LANE)

    scale = d ** -0.5
    kcmp = _compress(z, offs[0], cmp_pos_k, k_w1, k_w2, batch=batch, seq=seq, n_grp=G, name="nsa_compress_k")
    vcmp = _compress(z, offs[1], cmp_pos_v, v_w1, v_w2, batch=batch, seq=seq, n_grp=G, name="nsa_compress_v")
    tq = _pick(seq, (256, 128))
    o, sel = _cmp_attention(z, kcmp, vcmp, gl, gb, batch=batch, seq=seq, n_grp=G, n_rep=R, scale=scale, tq=tq)

    def branch(k_off, v_off, **kw):
        return _attn_call(
            "gqa", [z, z, z],
            [_spec((kw["tq"], R * d), "q", lambda b: b % G),
             _spec((kw["tk"], d), "k", lambda b: k_off // d + b % G),
             _spec((kw["tk"], d), "k", lambda b: v_off // d + b % G)],
            batch=batch, seq=seq, n_grp=G, n_rep=R, dq=d, dv=d, scale=scale, **kw)

    o = branch(offs[2], offs[3], window=seq, tq=_pick(seq, (512, 256, 128)), tk=_pick(seq, (512, 256, 128)),
               sel=sel, gate=(gl, gb, 1), add=o, name="nsa_slc_attn")
    o = _band_call(z, 0, offs[4], offs[5], batch=batch, seq=seq, n_grp=G, n_rep=R, d=d, scale=scale,
                   window=NSA_WINDOW, t=tq, gate=(gl, gb, 2), add=o, name="nsa_win_attn")
    return [o]


def kernel(x, ln_g, ln_b, ffn_w_gate, ffn_w_up, ffn_w_down, even_w_in, mla_q_norm, mla_w_uq, mla_kv_norm, mla_w_ukv, swa_sinks, even_w_out, odd_w_in, nsa_gate_b, nsa_cmp_pos_k, nsa_cmp_pos_v, nsa_cmp_k_w1, nsa_cmp_k_w2, nsa_cmp_v_w1, nsa_cmp_v_w2, odd_w_out):
    batch, seq, dm = x.shape
    depth = ln_g.shape[0]
    alpha = float((2 * depth) ** 0.25)
    x32 = x.reshape(batch * seq, dm)
    x16 = x32.astype(BF)
    even_out16 = even_w_out.astype(BF)
    odd_out16 = odd_w_out.astype(BF)

    def ffn(x32, x16, l, idx):
        h, w_down16 = _ffn_up(x16, ffn_w_gate, ffn_w_up, ffn_w_down, l, idx)
        return _mm_res_ln([h], w_down16, (), x32, ln_g[l, 2 * idx], ln_b[l, 2 * idx], alpha=alpha, coef=0.5,
                          name="ffn_down_ln")

    for l in range(depth):
        x32, x16 = ffn(x32, x16, l, 0)
        j = l // 2
        if l % 2 == 0:
            o = _even_mixer(x16, batch, seq, even_w_in[j], mla_q_norm[j], mla_w_uq[j], mla_kv_norm[j],
                            mla_w_ukv[j], swa_sinks[j])
            w_out = even_out16
        else:
            o = _odd_mixer(x16, batch, seq, odd_w_in[j], nsa_gate_b[j], nsa_cmp_pos_k[j], nsa_cmp_pos_v[j],
                           nsa_cmp_k_w1[j], nsa_cmp_k_w2[j], nsa_cmp_v_w1[j], nsa_cmp_v_w2[j])
            w_out = odd_out16
        x32, x16 = _mm_res_ln(o, w_out, (j,), x32, ln_g[l, 1], ln_b[l, 1], alpha=alpha, coef=1.0,
                              name="mixer_out_ln")
        x32, x16 = ffn(x32, x16, l, 1)
    return x32.reshape(batch, seq, dm)
```

```python
import functools
import math

import numpy as np
import jax
import jax.numpy as jnp
from jax import lax
from jax.experimental import pallas as pl
from jax.experimental.pallas import tpu as pltpu

F32 = jnp.float32
BF = jnp.bfloat16

ROPE_THETA = 10000.0
LN_EPS = 1e-5
RMS_EPS = 1e-6
NEG = -1e30
FORCE_SCORE = 1e9
LOG2E = math.log2(math.e)

MLA_HEADS, MLA_NOPE, MLA_ROPE, MLA_V = 8, 128, 64, 128
MLA_Q_RANK, MLA_KV_RANK = 512, 512
SWA_HEADS, SWA_KV_HEADS, SWA_HEAD_DIM, SWA_WINDOW = 16, 2, 64, 128
NSA_HEADS, NSA_KV_GROUPS, NSA_HEAD_DIM = 16, 2, 128
CMP_BLOCK, CMP_STRIDE, CMP_HIDDEN = 32, 16, 256
SLC_BLOCK, N_SEL, NSA_WINDOW = 64, 8, 512

LANE = 128
UNIT_ROWS = 128
MLA_HEADS_PER_STEP = 4
VMEM_LIMIT_BYTES = 52 * 1024 * 1024


def _params(*sem):
    return pltpu.CompilerParams(dimension_semantics=sem, vmem_limit_bytes=VMEM_LIMIT_BYTES)


def _pick(n, candidates):
    for c in candidates:
        if n % c == 0:
            return c
    raise ValueError(f"no tile in {candidates} divides {n}")


def _lane_tile(x, n):
    return x if n == 1 else jnp.concatenate([x] * n, axis=1)


def _proj_kernel(*refs, has_rms, groups, half, n_tail, chunked):
    it = iter(refs)
    x_ref, w_ref = next(it), next(it)
    g_ref = next(it) if has_rms else None
    cos_ref, sin_ref = (next(it), next(it)) if groups else (None, None)
    o_ref = next(it)
    tail_ref = next(it) if n_tail else None
    chunk_ref, stage = (next(it), next(it)) if chunked else (None, None)
    x = x_ref[...]
    if has_rms:
        xf = x.astype(F32)
        x = xf * lax.rsqrt(jnp.mean(xf * xf, -1, keepdims=True) + RMS_EPS) * g_ref[...]
    acc = jnp.dot(x.astype(BF), w_ref[...], preferred_element_type=F32)
    if n_tail:
        tail_ref[...] = acc[:, acc.shape[1] - n_tail:]
        acc = acc[:, :acc.shape[1] - n_tail]
    if not groups:
        o_ref[...] = acc.astype(o_ref.dtype)
        return
    cos, sin = cos_ref[...], sin_ref[...]
    lane = lax.broadcasted_iota(jnp.int32, cos.shape, 1)
    low = lane < LANE // 2
    cos_l, sin_l = jnp.where(low, cos, 1.0), jnp.where(low, sin, 0.0)
    for c, kind in enumerate(groups):
        a = acc[:, c * LANE:(c + 1) * LANE]
        if kind != "N":
            if 2 * half == LANE:
                partner = pltpu.roll(a, half, 1)
            else:
                partner = jnp.where(lane % (2 * half) < half, pltpu.roll(a, LANE - half, 1), pltpu.roll(a, half, 1))
            a = a * cos + partner * sin if kind == "R" else a * cos_l + partner * sin_l
        o_ref[:, c * LANE:(c + 1) * LANE] = a.astype(o_ref.dtype)
        if c in chunked:
            stage[...] = a
            n_ch = a.shape[0] // CMP_STRIDE
            for l in range(CMP_STRIDE):
                lanes = slice((chunked.index(c) * CMP_STRIDE + l) * LANE, (chunked.index(c) * CMP_STRIDE + l + 1) * LANE)
                chunk_ref[:, lanes] = stage[pl.ds(l, n_ch, stride=CMP_STRIDE), :].astype(chunk_ref.dtype)


def _proj(x, w, *, k_block=0, k_size=None, rms_g=None, rope=None, seq=None, out_dtype=F32, n_tail=0, chunked=(),
          name="proj"):
    m = x.shape[0]
    k_size = x.shape[1] if k_size is None else k_size
    n = w.shape[1]
    n_main = n - n_tail
    out_bytes = jnp.dtype(out_dtype).itemsize
    for tm in (512, 256, 128):
        est = k_size * n * 2 + 2 * tm * k_size * x.dtype.itemsize + 2 * tm * n * out_bytes + 3 * tm * n * 4
        if m % tm == 0 and est <= VMEM_LIMIT_BYTES * 3 // 4:
            break
    in_specs = [pl.BlockSpec((tm, k_size), lambda i: (i, k_block)),
                pl.BlockSpec((k_size, n), lambda i: (0, 0), pipeline_mode=pl.Buffered(1))]
    args = [x, w]
    if rms_g is not None:
        in_specs.append(pl.BlockSpec((1, k_size), lambda i: (0, 0)))
        args.append(rms_g.reshape(1, k_size).astype(F32))
    half, groups = 0, ()
    if rope is not None:
        cos, sin, half, groups = rope
        sb = seq // tm
        in_specs += [pl.BlockSpec((tm, LANE), lambda i: (i % sb, 0))] * 2
        args += [cos, sin]
    out_specs = [pl.BlockSpec((tm, n_main), lambda i: (i, 0))]
    out_shape = [jax.ShapeDtypeStruct((m, n_main), out_dtype)]
    if n_tail:
        out_specs.append(pl.BlockSpec((tm, n_tail), lambda i: (i, 0)))
        out_shape.append(jax.ShapeDtypeStruct((m, n_tail), F32))
    scratch = []
    if chunked:
        width = len(chunked) * CMP_STRIDE * LANE
        out_specs.append(pl.BlockSpec((tm // CMP_STRIDE, width), lambda i: (i, 0)))
        out_shape.append(jax.ShapeDtypeStruct((m // CMP_STRIDE, width), out_dtype))
        scratch.append(pltpu.VMEM((tm, LANE), F32))
    out = pl.pallas_call(
        functools.partial(_proj_kernel, has_rms=rms_g is not None, groups=tuple(groups), half=half, n_tail=n_tail,
                          chunked=tuple(chunked)),
        grid=(m // tm,),
        in_specs=in_specs,
        out_specs=out_specs,
        out_shape=out_shape,
        scratch_shapes=scratch,
        compiler_params=_params("parallel"),
        name=name,
    )(*args)
    return out if len(out) > 1 else out[0]


def _ffn_up_kernel(x_ref, wg_ref, wu_ref, wd_ref, o_ref, wd16_ref, wg16, wu16):
    @pl.when(pl.program_id(1) == 0)
    def _():
        wg16[...] = wg_ref[...].astype(BF)
        wu16[...] = wu_ref[...].astype(BF)

    wd16_ref[...] = wd_ref[...].astype(BF)
    x = x_ref[...]
    g = jnp.dot(x, wg16[...], preferred_element_type=F32)
    u = jnp.dot(x, wu16[...], preferred_element_type=F32)
    o_ref[...] = (g * jax.nn.sigmoid(g) * u).astype(o_ref.dtype)


def _ffn_up(x16, wg_all, wu_all, wd_all, layer, idx):
    m, k = x16.shape
    n = wg_all.shape[-1]
    n_out = wd_all.shape[-1]
    tm = _pick(m, (1024, 512, 256, 128))
    tn = _pick(n, (512, 256, 128))
    ni = m // tm
    slab = n // ((n // tn) * ni)
    assert slab * (n // tn) * ni == n and slab % 16 == 0
    w_spec = pl.BlockSpec((None, None, k, tn), lambda j, i: (layer, idx, 0, j))
    return pl.pallas_call(
        _ffn_up_kernel,
        grid=(n // tn, ni),
        in_specs=[pl.BlockSpec((tm, k), lambda j, i: (i, 0)), w_spec, w_spec,
                  pl.BlockSpec((None, None, slab, n_out), lambda j, i: (layer, idx, j * ni + i, 0))],
        out_specs=[pl.BlockSpec((tm, tn), lambda j, i: (i, j)),
                   pl.BlockSpec((slab, n_out), lambda j, i: (j * ni + i, 0))],
        out_shape=[jax.ShapeDtypeStruct((m, n), BF), jax.ShapeDtypeStruct((n, n_out), BF)],
        scratch_shapes=[pltpu.VMEM((k, tn), BF), pltpu.VMEM((k, tn), BF)],
        compiler_params=_params("parallel", "arbitrary"),
        name="ffn_up",
    )(x16, wg_all, wu_all, wd_all)


def _mm_res_ln_kernel(*refs, n_x, alpha, coef):
    x_refs = refs[:n_x]
    w_ref, res_ref, g_ref, b_ref, o32_ref, o16_ref = refs[n_x:]
    x = x_refs[0][...] if n_x == 1 else jnp.concatenate([r[...] for r in x_refs], axis=1)
    y = alpha * res_ref[...] + coef * jnp.dot(x, w_ref[...], preferred_element_type=F32)
    mu = jnp.mean(y, -1, keepdims=True)
    yc = y - mu
    var = jnp.mean(yc * yc, -1, keepdims=True)
    out = yc * lax.rsqrt(var + LN_EPS) * g_ref[...] + b_ref[...]
    o32_ref[...] = out
    o16_ref[...] = out.astype(BF)


def _mm_res_ln(x_parts, w_all, w_idx, res, g, b, *, alpha, coef, name):
    m = x_parts[0].shape[0]
    k, n = w_all.shape[-2:]
    tm = _pick(m, (256, 128))
    lead = (None,) * len(w_idx)
    return pl.pallas_call(
        functools.partial(_mm_res_ln_kernel, n_x=len(x_parts), alpha=alpha, coef=coef),
        grid=(m // tm,),
        in_specs=[pl.BlockSpec((tm, xp.shape[1]), lambda i: (i, 0)) for xp in x_parts] + [
                  pl.BlockSpec(lead + (k, n), lambda i: tuple(w_idx) + (0, 0), pipeline_mode=pl.Buffered(1)),
                  pl.BlockSpec((tm, n), lambda i: (i, 0)),
                  pl.BlockSpec((1, n), lambda i: (0, 0)),
                  pl.BlockSpec((1, n), lambda i: (0, 0))],
        out_specs=[pl.BlockSpec((tm, n), lambda i: (i, 0)),
                   pl.BlockSpec((tm, n), lambda i: (i, 0))],
        out_shape=[jax.ShapeDtypeStruct((m, n), F32), jax.ShapeDtypeStruct((m, n), BF)],
        compiler_params=_params("parallel"),
        name=name,
    )(*x_parts, w_all, res, g.reshape(1, n).astype(F32), b.reshape(1, n).astype(F32))


def _attn_kernel(qi_ref, kj_ref, fst_ref, lst_ref, msk_ref, *refs, variant, n_grp, n_rep, tq, tk, rc, dv, c_exp,
                 window, seq, has_sel, has_sink, gate_col, has_add, two_paths):
    it = iter(refs)
    if variant == "mla":
        qn_ref, qr_ref, kn_ref, kp_ref, v_ref = (next(it) for _ in range(5))
    elif variant == "swa":
        q_ref, kv_ref = next(it), next(it)
    else:
        q_ref, k_ref, v_ref = next(it), next(it), next(it)
    sel_ref = next(it) if has_sel else None
    sink_ref = next(it) if has_sink else None
    gl_ref, gb_ref = (next(it), next(it)) if gate_col is not None else (None, None)
    add_ref = next(it) if has_add else None
    o_ref = next(it)
    q_sc, m_sc, acc_sc, bias_sc = next(it), next(it), next(it), next(it)
    p = pl.program_id(1)
    qi = qi_ref[p]
    kj = kj_ref[p]
    rows = n_rep * tq

    @pl.when(fst_ref[p] == 1)
    def _():
        if has_sink:
            m_sc[...] = sink_ref[0]
            lane = lax.broadcasted_iota(jnp.int32, acc_sc.shape, 1)
            acc_sc[...] = jnp.where(lane >= dv, 1.0, 0.0)
        else:
            m_sc[...] = jnp.full_like(m_sc, NEG)
            acc_sc[...] = jnp.zeros_like(acc_sc)
        if variant == "mla":
            half_of_lane = lax.broadcasted_iota(jnp.int32, (tq, LANE), 1) // MLA_ROPE
            for r in range(n_rep):
                rope_pair = qr_ref[:, (r // 2) * LANE:(r // 2 + 1) * LANE].astype(F32) * c_exp
                q_sc[r * tq:(r + 1) * tq, :LANE] = (qn_ref[:, r * LANE:(r + 1) * LANE].astype(F32) * c_exp).astype(BF)
                q_sc[r * tq:(r + 1) * tq, LANE:] = jnp.where(half_of_lane == r % 2, rope_pair, 0.0).astype(BF)
        elif variant == "swa":
            low = lax.broadcasted_iota(jnp.int32, (tq, LANE), 1) < dv
            for r in range(n_rep):
                chunk = q_ref[:, (r // 2) * LANE:(r // 2 + 1) * LANE].astype(F32)
                if r % 2 == 1:
                    chunk = pltpu.roll(chunk, dv, 1)
                q_sc[r * tq:(r + 1) * tq, :] = jnp.where(low, chunk * c_exp, 0.0).astype(BF)
        else:
            dq = q_sc.shape[1]
            for r in range(n_rep):
                q_sc[r * tq:(r + 1) * tq, :] = (q_ref[:, r * dq:(r + 1) * dq].astype(F32) * c_exp).astype(BF)

    ones = jnp.ones((tk, dv), BF)
    if variant == "mla":
        ks = [jnp.concatenate([kn_ref[:, r * LANE:(r + 1) * LANE], kp_ref[...]], axis=1) for r in range(n_rep)]
        vs = [jnp.concatenate([v_ref[:, r * dv:(r + 1) * dv], ones], axis=1) for r in range(n_rep)]
    elif variant == "swa":
        swapped = pltpu.roll(kv_ref[...].astype(F32), dv, 1)
        ks = [kv_ref[...]]
        vs = [jnp.where(lax.broadcasted_iota(jnp.int32, (tk, LANE), 1) < dv, swapped, 1.0).astype(BF)]
    else:
        ks = [k_ref[...]]
        vs = [jnp.concatenate([v_ref[...], ones], axis=1)]

    def kv_of(u):
        return (u * rc) // tq if len(ks) > 1 else 0

    def tile_bias():
        row = lax.broadcasted_iota(jnp.int32, (tq, tk), 0)
        col = lax.broadcasted_iota(jnp.int32, (tq, tk), 1)
        dist = qi * tq - kj * tk + row - col
        mask = dist >= 0
        if window < seq:
            mask = mask & (dist < window)
        if has_sel:
            n_blk = sel_ref.shape[-1]
            blk_of_key = (kj * tk + lax.broadcasted_iota(jnp.int32, (n_blk, tk), 1)) // SLC_BLOCK
            expand = jnp.where(lax.broadcasted_iota(jnp.int32, (n_blk, tk), 0) == blk_of_key, 1.0, 0.0).astype(BF)
            picked = jnp.dot(sel_ref[0].astype(BF), expand, preferred_element_type=F32)
            mask = mask & (picked > 0.5)
        return jnp.where(mask, 0.0, NEG)

    def logits(u):
        return lax.dot_general(q_sc[u * rc:(u + 1) * rc, :], ks[kv_of(u)], (((1,), (1,)), ((), ())),
                               preferred_element_type=F32)

    def online_softmax(masked):
        if masked and has_sel:
            bias_sc[...] = tile_bias()
        if masked and not has_sel:
            rel = (lax.broadcasted_iota(jnp.int32, (rc, tk), 0) - lax.broadcasted_iota(jnp.int32, (rc, tk), 1))
        n_units = rows // rc
        s_next = logits(0)
        for u in range(n_units):
            s = s_next
            if u + 1 < n_units:
                s_next = logits(u + 1)
            us = slice(u * rc, (u + 1) * rc)
            m_prev = m_sc[us, :]
            if masked and has_sel:
                s = s + bias_sc[(u * rc) % tq:(u * rc) % tq + rc, :]
            elif masked:
                base = qi * tq - kj * tk + (u * rc) % tq
                visible = rel >= -base
                if window < seq:
                    visible = visible & (rel < window - base)
                s = jnp.where(visible, s, NEG)
            m_new = jnp.maximum(m_prev, jnp.max(s, -1, keepdims=True))
            w = jnp.exp2(s - _lane_tile(m_new, tk // LANE))
            m_sc[us, :] = m_new
            alpha = jnp.exp2(m_prev - m_new)
            acc_sc[us, :] = (_lane_tile(alpha, acc_sc.shape[1] // LANE) * acc_sc[us, :]
                             + jnp.dot(w.astype(BF), vs[kv_of(u)], preferred_element_type=F32))

    if not two_paths:
        online_softmax(True)
    else:
        pl.when(msk_ref[p] == 1)(lambda: online_softmax(True))
        pl.when(msk_ref[p] == 0)(lambda: online_softmax(False))

    @pl.when(lst_ref[p] == 1)
    def _():
        acc = acc_sc[...]
        if dv % LANE == 0:
            o = acc[:, :dv] / acc[:, dv:]
        else:
            o = acc[:, :dv] / acc[:, dv:dv + 1]
        if gate_col is not None:
            gates = jax.nn.sigmoid(gl_ref[...] + gb_ref[...])
        for r in range(n_rep):
            o_r = o[r * tq:(r + 1) * tq]
            if gate_col is not None:
                o_r = o_r * gates[:, 3 * r + gate_col:3 * r + gate_col + 1]
            if has_add:
                o_r = o_r + add_ref[:, r * dv:(r + 1) * dv].astype(F32)
            o_ref[:, r * dv:(r + 1) * dv] = o_r.astype(o_ref.dtype)


def _attn_call(variant, qkv_args, qkv_specs, *, batch, seq, n_grp, n_rep, dq, dv, scale, window, tq, tk,
               sel=None, sinks=None, gate=None, add=None, name="attn"):
    rows = n_rep * tq
    nq, nk = seq // tq, seq // tk
    pairs, fst, lst, msk = [], [], [], []
    for i in range(nq):
        lo = max(0, i * tq - (window - 1)) // tk
        hi = (i * tq + tq - 1) // tk
        for j in range(lo, hi + 1):
            pairs.append((i, j))
            fst.append(int(j == lo))
            lst.append(int(j == hi))
            inside = i * tq - (j * tk + tk - 1) >= 0 and i * tq + tq - 1 - j * tk < window
            msk.append(int(sel is not None or not inside))
    two_paths = 0 < sum(msk) < len(msk)
    qi = jnp.asarray([a for a, _ in pairs], jnp.int32)
    kj = jnp.asarray([b for _, b in pairs], jnp.int32)

    def q_rows(b, p, qi, kj):
        return (b // n_grp) * nq + qi[p]

    def k_rows(b, p, qi, kj):
        return (b // n_grp) * nk + kj[p]

    in_specs = [mk(q_rows, k_rows) for mk in qkv_specs]
    args = list(qkv_args)
    if sel is not None:
        in_specs.append(pl.BlockSpec((1, tq, sel.shape[-1]), lambda b, p, qi, kj, *_: (b, qi[p], 0)))
        args.append(sel)
    if sinks is not None:
        m0 = jnp.broadcast_to((sinks.astype(F32) * LOG2E)[:, :, None, None], (n_grp, n_rep, tq, LANE))
        in_specs.append(pl.BlockSpec((1, rows, LANE), lambda b, p, qi, kj, *_: (b % n_grp, 0, 0)))
        args.append(m0.reshape(n_grp, rows, LANE))
    gate_col = None
    if gate is not None:
        gl, gb, gate_col = gate
        in_specs += [pl.BlockSpec((tq, LANE), lambda b, p, qi, kj, *_: (q_rows(b, p, qi, kj), b % n_grp)),
                     pl.BlockSpec((1, LANE), lambda b, p, qi, kj, *_: (0, b % n_grp))]
        args += [gl, gb]
    out_spec = pl.BlockSpec((tq, n_rep * dv), lambda b, p, qi, kj, *_: (q_rows(b, p, qi, kj), b % n_grp))
    if add is not None:
        in_specs.append(out_spec)
        args.append(add)
    rc = min(tq, UNIT_ROWS)
    kern = functools.partial(_attn_kernel, variant=variant, n_grp=n_grp, n_rep=n_rep, tq=tq, tk=tk, rc=rc, dv=dv,
                             c_exp=scale * LOG2E, window=window, seq=seq, has_sel=sel is not None,
                             has_sink=sinks is not None, gate_col=gate_col, has_add=add is not None,
                             two_paths=two_paths)
    return pl.pallas_call(
        kern,
        grid_spec=pltpu.PrefetchScalarGridSpec(
            num_scalar_prefetch=5,
            grid=(batch * n_grp, len(pairs)),
            in_specs=in_specs,
            out_specs=out_spec,
            scratch_shapes=[pltpu.VMEM((rows, dq), BF), pltpu.VMEM((rows, LANE), F32),
                            pltpu.VMEM((rows, 2 * dv), F32), pltpu.VMEM((tq, tk), F32)]),
        out_shape=jax.ShapeDtypeStruct((batch * seq, n_grp * n_rep * dv), BF),
        compiler_params=_params("parallel", "arbitrary"),
        name=name,
    )(qi, kj, jnp.asarray(fst, jnp.int32), jnp.asarray(lst, jnp.int32), jnp.asarray(msk, jnp.int32), *args)


def _spec(shape, rows_of, col_fn):
    def make(q_rows, k_rows):
        rf = q_rows if rows_of == "q" else k_rows
        return pl.BlockSpec(shape, lambda b, p, qi, kj, *_: (rf(b, p, qi, kj), col_fn(b)))
    return make


def _band_kernel(*refs, n_rep, t, n_kt, rc, dv, c_exp, window, gate_col, has_add):
    it = iter(refs)
    q_ref = next(it)
    k_refs = [next(it) for _ in range(n_kt)]
    v_refs = [next(it) for _ in range(n_kt)]
    gl_ref, gb_ref = (next(it), next(it)) if gate_col is not None else (None, None)
    add_ref = next(it) if has_add else None
    o_ref = next(it)
    i = pl.program_id(1)
    n_keys = n_kt * t
    k = jnp.concatenate([r[...] for r in k_refs], axis=0)
    d = k.shape[1]
    v_ext = jnp.concatenate([jnp.concatenate([r[...] for r in v_refs], axis=0), jnp.ones((n_keys, dv), BF)], axis=1)
    col = lax.broadcasted_iota(jnp.int32, (rc, n_keys), 1)
    rel = lax.broadcasted_iota(jnp.int32, (rc, n_keys), 0) - col
    in_seq = col >= (n_kt - 1 - i) * t
    if gate_col is not None:
        gates = jax.nn.sigmoid(gl_ref[...] + gb_ref[...])

    def q_block(u):
        r, i0 = (u * rc) // t, (u * rc) % t
        return (q_ref[i0:i0 + rc, r * d:(r + 1) * d].astype(F32) * c_exp).astype(BF)

    def logits(u):
        return lax.dot_general(q_block(u), k, (((1,), (1,)), ((), ())), preferred_element_type=F32)

    n_units = n_rep * t // rc
    s_next = logits(0)
    for u in range(n_units):
        s = s_next
        if u + 1 < n_units:
            s_next = logits(u + 1)
        r, i0 = (u * rc) // t, (u * rc) % t
        base = (n_kt - 1) * t + i0
        visible = (rel >= -base) & (rel < window - base) & in_seq
        s = jnp.where(visible, s, NEG)
        w = jnp.exp2(s - jnp.max(s, -1, keepdims=True))
        ext = jnp.dot(w.astype(BF), v_ext, preferred_element_type=F32)
        o_u = ext[:, :dv] / ext[:, dv:]
        if gate_col is not None:
            o_u = o_u * gates[i0:i0 + rc, 3 * r + gate_col:3 * r + gate_col + 1]
        if has_add:
            o_u = o_u + add_ref[i0:i0 + rc, r * dv:(r + 1) * dv].astype(F32)
        o_ref[i0:i0 + rc, r * dv:(r + 1) * dv] = o_u.astype(o_ref.dtype)


def _band_call(z, q_col, k_col, v_col, *, batch, seq, n_grp, n_rep, d, scale, window, t, gate=None, add=None,
               name="band_attn"):
    dv = d
    n_kt = -(-(window - 1) // t) + 1
    nq = seq // t
    rc = min(t, UNIT_ROWS)

    def q_row(b, i):
        return (b // n_grp) * nq + i

    in_specs = [pl.BlockSpec((t, n_rep * d), lambda b, i: (q_row(b, i), q_col // (n_rep * d) + b % n_grp))]
    args = [z]
    for col in (k_col, v_col):
        for j in range(n_kt):
            in_specs.append(pl.BlockSpec(
                (t, d), lambda b, i, j=j, col=col: ((b // n_grp) * nq + jnp.maximum(i - (n_kt - 1) + j, 0),
                                                     col // d + b % n_grp)))
            args.append(z)
    gate_col = None
    if gate is not None:
        gl, gb, gate_col = gate
        in_specs += [pl.BlockSpec((t, LANE), lambda b, i: (q_row(b, i), b % n_grp)),
                     pl.BlockSpec((1, LANE), lambda b, i: (0, b % n_grp))]
        args += [gl, gb]
    out_spec = pl.BlockSpec((t, n_rep * dv), lambda b, i: (q_row(b, i), b % n_grp))
    if add is not None:
        in_specs.append(out_spec)
        args.append(add)
    kern = functools.partial(_band_kernel, n_rep=n_rep, t=t, n_kt=n_kt, rc=rc, dv=dv, c_exp=scale * LOG2E,
                             window=window, gate_col=gate_col, has_add=add is not None)
    return pl.pallas_call(
        kern,
        grid=(batch * n_grp, nq),
        in_specs=in_specs,
        out_specs=out_spec,
        out_shape=jax.ShapeDtypeStruct((batch * seq, n_grp * n_rep * dv), BF),
        compiler_params=_params("parallel", "parallel"),
        name=name,
    )(*args)


def _compress_kernel(x_ref, pa_ref, pb_ref, w1a_ref, w1b_ref, w2_ref, o_ref):
    x = x_ref[...].astype(F32)
    a = jnp.dot((x + pa_ref[...]).astype(BF), w1a_ref[...], preferred_element_type=F32)
    b = jnp.dot((x + pb_ref[...]).astype(BF), w1b_ref[...], preferred_element_type=F32)
    h = jax.nn.gelu(a + pltpu.roll(b, b.shape[0] - 1, 0))
    o_ref[0] = jnp.dot(h.astype(BF), w2_ref[...], preferred_element_type=F32).astype(o_ref.dtype)


def _compress(zc, first_blk, pos, w1, w2, *, batch, seq, n_grp, name):
    d = w2.shape[1]
    hid = w1.shape[1]
    nch = seq // CMP_STRIDE
    width = CMP_STRIDE * d
    pa = pos[:CMP_STRIDE].reshape(1, width).astype(F32)
    pb = pos[CMP_STRIDE:].reshape(1, width).astype(F32)
    w1 = w1.astype(BF)
    return pl.pallas_call(
        _compress_kernel,
        grid=(batch * n_grp,),
        in_specs=[pl.BlockSpec((nch, width), lambda b: (b // n_grp, first_blk + b % n_grp)),
                  pl.BlockSpec((1, width), lambda b: (0, 0)),
                  pl.BlockSpec((1, width), lambda b: (0, 0)),
                  pl.BlockSpec((width, hid), lambda b: (0, 0)),
                  pl.BlockSpec((width, hid), lambda b: (1, 0)),
                  pl.BlockSpec((hid, d), lambda b: (0, 0))],
        out_specs=pl.BlockSpec((1, nch, d), lambda b: (b, 0, 0)),
        out_shape=jax.ShapeDtypeStruct((batch * n_grp, nch, d), BF),
        compiler_params=_params("parallel"),
        name=name,
    )(zc, pa, pb, w1, w1, w2.astype(BF))


def _cmp_attn_kernel(q_ref, k_ref, v_ref, ovt_ref, gl_ref, gb_ref, o_ref, sel_ref, *, n_rep, tq, c_exp):
    qi = pl.program_id(1)
    nc, d = k_ref.shape[1], k_ref.shape[2]
    rows = n_rep * tq
    q = jnp.concatenate([q_ref[:, r * d:(r + 1) * d] for r in range(n_rep)], axis=0)
    s = lax.dot_general(q, k_ref[0], (((1,), (1,)), ((), ())), preferred_element_type=F32)
    qpos = qi * tq + lax.broadcasted_iota(jnp.int32, (tq, nc), 0)
    cmp_end = lax.broadcasted_iota(jnp.int32, (tq, nc), 1) * CMP_STRIDE + (CMP_BLOCK - 1)
    cmask = (cmp_end <= qpos)[None]
    s = jnp.where(cmask, s.reshape(n_rep, tq, nc), NEG)
    e = jnp.exp2((s - jnp.max(s, -1, keepdims=True)) * c_exp)
    e = jnp.where(cmask, e, 0.0).reshape(rows, nc)
    ext = jnp.dot(e.astype(BF), v_ref[0], preferred_element_type=F32)
    den = ext[:, d:]
    inv = 1.0 / jnp.where(den > 0.0, den, 1.0)
    gates = jax.nn.sigmoid(gl_ref[...] + gb_ref[...])
    o = ext[:, :d] * inv
    for r in range(n_rep):
        o_ref[:, r * d:(r + 1) * d] = (o[r * tq:(r + 1) * tq] * gates[:, 3 * r:3 * r + 1]).astype(o_ref.dtype)
    inv_c = inv[:, :nc] if nc <= d else _lane_tile(inv, nc // d)
    prob = e * inv_c
    psum = jnp.sum(prob.reshape(n_rep, tq, nc), axis=0)
    p_hi = psum.astype(BF)
    p_lo = (psum - p_hi.astype(F32)).astype(BF)
    contract_last = (((1,), (1,)), ((), ()))
    imp = (lax.dot_general(ovt_ref[...], p_hi, contract_last, preferred_element_type=F32)
           + lax.dot_general(ovt_ref[...], p_lo, contract_last, preferred_element_type=F32))
    ns = imp.shape[0]
    blk = lax.broadcasted_iota(jnp.int32, (ns, tq), 0)
    cur = (qi * tq + lax.broadcasted_iota(jnp.int32, (ns, tq), 1)) // SLC_BLOCK
    eligible = blk <= cur
    forced = (blk == 0) | (blk == cur) | (blk == cur - 1)
    score = jnp.where(eligible, jnp.where(forced, FORCE_SCORE, imp), -1.0)
    beaten_by = jnp.zeros((ns, tq), F32)
    for kk in range(ns):
        sk = score[kk:kk + 1, :]
        beats = (sk > score) | ((sk == score) & (blk > kk))
        beaten_by = beaten_by + jnp.where(beats, 1.0, 0.0)
    n_sel = min(N_SEL, ns)
    sel_t = jnp.where((beaten_by < n_sel) & (score >= 0.0), 1.0, 0.0)
    sel_sq = jnp.concatenate([sel_t, jnp.zeros((tq - ns, tq), F32)], axis=0) if ns < tq else sel_t
    sel_ref[0] = sel_sq.T[:, :ns]


def _cmp_attention(z, kcmp, vcmp, gl, gb, *, batch, seq, n_grp, n_rep, scale, tq):
    bg, nc, d = kcmp.shape
    ns = seq // SLC_BLOCK
    nq = seq // tq
    starts = np.arange(nc) * CMP_STRIDE
    jb = np.arange(ns)
    overlap = ((starts[:, None] < (jb[None, :] + 1) * SLC_BLOCK)
               & (starts[:, None] + CMP_BLOCK > jb[None, :] * SLC_BLOCK)).astype(np.float32)
    v_ext = jnp.concatenate([vcmp, jnp.ones_like(vcmp)], -1)
    tok_spec = pl.BlockSpec((tq, n_rep * d), lambda b, i: ((b // n_grp) * nq + i, b % n_grp))
    return pl.pallas_call(
        functools.partial(_cmp_attn_kernel, n_rep=n_rep, tq=tq, c_exp=scale * LOG2E),
        grid=(bg, nq),
        in_specs=[tok_spec,
                  pl.BlockSpec((1, nc, d), lambda b, i: (b, 0, 0)),
                  pl.BlockSpec((1, nc, 2 * d), lambda b, i: (b, 0, 0)),
                  pl.BlockSpec((ns, nc), lambda b, i: (0, 0)),
                  pl.BlockSpec((tq, LANE), lambda b, i: ((b // n_grp) * nq + i, b % n_grp)),
                  pl.BlockSpec((1, LANE), lambda b, i: (0, b % n_grp))],
        out_specs=[tok_spec, pl.BlockSpec((1, tq, ns), lambda b, i: (b, i, 0))],
        out_shape=[jax.ShapeDtypeStruct((batch * seq, n_grp * n_rep * d), BF),
                   jax.ShapeDtypeStruct((bg, seq, ns), F32)],
        compiler_params=_params("parallel", "parallel"),
        name="nsa_cmp_attn",
    )(z, kcmp, v_ext, jnp.asarray(overlap.T, BF), gl, gb)


def _rope_tables(seq, dim):
    inv = 1.0 / (ROPE_THETA ** (jnp.arange(0, dim, 2, dtype=F32) / dim))
    ang = jnp.arange(seq, dtype=F32)[:, None] * inv[None, :]
    cos_h = jnp.concatenate([jnp.cos(ang), jnp.cos(ang)], -1)
    sin_h = jnp.concatenate([-jnp.sin(ang), jnp.sin(ang)], -1)
    return jnp.tile(cos_h, (1, LANE // dim)), jnp.tile(sin_h, (1, LANE // dim))


def _pad_cols(w, n):
    return jnp.pad(w, ((0, 0), (0, n - w.shape[1])))


def _even_mixer(x16, batch, seq, w_in, q_norm, w_uq, kv_norm, w_ukv, sinks):
    H, dn, dr, dv = MLA_HEADS, MLA_NOPE, MLA_ROPE, MLA_V
    G, R, ds = SWA_KV_HEADS, SWA_HEADS // SWA_KV_HEADS, SWA_HEAD_DIM
    n_lat = MLA_Q_RANK + MLA_KV_RANK
    o_qs = n_lat + dr
    o_ks = o_qs + SWA_HEADS * ds
    o_vs = o_ks + G * ds
    w_kpe = w_in[:, n_lat:o_qs]
    cols = [w_in[:, o_qs:o_ks]]
    for g in range(G):
        cols += [w_in[:, o_ks + g * ds:o_ks + (g + 1) * ds], w_in[:, o_vs + g * ds:o_vs + (g + 1) * ds]]
    cols += [w_kpe, w_kpe]
    w_rg = jnp.concatenate(cols, -1)
    n_q = SWA_HEADS * ds
    cos, sin = _rope_tables(seq, ds)
    groups = "R" * (n_q // LANE) + "L" * G + "R"
    z, z_lat = _proj(x16, jnp.concatenate([w_rg, w_in[:, :n_lat]], -1).astype(BF),
                     rope=(cos, sin, ds // 2, groups), seq=seq, out_dtype=BF, n_tail=n_lat, name="even_in")

    w_uq_p = jnp.concatenate([w_uq.reshape(-1, H, dn + dr)[:, :, :dn].reshape(-1, H * dn),
                              w_uq.reshape(-1, H, dn + dr)[:, :, dn:].reshape(-1, H * dr)], -1)
    w_ukv_p = jnp.concatenate([w_ukv.reshape(-1, H, dn + dv)[:, :, :dn].reshape(-1, H * dn),
                               w_ukv.reshape(-1, H, dn + dv)[:, :, dn:].reshape(-1, H * dv)], -1)
    cos_q, sin_q = _rope_tables(seq, dr)
    q = _proj(z_lat, w_uq_p.astype(BF), k_block=0, k_size=MLA_Q_RANK, rms_g=q_norm,
              rope=(cos_q, sin_q, dr // 2, "N" * (H * dn // LANE) + "R" * (H * dr // LANE)), seq=seq, out_dtype=BF,
              name="mla_q_up")
    kv = _proj(z_lat, w_ukv_p.astype(BF), k_block=1, k_size=MLA_KV_RANK, rms_g=kv_norm, out_dtype=BF,
               name="mla_kv_up")
    tq = _pick(seq, (512, 256, 128))
    tk = _pick(seq, (512, 256, 128))
    kpe_blk = (n_q + 2 * G * ds) // LANE
    hp = MLA_HEADS_PER_STEP
    n_hg = H // hp
    o_mla = _attn_call(
        "mla", [q, q, kv, z, kv],
        [_spec((tq, hp * dn), "q", lambda b: b % n_hg),
         _spec((tq, hp * dr), "q", lambda b: H * dn // (hp * dr) + b % n_hg),
         _spec((tk, hp * dn), "k", lambda b: b % n_hg),
         _spec((tk, LANE), "k", lambda b: kpe_blk),
         _spec((tk, hp * dv), "k", lambda b: H * dn // (hp * dv) + b % n_hg)],
        batch=batch, seq=seq, n_grp=n_hg, n_rep=hp, dq=2 * LANE, dv=dv, scale=(dn + dr) ** -0.5, window=seq,
        tq=tq, tk=tk, name="mla_attn")

    tq = _pick(seq, (256, 128))
    o_swa = _attn_call(
        "swa", [z, z],
        [_spec((tq, R * ds), "q", lambda b: b % G),
         _spec((tq, LANE), "k", lambda b: n_q // LANE + b % G)],
        batch=batch, seq=seq, n_grp=G, n_rep=R, dq=LANE, dv=ds, scale=ds ** -0.5, window=SWA_WINDOW,
        tq=tq, tk=tq, sinks=sinks.reshape(G, R), name="swa_attn")
    return [o_mla, o_swa]


def _odd_mixer(x16, batch, seq, w_in, gate_b, cmp_pos_k, cmp_pos_v, k_w1, k_w2, v_w1, v_w2):
    G, d = NSA_KV_GROUPS, NSA_HEAD_DIM
    H = NSA_HEADS
    R = H // G
    kvw = G * d
    o_kc = H * d
    offs = [o_kc + i * kvw for i in range(7)]
    cos, sin = _rope_tables(seq, d)
    groups = "R" * H + ("R" * G + "N" * G) * 3
    w_gl = jnp.concatenate([_pad_cols(w_in[:, offs[6] + g * 3 * R:offs[6] + (g + 1) * 3 * R], LANE)
                            for g in range(G)], -1)
    cmp_groups = [offs[0] // d + g for g in range(G)] + [offs[1] // d + g for g in range(G)]
    z, gl, zc = _proj(x16, jnp.concatenate([w_in[:, :offs[6]], w_gl], -1).astype(BF),
                      rope=(cos, sin, d // 2, groups), seq=seq, out_dtype=BF, n_tail=G * LANE, chunked=cmp_groups,
                      name="odd_in")
    gb = jnp.concatenate([jnp.pad(gate_b[g * 3 * R:(g + 1) * 3 * R].astype(F32), (0, LANE - 3 * R))
                          for g in range(G)]).reshape(1, G * LANE)

    scale = d ** -0.5
    kcmp = _compress(zc, 0, cmp_pos_k, k_w1, k_w2, batch=batch, seq=seq, n_grp=G, name="nsa_compress_k")
    vcmp = _compress(zc, G, cmp_pos_v, v_w1, v_w2, batch=batch, seq=seq, n_grp=G, name="nsa_compress_v")
    tq = _pick(seq, (256, 128))
    o, sel = _cmp_attention(z, kcmp, vcmp, gl, gb, batch=batch, seq=seq, n_grp=G, n_rep=R, scale=scale, tq=tq)

    def branch(k_off, v_off, **kw):
        return _attn_call(
            "gqa", [z, z, z],
            [_spec((kw["tq"], R * d), "q", lambda b: b % G),
             _spec((kw["tk"], d), "k", lambda b: k_off // d + b % G),
             _spec((kw["tk"], d), "k", lambda b: v_off // d + b % G)],
            batch=batch, seq=seq, n_grp=G, n_rep=R, dq=d, dv=d, scale=scale, **kw)

    o = branch(offs[2], offs[3], window=seq, tq=_pick(seq, (512, 256, 128)), tk=_pick(seq, (512, 256, 128)),
               sel=sel, gate=(gl, gb, 1), add=o, name="nsa_slc_attn")
    o = _band_call(z, 0, offs[4], offs[5], batch=batch, seq=seq, n_grp=G, n_rep=R, d=d, scale=scale,
                   window=NSA_WINDOW, t=tq, gate=(gl, gb, 2), add=o, name="nsa_win_attn")
    return [o]


def kernel(x, ln_g, ln_b, ffn_w_gate, ffn_w_up, ffn_w_down, even_w_in, mla_q_norm, mla_w_uq, mla_kv_norm, mla_w_ukv, swa_sinks, even_w_out, odd_w_in, nsa_gate_b, nsa_cmp_pos_k, nsa_cmp_pos_v, nsa_cmp_k_w1, nsa_cmp_k_w2, nsa_cmp_v_w1, nsa_cmp_v_w2, odd_w_out):
    batch, seq, dm = x.shape
    depth = ln_g.shape[0]
    alpha = float((2 * depth) ** 0.25)
    x32 = x.reshape(batch * seq, dm)
    x16 = x32.astype(BF)
    even_out16 = even_w_out.astype(BF)
    odd_out16 = odd_w_out.astype(BF)

    def ffn(x32, x16, l, idx):
        h, w_down16 = _ffn_up(x16, ffn_w_gate, ffn_w_up, ffn_w_down, l, idx)
        return _mm_res_ln([h], w_down16, (), x32, ln_g[l, 2 * idx], ln_b[l, 2 * idx], alpha=alpha, coef=0.5,
                          name="ffn_down_ln")

    for l in range(depth):
        x32, x16 = ffn(x32, x16, l, 0)
        j = l // 2
        if l % 2 == 0:
            o = _even_mixer(x16, batch, seq, even_w_in[j], mla_q_norm[j], mla_w_uq[j], mla_kv_norm[j],
                            mla_w_ukv[j], swa_sinks[j])
            w_out = even_out16
        else:
            o = _odd_mixer(x16, batch, seq, odd_w_in[j], nsa_gate_b[j], nsa_cmp_pos_k[j], nsa_cmp_pos_v[j],
                           nsa_cmp_k_w1[j], nsa_cmp_k_w2[j], nsa_cmp_v_w1[j], nsa_cmp_v_w2[j])
            w_out = odd_out16
        x32, x16 = _mm_res_ln(o, w_out, (j,), x32, ln_g[l, 1], ln_b[l, 1], alpha=alpha, coef=1.0,
                              name="mixer_out_ln")
        x32, x16 = ffn(x32, x16, l, 1)
    return x32.reshape(batch, seq, dm)
```

```python
import functools
import math

import numpy as np
import jax
import jax.numpy as jnp
from jax import lax
from jax.experimental import pallas as pl
from jax.experimental.pallas import tpu as pltpu

F32 = jnp.float32
BF = jnp.bfloat16

ROPE_THETA = 10000.0
LN_EPS = 1e-5
RMS_EPS = 1e-6
NEG = -1e30
FORCE_SCORE = 1e9
LOG2E = math.log2(math.e)

MLA_HEADS, MLA_NOPE, MLA_ROPE, MLA_V = 8, 128, 64, 128
MLA_Q_RANK, MLA_KV_RANK = 512, 512
SWA_HEADS, SWA_KV_HEADS, SWA_HEAD_DIM, SWA_WINDOW = 16, 2, 64, 128
NSA_HEADS, NSA_KV_GROUPS, NSA_HEAD_DIM = 16, 2, 128
CMP_BLOCK, CMP_STRIDE, CMP_HIDDEN = 32, 16, 256
SLC_BLOCK, N_SEL, NSA_WINDOW = 64, 8, 512

LANE = 128
UNIT_ROWS = 128
MLA_HEADS_PER_STEP = 8
VMEM_LIMIT_BYTES = 52 * 1024 * 1024


def _params(*sem):
    return pltpu.CompilerParams(dimension_semantics=sem, vmem_limit_bytes=VMEM_LIMIT_BYTES)


def _pick(n, candidates):
    for c in candidates:
        if n % c == 0:
            return c
    raise ValueError(f"no tile in {candidates} divides {n}")


def _lane_tile(x, n):
    return x if n == 1 else jnp.concatenate([x] * n, axis=1)


def _proj_kernel(*refs, has_rms, groups, half, n_tail, chunked):
    it = iter(refs)
    x_ref, w_ref = next(it), next(it)
    g_ref = next(it) if has_rms else None
    cos_ref, sin_ref = (next(it), next(it)) if groups else (None, None)
    o_ref = next(it)
    tail_ref = next(it) if n_tail else None
    chunk_ref, stage = (next(it), next(it)) if chunked else (None, None)
    x = x_ref[...]
    if has_rms:
        xf = x.astype(F32)
        x = xf * lax.rsqrt(jnp.mean(xf * xf, -1, keepdims=True) + RMS_EPS) * g_ref[...]
    acc = jnp.dot(x.astype(BF), w_ref[...], preferred_element_type=F32)
    if n_tail:
        tail_ref[...] = acc[:, acc.shape[1] - n_tail:]
        acc = acc[:, :acc.shape[1] - n_tail]
    if not groups:
        o_ref[...] = acc.astype(o_ref.dtype)
        return
    cos, sin = cos_ref[...], sin_ref[...]
    lane = lax.broadcasted_iota(jnp.int32, cos.shape, 1)
    low = lane < LANE // 2
    cos_l, sin_l = jnp.where(low, cos, 1.0), jnp.where(low, sin, 0.0)
    for c, kind in enumerate(groups):
        a = acc[:, c * LANE:(c + 1) * LANE]
        if kind != "N":
            if 2 * half == LANE:
                partner = pltpu.roll(a, half, 1)
            else:
                partner = jnp.where(lane % (2 * half) < half, pltpu.roll(a, LANE - half, 1), pltpu.roll(a, half, 1))
            a = a * cos + partner * sin if kind == "R" else a * cos_l + partner * sin_l
        o_ref[:, c * LANE:(c + 1) * LANE] = a.astype(o_ref.dtype)
        if c in chunked:
            stage[...] = a
            n_ch = a.shape[0] // CMP_STRIDE
            for l in range(CMP_STRIDE):
                lanes = slice((chunked.index(c) * CMP_STRIDE + l) * LANE, (chunked.index(c) * CMP_STRIDE + l + 1) * LANE)
                chunk_ref[:, lanes] = stage[pl.ds(l, n_ch, stride=CMP_STRIDE), :].astype(chunk_ref.dtype)


def _proj(x, w, *, k_block=0, k_size=None, rms_g=None, rope=None, seq=None, out_dtype=F32, n_tail=0, chunked=(),
          name="proj"):
    m = x.shape[0]
    k_size = x.shape[1] if k_size is None else k_size
    n = w.shape[1]
    n_main = n - n_tail
    out_bytes = jnp.dtype(out_dtype).itemsize
    for tm in (512, 256, 128):
        est = k_size * n * 2 + 2 * tm * k_size * x.dtype.itemsize + 2 * tm * n * out_bytes + 3 * tm * n * 4
        if m % tm == 0 and est <= VMEM_LIMIT_BYTES * 3 // 4:
            break
    in_specs = [pl.BlockSpec((tm, k_size), lambda i: (i, k_block)),
                pl.BlockSpec((k_size, n), lambda i: (0, 0), pipeline_mode=pl.Buffered(1))]
    args = [x, w]
    if rms_g is not None:
        in_specs.append(pl.BlockSpec((1, k_size), lambda i: (0, 0)))
        args.append(rms_g.reshape(1, k_size).astype(F32))
    half, groups = 0, ()
    if rope is not None:
        cos, sin, half, groups = rope
        sb = seq // tm
        in_specs += [pl.BlockSpec((tm, LANE), lambda i: (i % sb, 0))] * 2
        args += [cos, sin]
    out_specs = [pl.BlockSpec((tm, n_main), lambda i: (i, 0))]
    out_shape = [jax.ShapeDtypeStruct((m, n_main), out_dtype)]
    if n_tail:
        out_specs.append(pl.BlockSpec((tm, n_tail), lambda i: (i, 0)))
        out_shape.append(jax.ShapeDtypeStruct((m, n_tail), F32))
    scratch = []
    if chunked:
        width = len(chunked) * CMP_STRIDE * LANE
        out_specs.append(pl.BlockSpec((tm // CMP_STRIDE, width), lambda i: (i, 0)))
        out_shape.append(jax.ShapeDtypeStruct((m // CMP_STRIDE, width), out_dtype))
        scratch.append(pltpu.VMEM((tm, LANE), F32))
    out = pl.pallas_call(
        functools.partial(_proj_kernel, has_rms=rms_g is not None, groups=tuple(groups), half=half, n_tail=n_tail,
                          chunked=tuple(chunked)),
        grid=(m // tm,),
        in_specs=in_specs,
        out_specs=out_specs,
        out_shape=out_shape,
        scratch_shapes=scratch,
        compiler_params=_params("parallel"),
        name=name,
    )(*args)
    return out if len(out) > 1 else out[0]


def _ffn_up_kernel(x_ref, wg_ref, wu_ref, wd_ref, o_ref, wd16_ref, wg16, wu16):
    @pl.when(pl.program_id(1) == 0)
    def _():
        wg16[...] = wg_ref[...].astype(BF)
        wu16[...] = wu_ref[...].astype(BF)

    wd16_ref[...] = wd_ref[...].astype(BF)
    x = x_ref[...]
    g = jnp.dot(x, wg16[...], preferred_element_type=F32)
    u = jnp.dot(x, wu16[...], preferred_element_type=F32)
    o_ref[...] = (g * jax.nn.sigmoid(g) * u).astype(o_ref.dtype)


def _ffn_up(x16, wg_all, wu_all, wd_all, layer, idx):
    m, k = x16.shape
    n = wg_all.shape[-1]
    n_out = wd_all.shape[-1]
    tm = _pick(m, (1024, 512, 256, 128))
    tn = _pick(n, (512, 256, 128))
    ni = m // tm
    slab = n // ((n // tn) * ni)
    assert slab * (n // tn) * ni == n and slab % 16 == 0
    w_spec = pl.BlockSpec((None, None, k, tn), lambda j, i: (layer, idx, 0, j))
    return pl.pallas_call(
        _ffn_up_kernel,
        grid=(n // tn, ni),
        in_specs=[pl.BlockSpec((tm, k), lambda j, i: (i, 0)), w_spec, w_spec,
                  pl.BlockSpec((None, None, slab, n_out), lambda j, i: (layer, idx, j * ni + i, 0))],
        out_specs=[pl.BlockSpec((tm, tn), lambda j, i: (i, j)),
                   pl.BlockSpec((slab, n_out), lambda j, i: (j * ni + i, 0))],
        out_shape=[jax.ShapeDtypeStruct((m, n), BF), jax.ShapeDtypeStruct((n, n_out), BF)],
        scratch_shapes=[pltpu.VMEM((k, tn), BF), pltpu.VMEM((k, tn), BF)],
        compiler_params=_params("parallel", "arbitrary"),
        name="ffn_up",
    )(x16, wg_all, wu_all, wd_all)


def _mm_res_ln_kernel(*refs, n_x, alpha, coef):
    x_refs = refs[:n_x]
    w_ref, res_ref, g_ref, b_ref, o32_ref, o16_ref = refs[n_x:]
    x = x_refs[0][...] if n_x == 1 else jnp.concatenate([r[...] for r in x_refs], axis=1)
    y = alpha * res_ref[...] + coef * jnp.dot(x, w_ref[...], preferred_element_type=F32)
    mu = jnp.mean(y, -1, keepdims=True)
    yc = y - mu
    var = jnp.mean(yc * yc, -1, keepdims=True)
    out = yc * lax.rsqrt(var + LN_EPS) * g_ref[...] + b_ref[...]
    o32_ref[...] = out
    o16_ref[...] = out.astype(BF)


def _mm_res_ln(x_parts, w_all, w_idx, res, g, b, *, alpha, coef, name):
    m = x_parts[0].shape[0]
    k, n = w_all.shape[-2:]
    tm = _pick(m, (256, 128))
    lead = (None,) * len(w_idx)
    return pl.pallas_call(
        functools.partial(_mm_res_ln_kernel, n_x=len(x_parts), alpha=alpha, coef=coef),
        grid=(m // tm,),
        in_specs=[pl.BlockSpec((tm, xp.shape[1]), lambda i: (i, 0)) for xp in x_parts] + [
                  pl.BlockSpec(lead + (k, n), lambda i: tuple(w_idx) + (0, 0), pipeline_mode=pl.Buffered(1)),
                  pl.BlockSpec((tm, n), lambda i: (i, 0)),
                  pl.BlockSpec((1, n), lambda i: (0, 0)),
                  pl.BlockSpec((1, n), lambda i: (0, 0))],
        out_specs=[pl.BlockSpec((tm, n), lambda i: (i, 0)),
                   pl.BlockSpec((tm, n), lambda i: (i, 0))],
        out_shape=[jax.ShapeDtypeStruct((m, n), F32), jax.ShapeDtypeStruct((m, n), BF)],
        compiler_params=_params("parallel"),
        name=name,
    )(*x_parts, w_all, res, g.reshape(1, n).astype(F32), b.reshape(1, n).astype(F32))


def _attn_kernel(qi_ref, kj_ref, fst_ref, lst_ref, msk_ref, *refs, variant, n_grp, n_rep, tq, tk, rc, dv, c_exp,
                 window, seq, has_sel, has_sink, gate_col, has_add, two_paths):
    it = iter(refs)
    if variant == "mla":
        qn_ref, qr_ref, kn_ref, kp_ref, v_ref = (next(it) for _ in range(5))
    elif variant == "swa":
        q_ref, kv_ref = next(it), next(it)
    else:
        q_ref, k_ref, v_ref = next(it), next(it), next(it)
    sel_ref = next(it) if has_sel else None
    sink_ref = next(it) if has_sink else None
    gl_ref, gb_ref = (next(it), next(it)) if gate_col is not None else (None, None)
    add_ref = next(it) if has_add else None
    o_ref = next(it)
    q_sc, m_sc, acc_sc, bias_sc = next(it), next(it), next(it), next(it)
    p = pl.program_id(1)
    qi = qi_ref[p]
    kj = kj_ref[p]
    rows = n_rep * tq

    @pl.when(fst_ref[p] == 1)
    def _():
        if has_sink:
            m_sc[...] = sink_ref[0]
            lane = lax.broadcasted_iota(jnp.int32, acc_sc.shape, 1)
            acc_sc[...] = jnp.where(lane >= dv, 1.0, 0.0)
        else:
            m_sc[...] = jnp.full_like(m_sc, NEG)
            acc_sc[...] = jnp.zeros_like(acc_sc)
        if variant == "mla":
            half_of_lane = lax.broadcasted_iota(jnp.int32, (tq, LANE), 1) // MLA_ROPE
            for r in range(n_rep):
                rope_pair = qr_ref[:, (r // 2) * LANE:(r // 2 + 1) * LANE].astype(F32) * c_exp
                q_sc[r * tq:(r + 1) * tq, :LANE] = (qn_ref[:, r * LANE:(r + 1) * LANE].astype(F32) * c_exp).astype(BF)
                q_sc[r * tq:(r + 1) * tq, LANE:] = jnp.where(half_of_lane == r % 2, rope_pair, 0.0).astype(BF)
        elif variant == "swa":
            low = lax.broadcasted_iota(jnp.int32, (tq, LANE), 1) < dv
            for r in range(n_rep):
                chunk = q_ref[:, (r // 2) * LANE:(r // 2 + 1) * LANE].astype(F32)
                if r % 2 == 1:
                    chunk = pltpu.roll(chunk, dv, 1)
                q_sc[r * tq:(r + 1) * tq, :] = jnp.where(low, chunk * c_exp, 0.0).astype(BF)
        else:
            dq = q_sc.shape[1]
            for r in range(n_rep):
                q_sc[r * tq:(r + 1) * tq, :] = (q_ref[:, r * dq:(r + 1) * dq].astype(F32) * c_exp).astype(BF)

    ones = jnp.ones((tk, dv), BF)
    if variant == "mla":
        ks = [jnp.concatenate([kn_ref[:, r * LANE:(r + 1) * LANE], kp_ref[...]], axis=1) for r in range(n_rep)]
        vs = [jnp.concatenate([v_ref[:, r * dv:(r + 1) * dv], ones], axis=1) for r in range(n_rep)]
    elif variant == "swa":
        swapped = pltpu.roll(kv_ref[...].astype(F32), dv, 1)
        ks = [kv_ref[...]]
        vs = [jnp.where(lax.broadcasted_iota(jnp.int32, (tk, LANE), 1) < dv, swapped, 1.0).astype(BF)]
    else:
        ks = [k_ref[...]]
        vs = [jnp.concatenate([v_ref[...], ones], axis=1)]

    def kv_of(u):
        return (u * rc) // tq if len(ks) > 1 else 0

    def tile_bias():
        row = lax.broadcasted_iota(jnp.int32, (tq, tk), 0)
        col = lax.broadcasted_iota(jnp.int32, (tq, tk), 1)
        dist = qi * tq - kj * tk + row - col
        mask = dist >= 0
        if window < seq:
            mask = mask & (dist < window)
        if has_sel:
            n_blk = sel_ref.shape[-1]
            blk_of_key = (kj * tk + lax.broadcasted_iota(jnp.int32, (n_blk, tk), 1)) // SLC_BLOCK
            expand = jnp.where(lax.broadcasted_iota(jnp.int32, (n_blk, tk), 0) == blk_of_key, 1.0, 0.0).astype(BF)
            picked = jnp.dot(sel_ref[0].astype(BF), expand, preferred_element_type=F32)
            mask = mask & (picked > 0.5)
        return jnp.where(mask, 0.0, NEG)

    def logits(u):
        return lax.dot_general(q_sc[u * rc:(u + 1) * rc, :], ks[kv_of(u)], (((1,), (1,)), ((), ())),
                               preferred_element_type=F32)

    def online_softmax(masked):
        if masked and has_sel:
            bias_sc[...] = tile_bias()
        if masked and not has_sel:
            rel = (lax.broadcasted_iota(jnp.int32, (rc, tk), 0) - lax.broadcasted_iota(jnp.int32, (rc, tk), 1))
        n_units = rows // rc
        s_next = logits(0)
        for u in range(n_units):
            s = s_next
            if u + 1 < n_units:
                s_next = logits(u + 1)
            us = slice(u * rc, (u + 1) * rc)
            m_prev = m_sc[us, :]
            if masked and has_sel:
                s = s + bias_sc[(u * rc) % tq:(u * rc) % tq + rc, :]
            elif masked:
                base = qi * tq - kj * tk + (u * rc) % tq
                visible = rel >= -base
                if window < seq:
                    visible = visible & (rel < window - base)
                s = jnp.where(visible, s, NEG)
            m_new = jnp.maximum(m_prev, jnp.max(s, -1, keepdims=True))
            w = jnp.exp2(s - _lane_tile(m_new, tk // LANE))
            m_sc[us, :] = m_new
            alpha = jnp.exp2(m_prev - m_new)
            acc_sc[us, :] = (_lane_tile(alpha, acc_sc.shape[1] // LANE) * acc_sc[us, :]
                             + jnp.dot(w.astype(BF), vs[kv_of(u)], preferred_element_type=F32))

    if not two_paths:
        online_softmax(True)
    else:
        pl.when(msk_ref[p] == 1)(lambda: online_softmax(True))
        pl.when(msk_ref[p] == 0)(lambda: online_softmax(False))

    @pl.when(lst_ref[p] == 1)
    def _():
        acc = acc_sc[...]
        if dv % LANE == 0:
            o = acc[:, :dv] / acc[:, dv:]
        else:
            o = acc[:, :dv] / acc[:, dv:dv + 1]
        if gate_col is not None:
            gates = jax.nn.sigmoid(gl_ref[...] + gb_ref[...])
        for r in range(n_rep):
            o_r = o[r * tq:(r + 1) * tq]
            if gate_col is not None:
                o_r = o_r * gates[:, 3 * r + gate_col:3 * r + gate_col + 1]
            if has_add:
                o_r = o_r + add_ref[:, r * dv:(r + 1) * dv].astype(F32)
            o_ref[:, r * dv:(r + 1) * dv] = o_r.astype(o_ref.dtype)


def _attn_call(variant, qkv_args, qkv_specs, *, batch, seq, n_grp, n_rep, dq, dv, scale, window, tq, tk,
               sel=None, sinks=None, gate=None, add=None, unit_rows=UNIT_ROWS, name="attn"):
    rows = n_rep * tq
    nq, nk = seq // tq, seq // tk
    pairs, fst, lst, msk = [], [], [], []
    for i in range(nq):
        lo = max(0, i * tq - (window - 1)) // tk
        hi = (i * tq + tq - 1) // tk
        for j in range(lo, hi + 1):
            pairs.append((i, j))
            fst.append(int(j == lo))
            lst.append(int(j == hi))
            inside = i * tq - (j * tk + tk - 1) >= 0 and i * tq + tq - 1 - j * tk < window
            msk.append(int(sel is not None or not inside))
    two_paths = 0 < sum(msk) < len(msk)
    qi = jnp.asarray([a for a, _ in pairs], jnp.int32)
    kj = jnp.asarray([b for _, b in pairs], jnp.int32)

    def q_rows(b, p, qi, kj):
        return (b // n_grp) * nq + qi[p]

    def k_rows(b, p, qi, kj):
        return (b // n_grp) * nk + kj[p]

    in_specs = [mk(q_rows, k_rows) for mk in qkv_specs]
    args = list(qkv_args)
    if sel is not None:
        in_specs.append(pl.BlockSpec((1, tq, sel.shape[-1]), lambda b, p, qi, kj, *_: (b, qi[p], 0)))
        args.append(sel)
    if sinks is not None:
        m0 = jnp.broadcast_to((sinks.astype(F32) * LOG2E)[:, :, None, None], (n_grp, n_rep, tq, LANE))
        in_specs.append(pl.BlockSpec((1, rows, LANE), lambda b, p, qi, kj, *_: (b % n_grp, 0, 0)))
        args.append(m0.reshape(n_grp, rows, LANE))
    gate_col = None
    if gate is not None:
        gl, gb, gate_col = gate
        in_specs += [pl.BlockSpec((tq, LANE), lambda b, p, qi, kj, *_: (q_rows(b, p, qi, kj), b % n_grp)),
                     pl.BlockSpec((1, LANE), lambda b, p, qi, kj, *_: (0, b % n_grp))]
        args += [gl, gb]
    out_spec = pl.BlockSpec((tq, n_rep * dv), lambda b, p, qi, kj, *_: (q_rows(b, p, qi, kj), b % n_grp))
    if add is not None:
        in_specs.append(out_spec)
        args.append(add)
    rc = min(tq, unit_rows)
    kern = functools.partial(_attn_kernel, variant=variant, n_grp=n_grp, n_rep=n_rep, tq=tq, tk=tk, rc=rc, dv=dv,
                             c_exp=scale * LOG2E, window=window, seq=seq, has_sel=sel is not None,
                             has_sink=sinks is not None, gate_col=gate_col, has_add=add is not None,
                             two_paths=two_paths)
    return pl.pallas_call(
        kern,
        grid_spec=pltpu.PrefetchScalarGridSpec(
            num_scalar_prefetch=5,
            grid=(batch * n_grp, len(pairs)),
            in_specs=in_specs,
            out_specs=out_spec,
            scratch_shapes=[pltpu.VMEM((rows, dq), BF), pltpu.VMEM((rows, LANE), F32),
                            pltpu.VMEM((rows, 2 * dv), F32), pltpu.VMEM((tq, tk), F32)]),
        out_shape=jax.ShapeDtypeStruct((batch * seq, n_grp * n_rep * dv), BF),
        compiler_params=_params("parallel", "arbitrary"),
        name=name,
    )(qi, kj, jnp.asarray(fst, jnp.int32), jnp.asarray(lst, jnp.int32), jnp.asarray(msk, jnp.int32), *args)


def _spec(shape, rows_of, col_fn):
    def make(q_rows, k_rows):
        rf = q_rows if rows_of == "q" else k_rows
        return pl.BlockSpec(shape, lambda b, p, qi, kj, *_: (rf(b, p, qi, kj), col_fn(b)))
    return make


def _band_kernel(*refs, n_rep, t, n_kt, rc, dv, c_exp, window, gate_col, has_add):
    it = iter(refs)
    q_ref = next(it)
    k_refs = [next(it) for _ in range(n_kt)]
    v_refs = [next(it) for _ in range(n_kt)]
    gl_ref, gb_ref = (next(it), next(it)) if gate_col is not None else (None, None)
    add_ref = next(it) if has_add else None
    o_ref = next(it)
    i = pl.program_id(1)
    n_keys = n_kt * t
    k = jnp.concatenate([r[...] for r in k_refs], axis=0)
    d = k.shape[1]
    v_ext = jnp.concatenate([jnp.concatenate([r[...] for r in v_refs], axis=0), jnp.ones((n_keys, dv), BF)], axis=1)
    col = lax.broadcasted_iota(jnp.int32, (rc, n_keys), 1)
    rel = lax.broadcasted_iota(jnp.int32, (rc, n_keys), 0) - col
    in_seq = col >= (n_kt - 1 - i) * t
    if gate_col is not None:
        gates = jax.nn.sigmoid(gl_ref[...] + gb_ref[...])

    def q_block(u):
        r, i0 = (u * rc) // t, (u * rc) % t
        return (q_ref[i0:i0 + rc, r * d:(r + 1) * d].astype(F32) * c_exp).astype(BF)

    def logits(u):
        return lax.dot_general(q_block(u), k, (((1,), (1,)), ((), ())), preferred_element_type=F32)

    n_units = n_rep * t // rc
    s_next = logits(0)
    for u in range(n_units):
        s = s_next
        if u + 1 < n_units:
            s_next = logits(u + 1)
        r, i0 = (u * rc) // t, (u * rc) % t
        base = (n_kt - 1) * t + i0
        visible = (rel >= -base) & (rel < window - base) & in_seq
        s = jnp.where(visible, s, NEG)
        w = jnp.exp2(s - jnp.max(s, -1, keepdims=True))
        ext = jnp.dot(w.astype(BF), v_ext, preferred_element_type=F32)
        o_u = ext[:, :dv] / ext[:, dv:]
        if gate_col is not None:
            o_u = o_u * gates[i0:i0 + rc, 3 * r + gate_col:3 * r + gate_col + 1]
        if has_add:
            o_u = o_u + add_ref[i0:i0 + rc, r * dv:(r + 1) * dv].astype(F32)
        o_ref[i0:i0 + rc, r * dv:(r + 1) * dv] = o_u.astype(o_ref.dtype)


def _band_call(z, q_col, k_col, v_col, *, batch, seq, n_grp, n_rep, d, scale, window, t, gate=None, add=None,
               name="band_attn"):
    dv = d
    n_kt = -(-(window - 1) // t) + 1
    nq = seq // t
    rc = min(t, UNIT_ROWS)

    def q_row(b, i):
        return (b // n_grp) * nq + i

    in_specs = [pl.BlockSpec((t, n_rep * d), lambda b, i: (q_row(b, i), q_col // (n_rep * d) + b % n_grp))]
    args = [z]
    for col in (k_col, v_col):
        for j in range(n_kt):
            in_specs.append(pl.BlockSpec(
                (t, d), lambda b, i, j=j, col=col: ((b // n_grp) * nq + jnp.maximum(i - (n_kt - 1) + j, 0),
                                                     col // d + b % n_grp)))
            args.append(z)
    gate_col = None
    if gate is not None:
        gl, gb, gate_col = gate
        in_specs += [pl.BlockSpec((t, LANE), lambda b, i: (q_row(b, i), b % n_grp)),
                     pl.BlockSpec((1, LANE), lambda b, i: (0, b % n_grp))]
        args += [gl, gb]
    out_spec = pl.BlockSpec((t, n_rep * dv), lambda b, i: (q_row(b, i), b % n_grp))
    if add is not None:
        in_specs.append(out_spec)
        args.append(add)
    kern = functools.partial(_band_kernel, n_rep=n_rep, t=t, n_kt=n_kt, rc=rc, dv=dv, c_exp=scale * LOG2E,
                             window=window, gate_col=gate_col, has_add=add is not None)
    return pl.pallas_call(
        kern,
        grid=(batch * n_grp, nq),
        in_specs=in_specs,
        out_specs=out_spec,
        out_shape=jax.ShapeDtypeStruct((batch * seq, n_grp * n_rep * dv), BF),
        compiler_params=_params("parallel", "parallel"),
        name=name,
    )(*args)


def _compress_kernel(x_ref, pa_ref, pb_ref, w1a_ref, w1b_ref, w2_ref, o_ref):
    x = x_ref[...].astype(F32)
    a = jnp.dot((x + pa_ref[...]).astype(BF), w1a_ref[...], preferred_element_type=F32)
    b = jnp.dot((x + pb_ref[...]).astype(BF), w1b_ref[...], preferred_element_type=F32)
    h = jax.nn.gelu(a + pltpu.roll(b, b.shape[0] - 1, 0))
    o_ref[0] = jnp.dot(h.astype(BF), w2_ref[...], preferred_element_type=F32).astype(o_ref.dtype)


def _compress(zc, first_blk, pos, w1, w2, *, batch, seq, n_grp, name):
    d = w2.shape[1]
    hid = w1.shape[1]
    nch = seq // CMP_STRIDE
    width = CMP_STRIDE * d
    pa = pos[:CMP_STRIDE].reshape(1, width).astype(F32)
    pb = pos[CMP_STRIDE:].reshape(1, width).astype(F32)
    w1 = w1.astype(BF)
    return pl.pallas_call(
        _compress_kernel,
        grid=(batch * n_grp,),
        in_specs=[pl.BlockSpec((nch, width), lambda b: (b // n_grp, first_blk + b % n_grp)),
                  pl.BlockSpec((1, width), lambda b: (0, 0)),
                  pl.BlockSpec((1, width), lambda b: (0, 0)),
                  pl.BlockSpec((width, hid), lambda b: (0, 0)),
                  pl.BlockSpec((width, hid), lambda b: (1, 0)),
                  pl.BlockSpec((hid, d), lambda b: (0, 0))],
        out_specs=pl.BlockSpec((1, nch, d), lambda b: (b, 0, 0)),
        out_shape=jax.ShapeDtypeStruct((batch * n_grp, nch, d), BF),
        compiler_params=_params("parallel"),
        name=name,
    )(zc, pa, pb, w1, w1, w2.astype(BF))


def _cmp_attn_kernel(q_ref, k_ref, v_ref, ovt_ref, gl_ref, gb_ref, o_ref, sel_ref, *, n_rep, tq, c_exp):
    qi = pl.program_id(1)
    nc, d = k_ref.shape[1], k_ref.shape[2]
    rows = n_rep * tq
    q = jnp.concatenate([q_ref[:, r * d:(r + 1) * d] for r in range(n_rep)], axis=0)
    s = lax.dot_general(q, k_ref[0], (((1,), (1,)), ((), ())), preferred_element_type=F32)
    qpos = qi * tq + lax.broadcasted_iota(jnp.int32, (tq, nc), 0)
    cmp_end = lax.broadcasted_iota(jnp.int32, (tq, nc), 1) * CMP_STRIDE + (CMP_BLOCK - 1)
    cmask = (cmp_end <= qpos)[None]
    s = jnp.where(cmask, s.reshape(n_rep, tq, nc), NEG)
    e = jnp.exp2((s - jnp.max(s, -1, keepdims=True)) * c_exp)
    e = jnp.where(cmask, e, 0.0).reshape(rows, nc)
    ext = jnp.dot(e.astype(BF), v_ref[0], preferred_element_type=F32)
    den = ext[:, d:]
    inv = 1.0 / jnp.where(den > 0.0, den, 1.0)
    gates = jax.nn.sigmoid(gl_ref[...] + gb_ref[...])
    o = ext[:, :d] * inv
    for r in range(n_rep):
        o_ref[:, r * d:(r + 1) * d] = (o[r * tq:(r + 1) * tq] * gates[:, 3 * r:3 * r + 1]).astype(o_ref.dtype)
    inv_c = inv[:, :nc] if nc <= d else _lane_tile(inv, nc // d)
    prob = e * inv_c
    psum = jnp.sum(prob.reshape(n_rep, tq, nc), axis=0)
    p_hi = psum.astype(BF)
    p_lo = (psum - p_hi.astype(F32)).astype(BF)
    contract_last = (((1,), (1,)), ((), ()))
    imp = (lax.dot_general(ovt_ref[...], p_hi, contract_last, preferred_element_type=F32)
           + lax.dot_general(ovt_ref[...], p_lo, contract_last, preferred_element_type=F32))
    ns = imp.shape[0]
    blk = lax.broadcasted_iota(jnp.int32, (ns, tq), 0)
    cur = (qi * tq + lax.broadcasted_iota(jnp.int32, (ns, tq), 1)) // SLC_BLOCK
    eligible = blk <= cur
    forced = (blk == 0) | (blk == cur) | (blk == cur - 1)
    score = jnp.where(eligible, jnp.where(forced, FORCE_SCORE, imp), -1.0)
    beaten_by = jnp.zeros((ns, tq), F32)
    for kk in range(ns):
        sk = score[kk:kk + 1, :]
        beats = (sk > score) | ((sk == score) & (blk > kk))
        beaten_by = beaten_by + jnp.where(beats, 1.0, 0.0)
    n_sel = min(N_SEL, ns)
    sel_t = jnp.where((beaten_by < n_sel) & (score >= 0.0), 1.0, 0.0)
    sel_sq = jnp.concatenate([sel_t, jnp.zeros((tq - ns, tq), F32)], axis=0) if ns < tq else sel_t
    sel_ref[0] = sel_sq.T[:, :ns]


def _cmp_attention(z, kcmp, vcmp, gl, gb, *, batch, seq, n_grp, n_rep, scale, tq):
    bg, nc, d = kcmp.shape
    ns = seq // SLC_BLOCK
    nq = seq // tq
    starts = np.arange(nc) * CMP_STRIDE
    jb = np.arange(ns)
    overlap = ((starts[:, None] < (jb[None, :] + 1) * SLC_BLOCK)
               & (starts[:, None] + CMP_BLOCK > jb[None, :] * SLC_BLOCK)).astype(np.float32)
    v_ext = jnp.concatenate([vcmp, jnp.ones_like(vcmp)], -1)
    tok_spec = pl.BlockSpec((tq, n_rep * d), lambda b, i: ((b // n_grp) * nq + i, b % n_grp))
    return pl.pallas_call(
        functools.partial(_cmp_attn_kernel, n_rep=n_rep, tq=tq, c_exp=scale * LOG2E),
        grid=(bg, nq),
        in_specs=[tok_spec,
                  pl.BlockSpec((1, nc, d), lambda b, i: (b, 0, 0)),
                  pl.BlockSpec((1, nc, 2 * d), lambda b, i: (b, 0, 0)),
                  pl.BlockSpec((ns, nc), lambda b, i: (0, 0)),
                  pl.BlockSpec((tq, LANE), lambda b, i: ((b // n_grp) * nq + i, b % n_grp)),
                  pl.BlockSpec((1, LANE), lambda b, i: (0, b % n_grp))],
        out_specs=[tok_spec, pl.BlockSpec((1, tq, ns), lambda b, i: (b, i, 0))],
        out_shape=[jax.ShapeDtypeStruct((batch * seq, n_grp * n_rep * d), BF),
                   jax.ShapeDtypeStruct((bg, seq, ns), F32)],
        compiler_params=_params("parallel", "parallel"),
        name="nsa_cmp_attn",
    )(z, kcmp, v_ext, jnp.asarray(overlap.T, BF), gl, gb)


def _rope_tables(seq, dim):
    inv = 1.0 / (ROPE_THETA ** (jnp.arange(0, dim, 2, dtype=F32) / dim))
    ang = jnp.arange(seq, dtype=F32)[:, None] * inv[None, :]
    cos_h = jnp.concatenate([jnp.cos(ang), jnp.cos(ang)], -1)
    sin_h = jnp.concatenate([-jnp.sin(ang), jnp.sin(ang)], -1)
    return jnp.tile(cos_h, (1, LANE // dim)), jnp.tile(sin_h, (1, LANE // dim))


def _pad_cols(w, n):
    return jnp.pad(w, ((0, 0), (0, n - w.shape[1])))


def _even_mixer(x16, batch, seq, w_in, q_norm, w_uq, kv_norm, w_ukv, sinks):
    H, dn, dr, dv = MLA_HEADS, MLA_NOPE, MLA_ROPE, MLA_V
    G, R, ds = SWA_KV_HEADS, SWA_HEADS // SWA_KV_HEADS, SWA_HEAD_DIM
    n_lat = MLA_Q_RANK + MLA_KV_RANK
    o_qs = n_lat + dr
    o_ks = o_qs + SWA_HEADS * ds
    o_vs = o_ks + G * ds
    w_kpe = w_in[:, n_lat:o_qs]
    cols = [w_in[:, o_qs:o_ks]]
    for g in range(G):
        cols += [w_in[:, o_ks + g * ds:o_ks + (g + 1) * ds], w_in[:, o_vs + g * ds:o_vs + (g + 1) * ds]]
    cols += [w_kpe, w_kpe]
    w_rg = jnp.concatenate(cols, -1)
    n_q = SWA_HEADS * ds
    cos, sin = _rope_tables(seq, ds)
    groups = "R" * (n_q // LANE) + "L" * G + "R"
    z, z_lat = _proj(x16, jnp.concatenate([w_rg, w_in[:, :n_lat]], -1).astype(BF),
                     rope=(cos, sin, ds // 2, groups), seq=seq, out_dtype=BF, n_tail=n_lat, name="even_in")

    w_uq_p = jnp.concatenate([w_uq.reshape(-1, H, dn + dr)[:, :, :dn].reshape(-1, H * dn),
                              w_uq.reshape(-1, H, dn + dr)[:, :, dn:].reshape(-1, H * dr)], -1)
    w_ukv_p = jnp.concatenate([w_ukv.reshape(-1, H, dn + dv)[:, :, :dn].reshape(-1, H * dn),
                               w_ukv.reshape(-1, H, dn + dv)[:, :, dn:].reshape(-1, H * dv)], -1)
    cos_q, sin_q = _rope_tables(seq, dr)
    q = _proj(z_lat, w_uq_p.astype(BF), k_block=0, k_size=MLA_Q_RANK, rms_g=q_norm,
              rope=(cos_q, sin_q, dr // 2, "N" * (H * dn // LANE) + "R" * (H * dr // LANE)), seq=seq, out_dtype=BF,
              name="mla_q_up")
    kv = _proj(z_lat, w_ukv_p.astype(BF), k_block=1, k_size=MLA_KV_RANK, rms_g=kv_norm, out_dtype=BF,
               name="mla_kv_up")
    tq = _pick(seq, (512, 256, 128))
    tk = _pick(seq, (512, 256, 128))
    kpe_blk = (n_q + 2 * G * ds) // LANE
    hp = MLA_HEADS_PER_STEP
    n_hg = H // hp
    o_mla = _attn_call(
        "mla", [q, q, kv, z, kv],
        [_spec((tq, hp * dn), "q", lambda b: b % n_hg),
         _spec((tq, hp * dr), "q", lambda b: H * dn // (hp * dr) + b % n_hg),
         _spec((tk, hp * dn), "k", lambda b: b % n_hg),
         _spec((tk, LANE), "k", lambda b: kpe_blk),
         _spec((tk, hp * dv), "k", lambda b: H * dn // (hp * dv) + b % n_hg)],
        batch=batch, seq=seq, n_grp=n_hg, n_rep=hp, dq=2 * LANE, dv=dv, scale=(dn + dr) ** -0.5, window=seq,
        tq=tq, tk=tk, unit_rows=4 * UNIT_ROWS, name="mla_attn")

    tq = _pick(seq, (256, 128))
    o_swa = _attn_call(
        "swa", [z, z],
        [_spec((tq, R * ds), "q", lambda b: b % G),
         _spec((tq, LANE), "k", lambda b: n_q // LANE + b % G)],
        batch=batch, seq=seq, n_grp=G, n_rep=R, dq=LANE, dv=ds, scale=ds ** -0.5, window=SWA_WINDOW,
        tq=tq, tk=tq, sinks=sinks.reshape(G, R), name="swa_attn")
    return [o_mla, o_swa]


def _odd_mixer(x16, batch, seq, w_in, gate_b, cmp_pos_k, cmp_pos_v, k_w1, k_w2, v_w1, v_w2):
    G, d = NSA_KV_GROUPS, NSA_HEAD_DIM
    H = NSA_HEADS
    R = H // G
    kvw = G * d
    o_kc = H * d
    offs = [o_kc + i * kvw for i in range(7)]
    cos, sin = _rope_tables(seq, d)
    groups = "R" * H + ("R" * G + "N" * G) * 3
    w_gl = jnp.concatenate([_pad_cols(w_in[:, offs[6] + g * 3 * R:offs[6] + (g + 1) * 3 * R], LANE)
                            for g in range(G)], -1)
    cmp_groups = [offs[0] // d + g for g in range(G)] + [offs[1] // d + g for g in range(G)]
    z, gl, zc = _proj(x16, jnp.concatenate([w_in[:, :offs[6]], w_gl], -1).astype(BF),
                      rope=(cos, sin, d // 2, groups), seq=seq, out_dtype=BF, n_tail=G * LANE, chunked=cmp_groups,
                      name="odd_in")
    gb = jnp.concatenate([jnp.pad(gate_b[g * 3 * R:(g + 1) * 3 * R].astype(F32), (0, LANE - 3 * R))
                          for g in range(G)]).reshape(1, G * LANE)

    scale = d ** -0.5
    kcmp = _compress(zc, 0, cmp_pos_k, k_w1, k_w2, batch=batch, seq=seq, n_grp=G, name="nsa_compress_k")
    vcmp = _compress(zc, G, cmp_pos_v, v_w1, v_w2, batch=batch, seq=seq, n_grp=G, name="nsa_compress_v")
    tq = _pick(seq, (256, 128))
    o, sel = _cmp_attention(z, kcmp, vcmp, gl, gb, batch=batch, seq=seq, n_grp=G, n_rep=R, scale=scale, tq=tq)

    def branch(k_off, v_off, **kw):
        return _attn_call(
            "gqa", [z, z, z],
            [_spec((kw["tq"], R * d), "q", lambda b: b % G),
             _spec((kw["tk"], d), "k", lambda b: k_off // d + b % G),
             _spec((kw["tk"], d), "k", lambda b: v_off // d + b % G)],
            batch=batch, seq=seq, n_grp=G, n_rep=R, dq=d, dv=d, scale=scale, **kw)

    o = branch(offs[2], offs[3], window=seq, tq=_pick(seq, (512, 256, 128)), tk=_pick(seq, (512, 256, 128)),
               sel=sel, gate=(gl, gb, 1), add=o, name="nsa_slc_attn")
    o = _band_call(z, 0, offs[4], offs[5], batch=batch, seq=seq, n_grp=G, n_rep=R, d=d, scale=scale,
                   window=NSA_WINDOW, t=tq, gate=(gl, gb, 2), add=o, name="nsa_win_attn")
    return [o]


def kernel(x, ln_g, ln_b, ffn_w_gate, ffn_w_up, ffn_w_down, even_w_in, mla_q_norm, mla_w_uq, mla_kv_norm, mla_w_ukv, swa_sinks, even_w_out, odd_w_in, nsa_gate_b, nsa_cmp_pos_k, nsa_cmp_pos_v, nsa_cmp_k_w1, nsa_cmp_k_w2, nsa_cmp_v_w1, nsa_cmp_v_w2, odd_w_out):
    batch, seq, dm = x.shape
    depth = ln_g.shape[0]
    alpha = float((2 * depth) ** 0.25)
    x32 = x.reshape(batch * seq, dm)
    x16 = x32.astype(BF)
    even_out16 = even_w_out.astype(BF)
    odd_out16 = odd_w_out.astype(BF)

    def ffn(x32, x16, l, idx):
        h, w_down16 = _ffn_up(x16, ffn_w_gate, ffn_w_up, ffn_w_down, l, idx)
        return _mm_res_ln([h], w_down16, (), x32, ln_g[l, 2 * idx], ln_b[l, 2 * idx], alpha=alpha, coef=0.5,
                          name="ffn_down_ln")

    for l in range(depth):
        x32, x16 = ffn(x32, x16, l, 0)
        j = l // 2
        if l % 2 == 0:
            o = _even_mixer(x16, batch, seq, even_w_in[j], mla_q_norm[j], mla_w_uq[j], mla_kv_norm[j],
                            mla_w_ukv[j], swa_sinks[j])
            w_out = even_out16
        else:
            o = _odd_mixer(x16, batch, seq, odd_w_in[j], nsa_gate_b[j], nsa_cmp_pos_k[j], nsa_cmp_pos_v[j],
                           nsa_cmp_k_w1[j], nsa_cmp_k_w2[j], nsa_cmp_v_w1[j], nsa_cmp_v_w2[j])
            w_out = odd_out16
        x32, x16 = _mm_res_ln(o, w_out, (j,), x32, ln_g[l, 1], ln_b[l, 1], alpha=alpha, coef=1.0,
                              name="mixer_out_ln")
        x32, x16 = ffn(x32, x16, l, 1)
    return x32.reshape(batch, seq, dm)
```

```python
import functools
import math

import numpy as np
import jax
import jax.numpy as jnp
from jax import lax
from jax.experimental import pallas as pl
from jax.experimental.pallas import tpu as pltpu

F32 = jnp.float32
BF = jnp.bfloat16

ROPE_THETA = 10000.0
LN_EPS = 1e-5
RMS_EPS = 1e-6
NEG = -1e30
FORCE_SCORE = 1e9
LOG2E = math.log2(math.e)

MLA_HEADS, MLA_NOPE, MLA_ROPE, MLA_V = 8, 128, 64, 128
MLA_Q_RANK, MLA_KV_RANK = 512, 512
SWA_HEADS, SWA_KV_HEADS, SWA_HEAD_DIM, SWA_WINDOW = 16, 2, 64, 128
NSA_HEADS, NSA_KV_GROUPS, NSA_HEAD_DIM = 16, 2, 128
CMP_BLOCK, CMP_STRIDE, CMP_HIDDEN = 32, 16, 256
SLC_BLOCK, N_SEL, NSA_WINDOW = 64, 8, 512

LANE = 128
UNIT_ROWS = 128
MLA_HEADS_PER_STEP = 8
VMEM_LIMIT_BYTES = 52 * 1024 * 1024


def _params(*sem):
    return pltpu.CompilerParams(dimension_semantics=sem, vmem_limit_bytes=VMEM_LIMIT_BYTES)


def _pick(n, candidates):
    for c in candidates:
        if n % c == 0:
            return c
    raise ValueError(f"no tile in {candidates} divides {n}")


def _lane_tile(x, n):
    return x if n == 1 else jnp.concatenate([x] * n, axis=1)


def _proj_kernel(*refs, has_rms, groups, half, n_tail, chunked):
    it = iter(refs)
    x_ref, w_ref = next(it), next(it)
    g_ref = next(it) if has_rms else None
    cos_ref, sin_ref = (next(it), next(it)) if groups else (None, None)
    o_ref = next(it)
    tail_ref = next(it) if n_tail else None
    chunk_ref, stage = (next(it), next(it)) if chunked else (None, None)
    x = x_ref[...]
    if has_rms:
        xf = x.astype(F32)
        x = xf * lax.rsqrt(jnp.mean(xf * xf, -1, keepdims=True) + RMS_EPS) * g_ref[...]
    acc = jnp.dot(x.astype(BF), w_ref[...], preferred_element_type=F32)
    if n_tail:
        tail_ref[...] = acc[:, acc.shape[1] - n_tail:]
        acc = acc[:, :acc.shape[1] - n_tail]
    if not groups:
        o_ref[...] = acc.astype(o_ref.dtype)
        return
    cos, sin = cos_ref[...], sin_ref[...]
    lane = lax.broadcasted_iota(jnp.int32, cos.shape, 1)
    low = lane < LANE // 2
    cos_l, sin_l = jnp.where(low, cos, 1.0), jnp.where(low, sin, 0.0)
    for c, kind in enumerate(groups):
        a = acc[:, c * LANE:(c + 1) * LANE]
        if kind != "N":
            if 2 * half == LANE:
                partner = pltpu.roll(a, half, 1)
            else:
                partner = jnp.where(lane % (2 * half) < half, pltpu.roll(a, LANE - half, 1), pltpu.roll(a, half, 1))
            a = a * cos + partner * sin if kind == "R" else a * cos_l + partner * sin_l
        o_ref[:, c * LANE:(c + 1) * LANE] = a.astype(o_ref.dtype)
        if c in chunked:
            stage[...] = a
            n_ch = a.shape[0] // CMP_STRIDE
            for l in range(CMP_STRIDE):
                lanes = slice((chunked.index(c) * CMP_STRIDE + l) * LANE, (chunked.index(c) * CMP_STRIDE + l + 1) * LANE)
                chunk_ref[:, lanes] = stage[pl.ds(l, n_ch, stride=CMP_STRIDE), :].astype(chunk_ref.dtype)


def _proj(x, w, *, k_block=0, k_size=None, rms_g=None, rope=None, seq=None, out_dtype=F32, n_tail=0, chunked=(),
          name="proj"):
    m = x.shape[0]
    k_size = x.shape[1] if k_size is None else k_size
    n = w.shape[1]
    n_main = n - n_tail
    out_bytes = jnp.dtype(out_dtype).itemsize
    for tm in (512, 256, 128):
        est = k_size * n * 2 + 2 * tm * k_size * x.dtype.itemsize + 2 * tm * n * out_bytes + 3 * tm * n * 4
        if m % tm == 0 and est <= VMEM_LIMIT_BYTES * 3 // 4:
            break
    in_specs = [pl.BlockSpec((tm, k_size), lambda i: (i, k_block)),
                pl.BlockSpec((k_size, n), lambda i: (0, 0), pipeline_mode=pl.Buffered(1))]
    args = [x, w]
    if rms_g is not None:
        in_specs.append(pl.BlockSpec((1, k_size), lambda i: (0, 0)))
        args.append(rms_g.reshape(1, k_size).astype(F32))
    half, groups = 0, ()
    if rope is not None:
        cos, sin, half, groups = rope
        sb = seq // tm
        in_specs += [pl.BlockSpec((tm, LANE), lambda i: (i % sb, 0))] * 2
        args += [cos, sin]
    out_specs = [pl.BlockSpec((tm, n_main), lambda i: (i, 0))]
    out_shape = [jax.ShapeDtypeStruct((m, n_main), out_dtype)]
    if n_tail:
        out_specs.append(pl.BlockSpec((tm, n_tail), lambda i: (i, 0)))
        out_shape.append(jax.ShapeDtypeStruct((m, n_tail), F32))
    scratch = []
    if chunked:
        width = len(chunked) * CMP_STRIDE * LANE
        out_specs.append(pl.BlockSpec((tm // CMP_STRIDE, width), lambda i: (i, 0)))
        out_shape.append(jax.ShapeDtypeStruct((m // CMP_STRIDE, width), out_dtype))
        scratch.append(pltpu.VMEM((tm, LANE), F32))
    out = pl.pallas_call(
        functools.partial(_proj_kernel, has_rms=rms_g is not None, groups=tuple(groups), half=half, n_tail=n_tail,
                          chunked=tuple(chunked)),
        grid=(m // tm,),
        in_specs=in_specs,
        out_specs=out_specs,
        out_shape=out_shape,
        scratch_shapes=scratch,
        compiler_params=_params("parallel"),
        name=name,
    )(*args)
    return out if len(out) > 1 else out[0]


def _ffn_up_kernel(x_ref, wg_ref, wu_ref, wd_ref, o_ref, wd16_ref, wg16, wu16):
    @pl.when(pl.program_id(1) == 0)
    def _():
        wg16[...] = wg_ref[...].astype(BF)
        wu16[...] = wu_ref[...].astype(BF)

    wd16_ref[...] = wd_ref[...].astype(BF)
    x = x_ref[...]
    g = jnp.dot(x, wg16[...], preferred_element_type=F32)
    u = jnp.dot(x, wu16[...], preferred_element_type=F32)
    o_ref[...] = (g * jax.nn.sigmoid(g) * u).astype(o_ref.dtype)


def _ffn_up(x16, wg_all, wu_all, wd_all, layer, idx):
    m, k = x16.shape
    n = wg_all.shape[-1]
    n_out = wd_all.shape[-1]
    tm = _pick(m, (1024, 512, 256, 128))
    tn = _pick(n, (512, 256, 128))
    ni = m // tm
    slab = n // ((n // tn) * ni)
    assert slab * (n // tn) * ni == n and slab % 16 == 0
    w_spec = pl.BlockSpec((None, None, k, tn), lambda j, i: (layer, idx, 0, j))
    return pl.pallas_call(
        _ffn_up_kernel,
        grid=(n // tn, ni),
        in_specs=[pl.BlockSpec((tm, k), lambda j, i: (i, 0)), w_spec, w_spec,
                  pl.BlockSpec((None, None, slab, n_out), lambda j, i: (layer, idx, j * ni + i, 0))],
        out_specs=[pl.BlockSpec((tm, tn), lambda j, i: (i, j)),
                   pl.BlockSpec((slab, n_out), lambda j, i: (j * ni + i, 0))],
        out_shape=[jax.ShapeDtypeStruct((m, n), BF), jax.ShapeDtypeStruct((n, n_out), BF)],
        scratch_shapes=[pltpu.VMEM((k, tn), BF), pltpu.VMEM((k, tn), BF)],
        compiler_params=_params("parallel", "arbitrary"),
        name="ffn_up",
    )(x16, wg_all, wu_all, wd_all)


def _mm_res_ln_kernel(*refs, n_x, alpha, coef):
    x_refs = refs[:n_x]
    w_ref, res_ref, g_ref, b_ref, o32_ref, o16_ref = refs[n_x:]
    x = x_refs[0][...] if n_x == 1 else jnp.concatenate([r[...] for r in x_refs], axis=1)
    y = alpha * res_ref[...] + coef * jnp.dot(x, w_ref[...], preferred_element_type=F32)
    mu = jnp.mean(y, -1, keepdims=True)
    yc = y - mu
    var = jnp.mean(yc * yc, -1, keepdims=True)
    out = yc * lax.rsqrt(var + LN_EPS) * g_ref[...] + b_ref[...]
    o32_ref[...] = out
    o16_ref[...] = out.astype(BF)


def _mm_res_ln(x_parts, w_all, w_idx, res, g, b, *, alpha, coef, name):
    m = x_parts[0].shape[0]
    k, n = w_all.shape[-2:]
    for tm in (512, 256, 128):
        est = k * n * 2 + 2 * tm * k * 2 + tm * n * (2 * 4 + 2 * 4 + 2 * 2 + 2 * 4)
        if m % tm == 0 and est <= VMEM_LIMIT_BYTES * 7 // 8:
            break
    lead = (None,) * len(w_idx)
    return pl.pallas_call(
        functools.partial(_mm_res_ln_kernel, n_x=len(x_parts), alpha=alpha, coef=coef),
        grid=(m // tm,),
        in_specs=[pl.BlockSpec((tm, xp.shape[1]), lambda i: (i, 0)) for xp in x_parts] + [
                  pl.BlockSpec(lead + (k, n), lambda i: tuple(w_idx) + (0, 0), pipeline_mode=pl.Buffered(1)),
                  pl.BlockSpec((tm, n), lambda i: (i, 0)),
                  pl.BlockSpec((1, n), lambda i: (0, 0)),
                  pl.BlockSpec((1, n), lambda i: (0, 0))],
        out_specs=[pl.BlockSpec((tm, n), lambda i: (i, 0)),
                   pl.BlockSpec((tm, n), lambda i: (i, 0))],
        out_shape=[jax.ShapeDtypeStruct((m, n), F32), jax.ShapeDtypeStruct((m, n), BF)],
        compiler_params=_params("parallel"),
        name=name,
    )(*x_parts, w_all, res, g.reshape(1, n).astype(F32), b.reshape(1, n).astype(F32))


def _attn_kernel(qi_ref, kj_ref, fst_ref, lst_ref, msk_ref, *refs, variant, n_grp, n_rep, tq, tk, rc, dv, c_exp,
                 window, seq, has_sel, has_sink, gate_col, has_add, two_paths):
    it = iter(refs)
    if variant == "mla":
        qn_ref, qr_ref, kn_ref, kp_ref, v_ref = (next(it) for _ in range(5))
    elif variant == "swa":
        q_ref, kv_ref = next(it), next(it)
    else:
        q_ref, k_ref, v_ref = next(it), next(it), next(it)
    sel_ref = next(it) if has_sel else None
    sink_ref = next(it) if has_sink else None
    gl_ref, gb_ref = (next(it), next(it)) if gate_col is not None else (None, None)
    add_ref = next(it) if has_add else None
    o_ref = next(it)
    q_sc, m_sc, acc_sc = next(it), next(it), next(it)
    p = pl.program_id(1)
    qi = qi_ref[p]
    kj = kj_ref[p]
    rows = n_rep * tq

    @pl.when(fst_ref[p] == 1)
    def _():
        if has_sink:
            m_sc[...] = sink_ref[0]
            lane = lax.broadcasted_iota(jnp.int32, acc_sc.shape, 1)
            acc_sc[...] = jnp.where(lane >= dv, 1.0, 0.0)
        else:
            m_sc[...] = jnp.full_like(m_sc, NEG)
            acc_sc[...] = jnp.zeros_like(acc_sc)
        if variant == "mla":
            half_of_lane = lax.broadcasted_iota(jnp.int32, (tq, LANE), 1) // MLA_ROPE
            for r in range(n_rep):
                rope_pair = qr_ref[:, (r // 2) * LANE:(r // 2 + 1) * LANE].astype(F32) * c_exp
                q_sc[r * tq:(r + 1) * tq, :LANE] = (qn_ref[:, r * LANE:(r + 1) * LANE].astype(F32) * c_exp).astype(BF)
                q_sc[r * tq:(r + 1) * tq, LANE:] = jnp.where(half_of_lane == r % 2, rope_pair, 0.0).astype(BF)
        elif variant == "swa":
            low = lax.broadcasted_iota(jnp.int32, (tq, LANE), 1) < dv
            for r in range(n_rep):
                chunk = q_ref[:, (r // 2) * LANE:(r // 2 + 1) * LANE].astype(F32)
                if r % 2 == 1:
                    chunk = pltpu.roll(chunk, dv, 1)
                q_sc[r * tq:(r + 1) * tq, :] = jnp.where(low, chunk * c_exp, 0.0).astype(BF)
        else:
            dq = k_ref.shape[1]
            for r in range(n_rep):
                q_sc[r * tq:(r + 1) * tq, :dq] = (q_ref[:, r * dq:(r + 1) * dq].astype(F32) * c_exp).astype(BF)
            if has_sel:
                n_blk = seq // SLC_BLOCK
                in_use = lax.broadcasted_iota(jnp.int32, (tq, LANE), 1) < n_blk
                unpicked = jnp.where(in_use, (sel_ref[0] - 1.0) * -NEG, 0.0).astype(BF)
                for r in range(n_rep):
                    q_sc[r * tq:(r + 1) * tq, dq:] = unpicked

    ones = jnp.ones((tk, dv), BF)
    if variant == "mla":
        ks = [jnp.concatenate([kn_ref[:, r * LANE:(r + 1) * LANE], kp_ref[...]], axis=1) for r in range(n_rep)]
        vs = [jnp.concatenate([v_ref[:, r * dv:(r + 1) * dv], ones], axis=1) for r in range(n_rep)]
    elif variant == "swa":
        swapped = pltpu.roll(kv_ref[...].astype(F32), dv, 1)
        ks = [kv_ref[...]]
        vs = [jnp.where(lax.broadcasted_iota(jnp.int32, (tk, LANE), 1) < dv, swapped, 1.0).astype(BF)]
    else:
        ks = [k_ref[...]]
        vs = [jnp.concatenate([v_ref[...], ones], axis=1)]

    def kv_of(u):
        return (u * rc) // tq if len(ks) > 1 else 0

    if has_sel:
        blk_of_key = (kj * tk + lax.broadcasted_iota(jnp.int32, (tk, LANE), 0)) // SLC_BLOCK
        own_blk = jnp.where(lax.broadcasted_iota(jnp.int32, (tk, LANE), 1) == blk_of_key, 1.0, 0.0).astype(BF)
        ks = [jnp.concatenate([ks[0], own_blk], axis=1)]

    def logits(u):
        return lax.dot_general(q_sc[u * rc:(u + 1) * rc, :], ks[kv_of(u)], (((1,), (1,)), ((), ())),
                               preferred_element_type=F32)

    def online_softmax(masked):
        if masked:
            rel = (lax.broadcasted_iota(jnp.int32, (rc, tk), 0) - lax.broadcasted_iota(jnp.int32, (rc, tk), 1))
        n_units = rows // rc
        s_next = logits(0)
        for u in range(n_units):
            s = s_next
            if u + 1 < n_units:
                s_next = logits(u + 1)
            us = slice(u * rc, (u + 1) * rc)
            m_prev = m_sc[us, :]
            if masked:
                base = qi * tq - kj * tk + (u * rc) % tq
                visible = rel >= -base
                if window < seq:
                    visible = visible & (rel < window - base)
                s = jnp.where(visible, s, NEG)
            m_new = jnp.maximum(m_prev, jnp.max(s, -1, keepdims=True))
            w = jnp.exp2(s - _lane_tile(m_new, tk // LANE))
            m_sc[us, :] = m_new
            alpha = jnp.exp2(m_prev - m_new)
            acc_sc[us, :] = (_lane_tile(alpha, acc_sc.shape[1] // LANE) * acc_sc[us, :]
                             + jnp.dot(w.astype(BF), vs[kv_of(u)], preferred_element_type=F32))

    if not two_paths:
        online_softmax(True)
    else:
        pl.when(msk_ref[p] == 1)(lambda: online_softmax(True))
        pl.when(msk_ref[p] == 0)(lambda: online_softmax(False))

    @pl.when(lst_ref[p] == 1)
    def _():
        acc = acc_sc[...]
        if dv % LANE == 0:
            o = acc[:, :dv] / acc[:, dv:]
        else:
            o = acc[:, :dv] / acc[:, dv:dv + 1]
        if gate_col is not None:
            gates = jax.nn.sigmoid(gl_ref[...] + gb_ref[...])
        for r in range(n_rep):
            o_r = o[r * tq:(r + 1) * tq]
            if gate_col is not None:
                o_r = o_r * gates[:, 3 * r + gate_col:3 * r + gate_col + 1]
            if has_add:
                o_r = o_r + add_ref[:, r * dv:(r + 1) * dv].astype(F32)
            o_ref[:, r * dv:(r + 1) * dv] = o_r.astype(o_ref.dtype)


def _attn_call(variant, qkv_args, qkv_specs, *, batch, seq, n_grp, n_rep, dq, dv, scale, window, tq, tk,
               sel=None, sinks=None, gate=None, add=None, unit_rows=UNIT_ROWS, name="attn"):
    rows = n_rep * tq
    nq, nk = seq // tq, seq // tk
    pairs, fst, lst, msk = [], [], [], []
    for i in range(nq):
        lo = max(0, i * tq - (window - 1)) // tk
        hi = (i * tq + tq - 1) // tk
        for j in range(lo, hi + 1):
            pairs.append((i, j))
            fst.append(int(j == lo))
            lst.append(int(j == hi))
            inside = i * tq - (j * tk + tk - 1) >= 0 and i * tq + tq - 1 - j * tk < window
            msk.append(int(not inside))
    two_paths = 0 < sum(msk) < len(msk)
    qi = jnp.asarray([a for a, _ in pairs], jnp.int32)
    kj = jnp.asarray([b for _, b in pairs], jnp.int32)

    def q_rows(b, p, qi, kj):
        return (b // n_grp) * nq + qi[p]

    def k_rows(b, p, qi, kj):
        return (b // n_grp) * nk + kj[p]

    in_specs = [mk(q_rows, k_rows) for mk in qkv_specs]
    args = list(qkv_args)
    if sel is not None:
        in_specs.append(pl.BlockSpec((1, tq, sel.shape[-1]), lambda b, p, qi, kj, *_: (b, qi[p], 0)))
        args.append(sel)
    if sinks is not None:
        m0 = jnp.broadcast_to((sinks.astype(F32) * LOG2E)[:, :, None, None], (n_grp, n_rep, tq, LANE))
        in_specs.append(pl.BlockSpec((1, rows, LANE), lambda b, p, qi, kj, *_: (b % n_grp, 0, 0)))
        args.append(m0.reshape(n_grp, rows, LANE))
    gate_col = None
    if gate is not None:
        gl, gb, gate_col = gate
        in_specs += [pl.BlockSpec((tq, LANE), lambda b, p, qi, kj, *_: (q_rows(b, p, qi, kj), b % n_grp)),
                     pl.BlockSpec((1, LANE), lambda b, p, qi, kj, *_: (0, b % n_grp))]
        args += [gl, gb]
    out_spec = pl.BlockSpec((tq, n_rep * dv), lambda b, p, qi, kj, *_: (q_rows(b, p, qi, kj), b % n_grp))
    if add is not None:
        in_specs.append(out_spec)
        args.append(add)
    rc = min(tq, unit_rows)
    kern = functools.partial(_attn_kernel, variant=variant, n_grp=n_grp, n_rep=n_rep, tq=tq, tk=tk, rc=rc, dv=dv,
                             c_exp=scale * LOG2E, window=window, seq=seq, has_sel=sel is not None,
                             has_sink=sinks is not None, gate_col=gate_col, has_add=add is not None,
                             two_paths=two_paths)
    return pl.pallas_call(
        kern,
        grid_spec=pltpu.PrefetchScalarGridSpec(
            num_scalar_prefetch=5,
            grid=(batch * n_grp, len(pairs)),
            in_specs=in_specs,
            out_specs=out_spec,
            scratch_shapes=[pltpu.VMEM((rows, dq), BF), pltpu.VMEM((rows, LANE), F32),
                            pltpu.VMEM((rows, 2 * dv), F32)]),
        out_shape=jax.ShapeDtypeStruct((batch * seq, n_grp * n_rep * dv), BF),
        compiler_params=_params("parallel", "arbitrary"),
        name=name,
    )(qi, kj, jnp.asarray(fst, jnp.int32), jnp.asarray(lst, jnp.int32), jnp.asarray(msk, jnp.int32), *args)


def _spec(shape, rows_of, col_fn):
    def make(q_rows, k_rows):
        rf = q_rows if rows_of == "q" else k_rows
        return pl.BlockSpec(shape, lambda b, p, qi, kj, *_: (rf(b, p, qi, kj), col_fn(b)))
    return make


def _band_kernel(*refs, n_rep, t, n_kt, rc, dv, c_exp, window, gate_col, has_add):
    it = iter(refs)
    q_ref = next(it)
    k_refs = [next(it) for _ in range(n_kt)]
    v_refs = [next(it) for _ in range(n_kt)]
    gl_ref, gb_ref = (next(it), next(it)) if gate_col is not None else (None, None)
    add_ref = next(it) if has_add else None
    o_ref = next(it)
    i = pl.program_id(1)
    n_keys = n_kt * t
    k = jnp.concatenate([r[...] for r in k_refs], axis=0)
    d = k.shape[1]
    v_ext = jnp.concatenate([jnp.concatenate([r[...] for r in v_refs], axis=0), jnp.ones((n_keys, dv), BF)], axis=1)
    col = lax.broadcasted_iota(jnp.int32, (rc, n_keys), 1)
    rel = lax.broadcasted_iota(jnp.int32, (rc, n_keys), 0) - col
    in_seq = col >= (n_kt - 1 - i) * t
    if gate_col is not None:
        gates = jax.nn.sigmoid(gl_ref[...] + gb_ref[...])

    def q_block(u):
        r, i0 = (u * rc) // t, (u * rc) % t
        return (q_ref[i0:i0 + rc, r * d:(r + 1) * d].astype(F32) * c_exp).astype(BF)

    def logits(u):
        return lax.dot_general(q_block(u), k, (((1,), (1,)), ((), ())), preferred_element_type=F32)

    n_units = n_rep * t // rc
    s_next = logits(0)
    for u in range(n_units):
        s = s_next
        if u + 1 < n_units:
            s_next = logits(u + 1)
        r, i0 = (u * rc) // t, (u * rc) % t
        base = (n_kt - 1) * t + i0
        visible = (rel >= -base) & (rel < window - base) & in_seq
        s = jnp.where(visible, s, NEG)
        w = jnp.exp2(s - jnp.max(s, -1, keepdims=True))
        ext = jnp.dot(w.astype(BF), v_ext, preferred_element_type=F32)
        o_u = ext[:, :dv] / ext[:, dv:]
        if gate_col is not None:
            o_u = o_u * gates[i0:i0 + rc, 3 * r + gate_col:3 * r + gate_col + 1]
        if has_add:
            o_u = o_u + add_ref[i0:i0 + rc, r * dv:(r + 1) * dv].astype(F32)
        o_ref[i0:i0 + rc, r * dv:(r + 1) * dv] = o_u.astype(o_ref.dtype)


def _band_call(z, q_col, k_col, v_col, *, batch, seq, n_grp, n_rep, d, scale, window, t, gate=None, add=None,
               name="band_attn"):
    dv = d
    n_kt = -(-(window - 1) // t) + 1
    nq = seq // t
    rc = min(t, UNIT_ROWS)

    def q_row(b, i):
        return (b // n_grp) * nq + i

    in_specs = [pl.BlockSpec((t, n_rep * d), lambda b, i: (q_row(b, i), q_col // (n_rep * d) + b % n_grp))]
    args = [z]
    for col in (k_col, v_col):
        for j in range(n_kt):
            in_specs.append(pl.BlockSpec(
                (t, d), lambda b, i, j=j, col=col: ((b // n_grp) * nq + jnp.maximum(i - (n_kt - 1) + j, 0),
                                                     col // d + b % n_grp)))
            args.append(z)
    gate_col = None
    if gate is not None:
        gl, gb, gate_col = gate
        in_specs += [pl.BlockSpec((t, LANE), lambda b, i: (q_row(b, i), b % n_grp)),
                     pl.BlockSpec((1, LANE), lambda b, i: (0, b % n_grp))]
        args += [gl, gb]
    out_spec = pl.BlockSpec((t, n_rep * dv), lambda b, i: (q_row(b, i), b % n_grp))
    if add is not None:
        in_specs.append(out_spec)
        args.append(add)
    kern = functools.partial(_band_kernel, n_rep=n_rep, t=t, n_kt=n_kt, rc=rc, dv=dv, c_exp=scale * LOG2E,
                             window=window, gate_col=gate_col, has_add=add is not None)
    return pl.pallas_call(
        kern,
        grid=(batch * n_grp, nq),
        in_specs=in_specs,
        out_specs=out_spec,
        out_shape=jax.ShapeDtypeStruct((batch * seq, n_grp * n_rep * dv), BF),
        compiler_params=_params("parallel", "parallel"),
        name=name,
    )(*args)


def _compress_kernel(x_ref, pa_ref, pb_ref, w1a_ref, w1b_ref, w2_ref, o_ref):
    x = x_ref[...].astype(F32)
    a = jnp.dot((x + pa_ref[...]).astype(BF), w1a_ref[...], preferred_element_type=F32)
    b = jnp.dot((x + pb_ref[...]).astype(BF), w1b_ref[...], preferred_element_type=F32)
    h = jax.nn.gelu(a + pltpu.roll(b, b.shape[0] - 1, 0))
    o_ref[0] = jnp.dot(h.astype(BF), w2_ref[...], preferred_element_type=F32).astype(o_ref.dtype)


def _compress(zc, first_blk, pos, w1, w2, *, batch, seq, n_grp, name):
    d = w2.shape[1]
    hid = w1.shape[1]
    nch = seq // CMP_STRIDE
    width = CMP_STRIDE * d
    pa = pos[:CMP_STRIDE].reshape(1, width).astype(F32)
    pb = pos[CMP_STRIDE:].reshape(1, width).astype(F32)
    w1 = w1.astype(BF)
    return pl.pallas_call(
        _compress_kernel,
        grid=(batch * n_grp,),
        in_specs=[pl.BlockSpec((nch, width), lambda b: (b // n_grp, first_blk + b % n_grp)),
                  pl.BlockSpec((1, width), lambda b: (0, 0)),
                  pl.BlockSpec((1, width), lambda b: (0, 0)),
                  pl.BlockSpec((width, hid), lambda b: (0, 0)),
                  pl.BlockSpec((width, hid), lambda b: (1, 0)),
                  pl.BlockSpec((hid, d), lambda b: (0, 0))],
        out_specs=pl.BlockSpec((1, nch, d), lambda b: (b, 0, 0)),
        out_shape=jax.ShapeDtypeStruct((batch * n_grp, nch, d), BF),
        compiler_params=_params("parallel"),
        name=name,
    )(zc, pa, pb, w1, w1, w2.astype(BF))


def _cmp_attn_kernel(q_ref, k_ref, v_ref, ovt_ref, gl_ref, gb_ref, o_ref, sel_ref, *, n_rep, tq, c_exp):
    qi = pl.program_id(1)
    nc, d = k_ref.shape[1], k_ref.shape[2]
    rows = n_rep * tq
    q = jnp.concatenate([q_ref[:, r * d:(r + 1) * d] for r in range(n_rep)], axis=0)
    s = lax.dot_general(q, k_ref[0], (((1,), (1,)), ((), ())), preferred_element_type=F32)
    qpos = qi * tq + lax.broadcasted_iota(jnp.int32, (tq, nc), 0)
    cmp_end = lax.broadcasted_iota(jnp.int32, (tq, nc), 1) * CMP_STRIDE + (CMP_BLOCK - 1)
    cmask = (cmp_end <= qpos)[None]
    s = jnp.where(cmask, s.reshape(n_rep, tq, nc), NEG)
    e = jnp.exp2((s - jnp.max(s, -1, keepdims=True)) * c_exp)
    e = jnp.where(cmask, e, 0.0).reshape(rows, nc)
    ext = jnp.dot(e.astype(BF), v_ref[0], preferred_element_type=F32)
    den = ext[:, d:]
    inv = 1.0 / jnp.where(den > 0.0, den, 1.0)
    gates = jax.nn.sigmoid(gl_ref[...] + gb_ref[...])
    o = ext[:, :d] * inv
    for r in range(n_rep):
        o_ref[:, r * d:(r + 1) * d] = (o[r * tq:(r + 1) * tq] * gates[:, 3 * r:3 * r + 1]).astype(o_ref.dtype)
    inv_c = inv[:, :nc] if nc <= d else _lane_tile(inv, nc // d)
    prob = e * inv_c
    psum = jnp.sum(prob.reshape(n_rep, tq, nc), axis=0)
    p_hi = psum.astype(BF)
    p_lo = (psum - p_hi.astype(F32)).astype(BF)
    contract_last = (((1,), (1,)), ((), ()))
    imp = (lax.dot_general(ovt_ref[...], p_hi, contract_last, preferred_element_type=F32)
           + lax.dot_general(ovt_ref[...], p_lo, contract_last, preferred_element_type=F32))
    ns = imp.shape[0]
    blk = lax.broadcasted_iota(jnp.int32, (ns, tq), 0)
    cur = (qi * tq + lax.broadcasted_iota(jnp.int32, (ns, tq), 1)) // SLC_BLOCK
    eligible = blk <= cur
    forced = (blk == 0) | (blk == cur) | (blk == cur - 1)
    score = jnp.where(eligible, jnp.where(forced, FORCE_SCORE, imp), -1.0)
    beaten_by = jnp.zeros((ns, tq), F32)
    for kk in range(ns):
        sk = score[kk:kk + 1, :]
        beats = (sk > score) | ((sk == score) & (blk > kk))
        beaten_by = beaten_by + jnp.where(beats, 1.0, 0.0)
    n_sel = min(N_SEL, ns)
    sel_t = jnp.where((beaten_by < n_sel) & (score >= 0.0), 1.0, 0.0)
    sel_sq = jnp.concatenate([sel_t, jnp.zeros((tq - ns, tq), F32)], axis=0) if ns < tq else sel_t
    sel_ref[0] = sel_sq.T[:, :LANE]


def _cmp_attention(z, kcmp, vcmp, gl, gb, *, batch, seq, n_grp, n_rep, scale, tq):
    bg, nc, d = kcmp.shape
    ns = seq // SLC_BLOCK
    assert ns <= LANE <= tq
    nq = seq // tq
    starts = np.arange(nc) * CMP_STRIDE
    jb = np.arange(ns)
    overlap = ((starts[:, None] < (jb[None, :] + 1) * SLC_BLOCK)
               & (starts[:, None] + CMP_BLOCK > jb[None, :] * SLC_BLOCK)).astype(np.float32)
    v_ext = jnp.concatenate([vcmp, jnp.ones_like(vcmp)], -1)
    tok_spec = pl.BlockSpec((tq, n_rep * d), lambda b, i: ((b // n_grp) * nq + i, b % n_grp))
    return pl.pallas_call(
        functools.partial(_cmp_attn_kernel, n_rep=n_rep, tq=tq, c_exp=scale * LOG2E),
        grid=(bg, nq),
        in_specs=[tok_spec,
                  pl.BlockSpec((1, nc, d), lambda b, i: (b, 0, 0)),
                  pl.BlockSpec((1, nc, 2 * d), lambda b, i: (b, 0, 0)),
                  pl.BlockSpec((ns, nc), lambda b, i: (0, 0)),
                  pl.BlockSpec((tq, LANE), lambda b, i: ((b // n_grp) * nq + i, b % n_grp)),
                  pl.BlockSpec((1, LANE), lambda b, i: (0, b % n_grp))],
        out_specs=[tok_spec, pl.BlockSpec((1, tq, LANE), lambda b, i: (b, i, 0))],
        out_shape=[jax.ShapeDtypeStruct((batch * seq, n_grp * n_rep * d), BF),
                   jax.ShapeDtypeStruct((bg, seq, LANE), F32)],
        compiler_params=_params("parallel", "parallel"),
        name="nsa_cmp_attn",
    )(z, kcmp, v_ext, jnp.asarray(overlap.T, BF), gl, gb)


def _rope_tables(seq, dim):
    inv = 1.0 / (ROPE_THETA ** (jnp.arange(0, dim, 2, dtype=F32) / dim))
    ang = jnp.arange(seq, dtype=F32)[:, None] * inv[None, :]
    cos_h = jnp.concatenate([jnp.cos(ang), jnp.cos(ang)], -1)
    sin_h = jnp.concatenate([-jnp.sin(ang), jnp.sin(ang)], -1)
    return jnp.tile(cos_h, (1, LANE // dim)), jnp.tile(sin_h, (1, LANE // dim))


def _pad_cols(w, n):
    return jnp.pad(w, ((0, 0), (0, n - w.shape[1])))


def _even_mixer(x16, batch, seq, w_in, q_norm, w_uq, kv_norm, w_ukv, sinks):
    H, dn, dr, dv = MLA_HEADS, MLA_NOPE, MLA_ROPE, MLA_V
    G, R, ds = SWA_KV_HEADS, SWA_HEADS // SWA_KV_HEADS, SWA_HEAD_DIM
    n_lat = MLA_Q_RANK + MLA_KV_RANK
    o_qs = n_lat + dr
    o_ks = o_qs + SWA_HEADS * ds
    o_vs = o_ks + G * ds
    w_kpe = w_in[:, n_lat:o_qs]
    cols = [w_in[:, o_qs:o_ks]]
    for g in range(G):
        cols += [w_in[:, o_ks + g * ds:o_ks + (g + 1) * ds], w_in[:, o_vs + g * ds:o_vs + (g + 1) * ds]]
    cols += [w_kpe, w_kpe]
    w_rg = jnp.concatenate(cols, -1)
    n_q = SWA_HEADS * ds
    cos, sin = _rope_tables(seq, ds)
    groups = "R" * (n_q // LANE) + "L" * G + "R"
    z, z_lat = _proj(x16, jnp.concatenate([w_rg, w_in[:, :n_lat]], -1).astype(BF),
                     rope=(cos, sin, ds // 2, groups), seq=seq, out_dtype=BF, n_tail=n_lat, name="even_in")

    w_uq_p = jnp.concatenate([w_uq.reshape(-1, H, dn + dr)[:, :, :dn].reshape(-1, H * dn),
                              w_uq.reshape(-1, H, dn + dr)[:, :, dn:].reshape(-1, H * dr)], -1)
    w_ukv_p = jnp.concatenate([w_ukv.reshape(-1, H, dn + dv)[:, :, :dn].reshape(-1, H * dn),
                               w_ukv.reshape(-1, H, dn + dv)[:, :, dn:].reshape(-1, H * dv)], -1)
    cos_q, sin_q = _rope_tables(seq, dr)
    q = _proj(z_lat, w_uq_p.astype(BF), k_block=0, k_size=MLA_Q_RANK, rms_g=q_norm,
              rope=(cos_q, sin_q, dr // 2, "N" * (H * dn // LANE) + "R" * (H * dr // LANE)), seq=seq, out_dtype=BF,
              name="mla_q_up")
    kv = _proj(z_lat, w_ukv_p.astype(BF), k_block=1, k_size=MLA_KV_RANK, rms_g=kv_norm, out_dtype=BF,
               name="mla_kv_up")
    tq = _pick(seq, (512, 256, 128))
    tk = _pick(seq, (512, 256, 128))
    kpe_blk = (n_q + 2 * G * ds) // LANE
    hp = MLA_HEADS_PER_STEP
    n_hg = H // hp
    o_mla = _attn_call(
        "mla", [q, q, kv, z, kv],
        [_spec((tq, hp * dn), "q", lambda b: b % n_hg),
         _spec((tq, hp * dr), "q", lambda b: H * dn // (hp * dr) + b % n_hg),
         _spec((tk, hp * dn), "k", lambda b: b % n_hg),
         _spec((tk, LANE), "k", lambda b: kpe_blk),
         _spec((tk, hp * dv), "k", lambda b: H * dn // (hp * dv) + b % n_hg)],
        batch=batch, seq=seq, n_grp=n_hg, n_rep=hp, dq=2 * LANE, dv=dv, scale=(dn + dr) ** -0.5, window=seq,
        tq=tq, tk=tk, unit_rows=4 * UNIT_ROWS, name="mla_attn")

    tq = _pick(seq, (256, 128))
    o_swa = _attn_call(
        "swa", [z, z],
        [_spec((tq, R * ds), "q", lambda b: b % G),
         _spec((tq, LANE), "k", lambda b: n_q // LANE + b % G)],
        batch=batch, seq=seq, n_grp=G, n_rep=R, dq=LANE, dv=ds, scale=ds ** -0.5, window=SWA_WINDOW,
        tq=tq, tk=tq, sinks=sinks.reshape(G, R), name="swa_attn")
    return [o_mla, o_swa]


def _odd_mixer(x16, batch, seq, w_in, gate_b, cmp_pos_k, cmp_pos_v, k_w1, k_w2, v_w1, v_w2):
    G, d = NSA_KV_GROUPS, NSA_HEAD_DIM
    H = NSA_HEADS
    R = H // G
    kvw = G * d
    o_kc = H * d
    offs = [o_kc + i * kvw for i in range(7)]
    cos, sin = _rope_tables(seq, d)
    groups = "R" * H + ("R" * G + "N" * G) * 3
    w_gl = jnp.concatenate([_pad_cols(w_in[:, offs[6] + g * 3 * R:offs[6] + (g + 1) * 3 * R], LANE)
                            for g in range(G)], -1)
    cmp_groups = [offs[0] // d + g for g in range(G)] + [offs[1] // d + g for g in range(G)]
    z, gl, zc = _proj(x16, jnp.concatenate([w_in[:, :offs[6]], w_gl], -1).astype(BF),
                      rope=(cos, sin, d // 2, groups), seq=seq, out_dtype=BF, n_tail=G * LANE, chunked=cmp_groups,
                      name="odd_in")
    gb = jnp.concatenate([jnp.pad(gate_b[g * 3 * R:(g + 1) * 3 * R].astype(F32), (0, LANE - 3 * R))
                          for g in range(G)]).reshape(1, G * LANE)

    scale = d ** -0.5
    kcmp = _compress(zc, 0, cmp_pos_k, k_w1, k_w2, batch=batch, seq=seq, n_grp=G, name="nsa_compress_k")
    vcmp = _compress(zc, G, cmp_pos_v, v_w1, v_w2, batch=batch, seq=seq, n_grp=G, name="nsa_compress_v")
    tq = _pick(seq, (256, 128))
    o, sel = _cmp_attention(z, kcmp, vcmp, gl, gb, batch=batch, seq=seq, n_grp=G, n_rep=R, scale=scale, tq=tq)

    def branch(k_off, v_off, **kw):
        return _attn_call(
            "gqa", [z, z, z],
            [_spec((kw["tq"], R * d), "q", lambda b: b % G),
             _spec((kw["tk"], d), "k", lambda b: k_off // d + b % G),
             _spec((kw["tk"], d), "k", lambda b: v_off // d + b % G)],
            batch=batch, seq=seq, n_grp=G, n_rep=R, dq=2 * d, dv=d, scale=scale, **kw)

    o = branch(offs[2], offs[3], window=seq, tq=_pick(seq, (512, 256, 128)), tk=_pick(seq, (512, 256, 128)),
               sel=sel, gate=(gl, gb, 1), add=o, unit_rows=4 * UNIT_ROWS, name="nsa_slc_attn")
    o = _band_call(z, 0, offs[4], offs[5], batch=batch, seq=seq, n_grp=G, n_rep=R, d=d, scale=scale,
                   window=NSA_WINDOW, t=tq, gate=(gl, gb, 2), add=o, name="nsa_win_attn")
    return [o]


def kernel(x, ln_g, ln_b, ffn_w_gate, ffn_w_up, ffn_w_down, even_w_in, mla_q_norm, mla_w_uq, mla_kv_norm, mla_w_ukv, swa_sinks, even_w_out, odd_w_in, nsa_gate_b, nsa_cmp_pos_k, nsa_cmp_pos_v, nsa_cmp_k_w1, nsa_cmp_k_w2, nsa_cmp_v_w1, nsa_cmp_v_w2, odd_w_out):
    batch, seq, dm = x.shape
    depth = ln_g.shape[0]
    alpha = float((2 * depth) ** 0.25)
    x32 = x.reshape(batch * seq, dm)
    x16 = x32.astype(BF)
    even_out16 = even_w_out.astype(BF)
    odd_out16 = odd_w_out.astype(BF)

    def ffn(x32, x16, l, idx):
        h, w_down16 = _ffn_up(x16, ffn_w_gate, ffn_w_up, ffn_w_down, l, idx)
        return _mm_res_ln([h], w_down16, (), x32, ln_g[l, 2 * idx], ln_b[l, 2 * idx], alpha=alpha, coef=0.5,
                          name="ffn_down_ln")

    for l in range(depth):
        x32, x16 = ffn(x32, x16, l, 0)
        j = l // 2
        if l % 2 == 0:
            o = _even_mixer(x16, batch, seq, even_w_in[j], mla_q_norm[j], mla_w_uq[j], mla_kv_norm[j],
                            mla_w_ukv[j], swa_sinks[j])
            w_out = even_out16
        else:
            o = _odd_mixer(x16, batch, seq, odd_w_in[j], nsa_gate_b[j], nsa_cmp_pos_k[j], nsa_cmp_pos_v[j],
                           nsa_cmp_k_w1[j], nsa_cmp_k_w2[j], nsa_cmp_v_w1[j], nsa_cmp_v_w2[j])
            w_out = odd_out16
        x32, x16 = _mm_res_ln(o, w_out, (j,), x32, ln_g[l, 1], ln_b[l, 1], alpha=alpha, coef=1.0,
                              name="mixer_out_ln")
        x32, x16 = ffn(x32, x16, l, 1)
    return x32.reshape(batch, seq, dm)
```

```python
import functools
import math

import numpy as np
import jax
import jax.numpy as jnp
from jax import lax
from jax.experimental import pallas as pl
from jax.experimental.pallas import tpu as pltpu

F32 = jnp.float32
BF = jnp.bfloat16

ROPE_THETA = 10000.0
LN_EPS = 1e-5
RMS_EPS = 1e-6
NEG = -1e30
FORCE_SCORE = 1e9
LOG2E = math.log2(math.e)

MLA_HEADS, MLA_NOPE, MLA_ROPE, MLA_V = 8, 128, 64, 128
MLA_Q_RANK, MLA_KV_RANK = 512, 512
SWA_HEADS, SWA_KV_HEADS, SWA_HEAD_DIM, SWA_WINDOW = 16, 2, 64, 128
NSA_HEADS, NSA_KV_GROUPS, NSA_HEAD_DIM = 16, 2, 128
CMP_BLOCK, CMP_STRIDE, CMP_HIDDEN = 32, 16, 256
SLC_BLOCK, N_SEL, NSA_WINDOW = 64, 8, 512

LANE = 128
UNIT_ROWS = 128
MLA_HEADS_PER_STEP = 8
VMEM_LIMIT_BYTES = 52 * 1024 * 1024


def _params(*sem):
    return pltpu.CompilerParams(dimension_semantics=sem, vmem_limit_bytes=VMEM_LIMIT_BYTES)


def _pick(n, candidates):
    for c in candidates:
        if n % c == 0:
            return c
    raise ValueError(f"no tile in {candidates} divides {n}")


def _lane_tile(x, n):
    return x if n == 1 else jnp.concatenate([x] * n, axis=1)


def _proj_kernel(*refs, has_rms, groups, half, n_tail, chunked):
    it = iter(refs)
    x_ref, w_ref = next(it), next(it)
    g_ref = next(it) if has_rms else None
    cos_ref, sin_ref = (next(it), next(it)) if groups else (None, None)
    o_ref = next(it)
    tail_ref = next(it) if n_tail else None
    chunk_ref, stage = (next(it), next(it)) if chunked else (None, None)
    x = x_ref[...]
    if has_rms:
        xf = x.astype(F32)
        x = xf * lax.rsqrt(jnp.mean(xf * xf, -1, keepdims=True) + RMS_EPS) * g_ref[...]
    acc = jnp.dot(x.astype(BF), w_ref[...], preferred_element_type=F32)
    if n_tail:
        tail_ref[...] = acc[:, acc.shape[1] - n_tail:]
        acc = acc[:, :acc.shape[1] - n_tail]
    if not groups:
        o_ref[...] = acc.astype(o_ref.dtype)
        return
    cos, sin = cos_ref[...], sin_ref[...]
    lane = lax.broadcasted_iota(jnp.int32, cos.shape, 1)
    low = lane < LANE // 2
    cos_l, sin_l = jnp.where(low, cos, 1.0), jnp.where(low, sin, 0.0)
    for c, kind in enumerate(groups):
        a = acc[:, c * LANE:(c + 1) * LANE]
        if kind != "N":
            if 2 * half == LANE:
                partner = pltpu.roll(a, half, 1)
            else:
                partner = jnp.where(lane % (2 * half) < half, pltpu.roll(a, LANE - half, 1), pltpu.roll(a, half, 1))
            a = a * cos + partner * sin if kind == "R" else a * cos_l + partner * sin_l
        o_ref[:, c * LANE:(c + 1) * LANE] = a.astype(o_ref.dtype)
        if c in chunked:
            stage[...] = a
            n_ch = a.shape[0] // CMP_STRIDE
            for l in range(CMP_STRIDE):
                lanes = slice((chunked.index(c) * CMP_STRIDE + l) * LANE, (chunked.index(c) * CMP_STRIDE + l + 1) * LANE)
                chunk_ref[:, lanes] = stage[pl.ds(l, n_ch, stride=CMP_STRIDE), :].astype(chunk_ref.dtype)


def _proj(x, w, *, k_block=0, k_size=None, rms_g=None, rope=None, seq=None, out_dtype=F32, n_tail=0, chunked=(),
          name="proj"):
    m = x.shape[0]
    k_size = x.shape[1] if k_size is None else k_size
    n = w.shape[1]
    n_main = n - n_tail
    out_bytes = jnp.dtype(out_dtype).itemsize
    for tm in (512, 256, 128):
        est = k_size * n * 2 + 2 * tm * k_size * x.dtype.itemsize + 2 * tm * n * out_bytes + 3 * tm * n * 4
        if m % tm == 0 and est <= VMEM_LIMIT_BYTES * 3 // 4:
            break
    in_specs = [pl.BlockSpec((tm, k_size), lambda i: (i, k_block)),
                pl.BlockSpec((k_size, n), lambda i: (0, 0), pipeline_mode=pl.Buffered(1))]
    args = [x, w]
    if rms_g is not None:
        in_specs.append(pl.BlockSpec((1, k_size), lambda i: (0, 0)))
        args.append(rms_g.reshape(1, k_size).astype(F32))
    half, groups = 0, ()
    if rope is not None:
        cos, sin, half, groups = rope
        sb = seq // tm
        in_specs += [pl.BlockSpec((tm, LANE), lambda i: (i % sb, 0))] * 2
        args += [cos, sin]
    out_specs = [pl.BlockSpec((tm, n_main), lambda i: (i, 0))]
    out_shape = [jax.ShapeDtypeStruct((m, n_main), out_dtype)]
    if n_tail:
        out_specs.append(pl.BlockSpec((tm, n_tail), lambda i: (i, 0)))
        out_shape.append(jax.ShapeDtypeStruct((m, n_tail), F32))
    scratch = []
    if chunked:
        width = len(chunked) * CMP_STRIDE * LANE
        out_specs.append(pl.BlockSpec((tm // CMP_STRIDE, width), lambda i: (i, 0)))
        out_shape.append(jax.ShapeDtypeStruct((m // CMP_STRIDE, width), out_dtype))
        scratch.append(pltpu.VMEM((tm, LANE), F32))
    out = pl.pallas_call(
        functools.partial(_proj_kernel, has_rms=rms_g is not None, groups=tuple(groups), half=half, n_tail=n_tail,
                          chunked=tuple(chunked)),
        grid=(m // tm,),
        in_specs=in_specs,
        out_specs=out_specs,
        out_shape=out_shape,
        scratch_shapes=scratch,
        compiler_params=_params("parallel"),
        name=name,
    )(*args)
    return out if len(out) > 1 else out[0]


def _ffn_up_kernel(x_ref, wg_ref, wu_ref, wd_ref, o_ref, wd16_ref, wg16, wu16):
    @pl.when(pl.program_id(1) == 0)
    def _():
        wg16[...] = wg_ref[...].astype(BF)
        wu16[...] = wu_ref[...].astype(BF)

    wd16_ref[...] = wd_ref[...].astype(BF)
    x = x_ref[...]
    g = jnp.dot(x, wg16[...], preferred_element_type=F32)
    u = jnp.dot(x, wu16[...], preferred_element_type=F32)
    o_ref[...] = (g * jax.nn.sigmoid(g) * u).astype(o_ref.dtype)


def _ffn_up(x16, wg_all, wu_all, wd_all, layer, idx):
    m, k = x16.shape
    n = wg_all.shape[-1]
    n_out = wd_all.shape[-1]
    tm = _pick(m, (1024, 512, 256, 128))
    tn = _pick(n, (512, 256, 128))
    ni = m // tm
    slab = n // ((n // tn) * ni)
    assert slab * (n // tn) * ni == n and slab % 16 == 0
    w_spec = pl.BlockSpec((None, None, k, tn), lambda j, i: (layer, idx, 0, j))
    return pl.pallas_call(
        _ffn_up_kernel,
        grid=(n // tn, ni),
        in_specs=[pl.BlockSpec((tm, k), lambda j, i: (i, 0)), w_spec, w_spec,
                  pl.BlockSpec((None, None, slab, n_out), lambda j, i: (layer, idx, j * ni + i, 0))],
        out_specs=[pl.BlockSpec((tm, tn), lambda j, i: (i, j)),
                   pl.BlockSpec((slab, n_out), lambda j, i: (j * ni + i, 0))],
        out_shape=[jax.ShapeDtypeStruct((m, n), BF), jax.ShapeDtypeStruct((n, n_out), BF)],
        scratch_shapes=[pltpu.VMEM((k, tn), BF), pltpu.VMEM((k, tn), BF)],
        compiler_params=_params("parallel", "arbitrary"),
        name="ffn_up",
    )(x16, wg_all, wu_all, wd_all)


def _mm_res_ln_kernel(*refs, n_x, alpha, coef):
    x_refs = refs[:n_x]
    w_ref, res_ref, g_ref, b_ref, o32_ref, o16_ref = refs[n_x:]
    x = x_refs[0][...] if n_x == 1 else jnp.concatenate([r[...] for r in x_refs], axis=1)
    y = alpha * res_ref[...] + coef * jnp.dot(x, w_ref[...], preferred_element_type=F32)
    mu = jnp.mean(y, -1, keepdims=True)
    yc = y - mu
    var = jnp.mean(yc * yc, -1, keepdims=True)
    out = yc * lax.rsqrt(var + LN_EPS) * g_ref[...] + b_ref[...]
    o32_ref[...] = out
    o16_ref[...] = out.astype(BF)


def _mm_res_ln(x_parts, w_all, w_idx, res, g, b, *, alpha, coef, name):
    m = x_parts[0].shape[0]
    k, n = w_all.shape[-2:]
    for tm in (512, 256, 128):
        est = k * n * 2 + 2 * tm * k * 2 + tm * n * (2 * 4 + 2 * 4 + 2 * 2 + 2 * 4)
        if m % tm == 0 and est <= VMEM_LIMIT_BYTES * 7 // 8:
            break
    lead = (None,) * len(w_idx)
    return pl.pallas_call(
        functools.partial(_mm_res_ln_kernel, n_x=len(x_parts), alpha=alpha, coef=coef),
        grid=(m // tm,),
        in_specs=[pl.BlockSpec((tm, xp.shape[1]), lambda i: (i, 0)) for xp in x_parts] + [
                  pl.BlockSpec(lead + (k, n), lambda i: tuple(w_idx) + (0, 0), pipeline_mode=pl.Buffered(1)),
                  pl.BlockSpec((tm, n), lambda i: (i, 0)),
                  pl.BlockSpec((1, n), lambda i: (0, 0)),
                  pl.BlockSpec((1, n), lambda i: (0, 0))],
        out_specs=[pl.BlockSpec((tm, n), lambda i: (i, 0)),
                   pl.BlockSpec((tm, n), lambda i: (i, 0))],
        out_shape=[jax.ShapeDtypeStruct((m, n), F32), jax.ShapeDtypeStruct((m, n), BF)],
        compiler_params=_params("parallel"),
        name=name,
    )(*x_parts, w_all, res, g.reshape(1, n).astype(F32), b.reshape(1, n).astype(F32))


def _attn_kernel(qi_ref, kj_ref, fst_ref, lst_ref, msk_ref, *refs, variant, n_grp, n_rep, tq, tk, rc, dv, c_exp,
                 window, seq, has_sel, has_sink, gate_col, has_add, two_paths):
    it = iter(refs)
    if variant == "mla":
        qn_ref, qr_ref, kn_ref, kp_ref, v_ref = (next(it) for _ in range(5))
    elif variant == "swa":
        q_ref, kv_ref = next(it), next(it)
    else:
        q_ref, k_ref, v_ref = next(it), next(it), next(it)
    sel_ref = next(it) if has_sel else None
    sink_ref = next(it) if has_sink else None
    gl_ref, gb_ref = (next(it), next(it)) if gate_col is not None else (None, None)
    add_ref = next(it) if has_add else None
    o_ref = next(it)
    q_sc, m_sc, acc_sc, bias_sc = next(it), next(it), next(it), next(it)
    p = pl.program_id(1)
    qi = qi_ref[p]
    kj = kj_ref[p]
    rows = n_rep * tq

    @pl.when(fst_ref[p] == 1)
    def _():
        if has_sink:
            m_sc[...] = sink_ref[0]
            lane = lax.broadcasted_iota(jnp.int32, acc_sc.shape, 1)
            acc_sc[...] = jnp.where(lane >= dv, 1.0, 0.0)
        else:
            m_sc[...] = jnp.full_like(m_sc, NEG)
            acc_sc[...] = jnp.zeros_like(acc_sc)
        if variant == "mla":
            half_of_lane = lax.broadcasted_iota(jnp.int32, (tq, LANE), 1) // MLA_ROPE
            for r in range(n_rep):
                rope_pair = qr_ref[:, (r // 2) * LANE:(r // 2 + 1) * LANE].astype(F32) * c_exp
                q_sc[r * tq:(r + 1) * tq, :LANE] = (qn_ref[:, r * LANE:(r + 1) * LANE].astype(F32) * c_exp).astype(BF)
                q_sc[r * tq:(r + 1) * tq, LANE:] = jnp.where(half_of_lane == r % 2, rope_pair, 0.0).astype(BF)
        elif variant == "swa":
            low = lax.broadcasted_iota(jnp.int32, (tq, LANE), 1) < dv
            for r in range(n_rep):
                chunk = q_ref[:, (r // 2) * LANE:(r // 2 + 1) * LANE].astype(F32)
                if r % 2 == 1:
                    chunk = pltpu.roll(chunk, dv, 1)
                q_sc[r * tq:(r + 1) * tq, :] = jnp.where(low, chunk * c_exp, 0.0).astype(BF)
        else:
            dq = q_sc.shape[1]
            for r in range(n_rep):
                q_sc[r * tq:(r + 1) * tq, :] = (q_ref[:, r * dq:(r + 1) * dq].astype(F32) * c_exp).astype(BF)

    ones = jnp.ones((tk, dv), BF)
    if variant == "mla":
        ks = [jnp.concatenate([kn_ref[:, r * LANE:(r + 1) * LANE], kp_ref[...]], axis=1) for r in range(n_rep)]
        vs = [jnp.concatenate([v_ref[:, r * dv:(r + 1) * dv], ones], axis=1) for r in range(n_rep)]
    elif variant == "swa":
        swapped = pltpu.roll(kv_ref[...].astype(F32), dv, 1)
        ks = [kv_ref[...]]
        vs = [jnp.where(lax.broadcasted_iota(jnp.int32, (tk, LANE), 1) < dv, swapped, 1.0).astype(BF)]
    else:
        ks = [k_ref[...]]
        vs = [jnp.concatenate([v_ref[...], ones], axis=1)]

    def kv_of(u):
        return (u * rc) // tq if len(ks) > 1 else 0

    def tile_bias():
        row = lax.broadcasted_iota(jnp.int32, (tq, tk), 0)
        col = lax.broadcasted_iota(jnp.int32, (tq, tk), 1)
        dist = qi * tq - kj * tk + row - col
        mask = dist >= 0
        if window < seq:
            mask = mask & (dist < window)
        if has_sel:
            n_blk = sel_ref.shape[-1]
            blk_of_key = (kj * tk + lax.broadcasted_iota(jnp.int32, (n_blk, tk), 1)) // SLC_BLOCK
            expand = jnp.where(lax.broadcasted_iota(jnp.int32, (n_blk, tk), 0) == blk_of_key, 1.0, 0.0).astype(BF)
            picked = jnp.dot(sel_ref[0].astype(BF), expand, preferred_element_type=F32)
            mask = mask & (picked > 0.5)
        return jnp.where(mask, 0.0, NEG)

    def logits(u):
        return lax.dot_general(q_sc[u * rc:(u + 1) * rc, :], ks[kv_of(u)], (((1,), (1,)), ((), ())),
                               preferred_element_type=F32)

    def online_softmax(masked):
        if masked and has_sel:
            bias_sc[...] = tile_bias()
        if masked and not has_sel:
            rel = (lax.broadcasted_iota(jnp.int32, (rc, tk), 0) - lax.broadcasted_iota(jnp.int32, (rc, tk), 1))
        n_units = rows // rc
        s_next = logits(0)
        for u in range(n_units):
            s = s_next
            if u + 1 < n_units:
                s_next = logits(u + 1)
            us = slice(u * rc, (u + 1) * rc)
            m_prev = m_sc[us, :]
            if masked and has_sel:
                s = s + bias_sc[(u * rc) % tq:(u * rc) % tq + rc, :]
            elif masked:
                base = qi * tq - kj * tk + (u * rc) % tq
                visible = rel >= -base
                if window < seq:
                    visible = visible & (rel < window - base)
                s = jnp.where(visible, s, NEG)
            m_new = jnp.maximum(m_prev, jnp.max(s, -1, keepdims=True))
            w = jnp.exp2(s - _lane_tile(m_new, tk // LANE))
            m_sc[us, :] = m_new
            alpha = jnp.exp2(m_prev - m_new)
            acc_sc[us, :] = (_lane_tile(alpha, acc_sc.shape[1] // LANE) * acc_sc[us, :]
                             + jnp.dot(w.astype(BF), vs[kv_of(u)], preferred_element_type=F32))

    if not two_paths:
        online_softmax(True)
    else:
        pl.when(msk_ref[p] == 1)(lambda: online_softmax(True))
        pl.when(msk_ref[p] == 0)(lambda: online_softmax(False))

    @pl.when(lst_ref[p] == 1)
    def _():
        acc = acc_sc[...]
        if dv % LANE == 0:
            o = acc[:, :dv] / acc[:, dv:]
        else:
            o = acc[:, :dv] / acc[:, dv:dv + 1]
        if gate_col is not None:
            gates = jax.nn.sigmoid(gl_ref[...] + gb_ref[...])
        for r in range(n_rep):
            o_r = o[r * tq:(r + 1) * tq]
            if gate_col is not None:
                o_r = o_r * gates[:, 3 * r + gate_col:3 * r + gate_col + 1]
            if has_add:
                o_r = o_r + add_ref[:, r * dv:(r + 1) * dv].astype(F32)
            o_ref[:, r * dv:(r + 1) * dv] = o_r.astype(o_ref.dtype)


def _attn_call(variant, qkv_args, qkv_specs, *, batch, seq, n_grp, n_rep, dq, dv, scale, window, tq, tk,
               sel=None, sinks=None, gate=None, add=None, unit_rows=UNIT_ROWS, name="attn"):
    rows = n_rep * tq
    nq, nk = seq // tq, seq // tk
    pairs, fst, lst, msk = [], [], [], []
    for i in range(nq):
        lo = max(0, i * tq - (window - 1)) // tk
        hi = (i * tq + tq - 1) // tk
        for j in range(lo, hi + 1):
            pairs.append((i, j))
            fst.append(int(j == lo))
            lst.append(int(j == hi))
            inside = i * tq - (j * tk + tk - 1) >= 0 and i * tq + tq - 1 - j * tk < window
            msk.append(int(sel is not None or not inside))
    two_paths = 0 < sum(msk) < len(msk)
    qi = jnp.asarray([a for a, _ in pairs], jnp.int32)
    kj = jnp.asarray([b for _, b in pairs], jnp.int32)

    def q_rows(b, p, qi, kj):
        return (b // n_grp) * nq + qi[p]

    def k_rows(b, p, qi, kj):
        return (b // n_grp) * nk + kj[p]

    in_specs = [mk(q_rows, k_rows) for mk in qkv_specs]
    args = list(qkv_args)
    if sel is not None:
        in_specs.append(pl.BlockSpec((1, tq, sel.shape[-1]), lambda b, p, qi, kj, *_: (b, qi[p], 0)))
        args.append(sel)
    if sinks is not None:
        m0 = jnp.broadcast_to((sinks.astype(F32) * LOG2E)[:, :, None, None], (n_grp, n_rep, tq, LANE))
        in_specs.append(pl.BlockSpec((1, rows, LANE), lambda b, p, qi, kj, *_: (b % n_grp, 0, 0)))
        args.append(m0.reshape(n_grp, rows, LANE))
    gate_col = None
    if gate is not None:
        gl, gb, gate_col = gate
        in_specs += [pl.BlockSpec((tq, LANE), lambda b, p, qi, kj, *_: (q_rows(b, p, qi, kj), b % n_grp)),
                     pl.BlockSpec((1, LANE), lambda b, p, qi, kj, *_: (0, b % n_grp))]
        args += [gl, gb]
    out_spec = pl.BlockSpec((tq, n_rep * dv), lambda b, p, qi, kj, *_: (q_rows(b, p, qi, kj), b % n_grp))
    if add is not None:
        in_specs.append(out_spec)
        args.append(add)
    rc = min(tq, unit_rows)
    kern = functools.partial(_attn_kernel, variant=variant, n_grp=n_grp, n_rep=n_rep, tq=tq, tk=tk, rc=rc, dv=dv,
                             c_exp=scale * LOG2E, window=window, seq=seq, has_sel=sel is not None,
                             has_sink=sinks is not None, gate_col=gate_col, has_add=add is not None,
                             two_paths=two_paths)
    return pl.pallas_call(
        kern,
        grid_spec=pltpu.PrefetchScalarGridSpec(
            num_scalar_prefetch=5,
            grid=(batch * n_grp, len(pairs)),
            in_specs=in_specs,
            out_specs=out_spec,
            scratch_shapes=[pltpu.VMEM((rows, dq), BF), pltpu.VMEM((rows, LANE), F32),
                            pltpu.VMEM((rows, 2 * dv), F32), pltpu.VMEM((tq, tk), F32)]),
        out_shape=jax.ShapeDtypeStruct((batch * seq, n_grp * n_rep * dv), BF),
        compiler_params=_params("parallel", "arbitrary"),
        name=name,
    )(qi, kj, jnp.asarray(fst, jnp.int32), jnp.asarray(lst, jnp.int32), jnp.asarray(msk, jnp.int32), *args)


def _spec(shape, rows_of, col_fn):
    def make(q_rows, k_rows):
        rf = q_rows if rows_of == "q" else k_rows
        return pl.BlockSpec(shape, lambda b, p, qi, kj, *_: (rf(b, p, qi, kj), col_fn(b)))
    return make


def _band_kernel(*refs, n_rep, t, n_kt, rc, dv, c_exp, window, gate_col, has_add):
    it = iter(refs)
    q_ref = next(it)
    k_refs = [next(it) for _ in range(n_kt)]
    v_refs = [next(it) for _ in range(n_kt)]
    gl_ref, gb_ref = (next(it), next(it)) if gate_col is not None else (None, None)
    add_ref = next(it) if has_add else None
    o_ref = next(it)
    i = pl.program_id(1)
    n_keys = n_kt * t
    k = jnp.concatenate([r[...] for r in k_refs], axis=0)
    d = k.shape[1]
    v_ext = jnp.concatenate([jnp.concatenate([r[...] for r in v_refs], axis=0), jnp.ones((n_keys, dv), BF)], axis=1)
    col = lax.broadcasted_iota(jnp.int32, (rc, n_keys), 1)
    rel = lax.broadcasted_iota(jnp.int32, (rc, n_keys), 0) - col
    in_seq = col >= (n_kt - 1 - i) * t
    if gate_col is not None:
        gates = jax.nn.sigmoid(gl_ref[...] + gb_ref[...])

    def q_block(u):
        r, i0 = (u * rc) // t, (u * rc) % t
        return (q_ref[i0:i0 + rc, r * d:(r + 1) * d].astype(F32) * c_exp).astype(BF)

    def logits(u):
        return lax.dot_general(q_block(u), k, (((1,), (1,)), ((), ())), preferred_element_type=F32)

    n_units = n_rep * t // rc
    s_next = logits(0)
    for u in range(n_units):
        s = s_next
        if u + 1 < n_units:
            s_next = logits(u + 1)
        r, i0 = (u * rc) // t, (u * rc) % t
        base = (n_kt - 1) * t + i0
        visible = (rel >= -base) & (rel < window - base) & in_seq
        s = jnp.where(visible, s, NEG)
        w = jnp.exp2(s - jnp.max(s, -1, keepdims=True))
        ext = jnp.dot(w.astype(BF), v_ext, preferred_element_type=F32)
        o_u = ext[:, :dv] / ext[:, dv:]
        if gate_col is not None:
            o_u = o_u * gates[i0:i0 + rc, 3 * r + gate_col:3 * r + gate_col + 1]
        if has_add:
            o_u = o_u + add_ref[i0:i0 + rc, r * dv:(r + 1) * dv].astype(F32)
        o_ref[i0:i0 + rc, r * dv:(r + 1) * dv] = o_u.astype(o_ref.dtype)


def _band_call(z, q_col, k_col, v_col, *, batch, seq, n_grp, n_rep, d, scale, window, t, gate=None, add=None,
               name="band_attn"):
    dv = d
    n_kt = -(-(window - 1) // t) + 1
    nq = seq // t
    rc = min(t, UNIT_ROWS)

    def q_row(b, i):
        return (b // n_grp) * nq + i

    in_specs = [pl.BlockSpec((t, n_rep * d), lambda b, i: (q_row(b, i), q_col // (n_rep * d) + b % n_grp))]
    args = [z]
    for col in (k_col, v_col):
        for j in range(n_kt):
            in_specs.append(pl.BlockSpec(
                (t, d), lambda b, i, j=j, col=col: ((b // n_grp) * nq + jnp.maximum(i - (n_kt - 1) + j, 0),
                                                     col // d + b % n_grp)))
            args.append(z)
    gate_col = None
    if gate is not None:
        gl, gb, gate_col = gate
        in_specs += [pl.BlockSpec((t, LANE), lambda b, i: (q_row(b, i), b % n_grp)),
                     pl.BlockSpec((1, LANE), lambda b, i: (0, b % n_grp))]
        args += [gl, gb]
    out_spec = pl.BlockSpec((t, n_rep * dv), lambda b, i: (q_row(b, i), b % n_grp))
    if add is not None:
        in_specs.append(out_spec)
        args.append(add)
    kern = functools.partial(_band_kernel, n_rep=n_rep, t=t, n_kt=n_kt, rc=rc, dv=dv, c_exp=scale * LOG2E,
                             window=window, gate_col=gate_col, has_add=add is not None)
    return pl.pallas_call(
        kern,
        grid=(batch * n_grp, nq),
        in_specs=in_specs,
        out_specs=out_spec,
        out_shape=jax.ShapeDtypeStruct((batch * seq, n_grp * n_rep * dv), BF),
        compiler_params=_params("parallel", "parallel"),
        name=name,
    )(*args)


def _compress_kernel(x_ref, pa_ref, pb_ref, w1a_ref, w1b_ref, w2_ref, o_ref):
    x = x_ref[...].astype(F32)
    a = jnp.dot((x + pa_ref[...]).astype(BF), w1a_ref[...], preferred_element_type=F32)
    b = jnp.dot((x + pb_ref[...]).astype(BF), w1b_ref[...], preferred_element_type=F32)
    h = jax.nn.gelu(a + pltpu.roll(b, b.shape[0] - 1, 0))
    o_ref[0] = jnp.dot(h.astype(BF), w2_ref[...], preferred_element_type=F32).astype(o_ref.dtype)


def _compress(zc, first_blk, pos, w1, w2, *, batch, seq, n_grp, name):
    d = w2.shape[1]
    hid = w1.shape[1]
    nch = seq // CMP_STRIDE
    width = CMP_STRIDE * d
    pa = pos[:CMP_STRIDE].reshape(1, width).astype(F32)
    pb = pos[CMP_STRIDE:].reshape(1, width).astype(F32)
    w1 = w1.astype(BF)
    return pl.pallas_call(
        _compress_kernel,
        grid=(batch * n_grp,),
        in_specs=[pl.BlockSpec((nch, width), lambda b: (b // n_grp, first_blk + b % n_grp)),
                  pl.BlockSpec((1, width), lambda b: (0, 0)),
                  pl.BlockSpec((1, width), lambda b: (0, 0)),
                  pl.BlockSpec((width, hid), lambda b: (0, 0)),
                  pl.BlockSpec((width, hid), lambda b: (1, 0)),
                  pl.BlockSpec((hid, d), lambda b: (0, 0))],
        out_specs=pl.BlockSpec((1, nch, d), lambda b: (b, 0, 0)),
        out_shape=jax.ShapeDtypeStruct((batch * n_grp, nch, d), BF),
        compiler_params=_params("parallel"),
        name=name,
    )(zc, pa, pb, w1, w1, w2.astype(BF))


def _cmp_attn_kernel(q_ref, k_ref, v_ref, ovt_ref, gl_ref, gb_ref, o_ref, sel_ref, *, n_rep, tq, c_exp):
    qi = pl.program_id(1)
    nc, d = k_ref.shape[1], k_ref.shape[2]
    rows = n_rep * tq
    q = jnp.concatenate([q_ref[:, r * d:(r + 1) * d] for r in range(n_rep)], axis=0)
    s = lax.dot_general(q, k_ref[0], (((1,), (1,)), ((), ())), preferred_element_type=F32)
    qpos = qi * tq + lax.broadcasted_iota(jnp.int32, (tq, nc), 0)
    cmp_end = lax.broadcasted_iota(jnp.int32, (tq, nc), 1) * CMP_STRIDE + (CMP_BLOCK - 1)
    cmask = (cmp_end <= qpos)[None]
    s = jnp.where(cmask, s.reshape(n_rep, tq, nc), NEG)
    e = jnp.exp2((s - jnp.max(s, -1, keepdims=True)) * c_exp)
    e = jnp.where(cmask, e, 0.0).reshape(rows, nc)
    ext = jnp.dot(e.astype(BF), v_ref[0], preferred_element_type=F32)
    den = ext[:, d:]
    inv = 1.0 / jnp.where(den > 0.0, den, 1.0)
    gates = jax.nn.sigmoid(gl_ref[...] + gb_ref[...])
    o = ext[:, :d] * inv
    for r in range(n_rep):
        o_ref[:, r * d:(r + 1) * d] = (o[r * tq:(r + 1) * tq] * gates[:, 3 * r:3 * r + 1]).astype(o_ref.dtype)
    inv_c = inv[:, :nc] if nc <= d else _lane_tile(inv, nc // d)
    prob = e * inv_c
    psum = jnp.sum(prob.reshape(n_rep, tq, nc), axis=0)
    p_hi = psum.astype(BF)
    p_lo = (psum - p_hi.astype(F32)).astype(BF)
    contract_last = (((1,), (1,)), ((), ()))
    imp = (lax.dot_general(ovt_ref[...], p_hi, contract_last, preferred_element_type=F32)
           + lax.dot_general(ovt_ref[...], p_lo, contract_last, preferred_element_type=F32))
    ns = imp.shape[0]
    blk = lax.broadcasted_iota(jnp.int32, (ns, tq), 0)
    cur = (qi * tq + lax.broadcasted_iota(jnp.int32, (ns, tq), 1)) // SLC_BLOCK
    eligible = blk <= cur
    forced = (blk == 0) | (blk == cur) | (blk == cur - 1)
    score = jnp.where(eligible, jnp.where(forced, FORCE_SCORE, imp), -1.0)
    beaten_by = jnp.zeros((ns, tq), F32)
    for kk in range(ns):
        sk = score[kk:kk + 1, :]
        beats = (sk > score) | ((sk == score) & (blk > kk))
        beaten_by = beaten_by + jnp.where(beats, 1.0, 0.0)
    n_sel = min(N_SEL, ns)
    sel_t = jnp.where((beaten_by < n_sel) & (score >= 0.0), 1.0, 0.0)
    sel_sq = jnp.concatenate([sel_t, jnp.zeros((tq - ns, tq), F32)], axis=0) if ns < tq else sel_t
    sel_ref[0] = sel_sq.T[:, :ns]


def _cmp_attention(z, kcmp, vcmp, gl, gb, *, batch, seq, n_grp, n_rep, scale, tq):
    bg, nc, d = kcmp.shape
    ns = seq // SLC_BLOCK
    nq = seq // tq
    starts = np.arange(nc) * CMP_STRIDE
    jb = np.arange(ns)
    overlap = ((starts[:, None] < (jb[None, :] + 1) * SLC_BLOCK)
               & (starts[:, None] + CMP_BLOCK > jb[None, :] * SLC_BLOCK)).astype(np.float32)
    v_ext = jnp.concatenate([vcmp, jnp.ones_like(vcmp)], -1)
    tok_spec = pl.BlockSpec((tq, n_rep * d), lambda b, i: ((b // n_grp) * nq + i, b % n_grp))
    return pl.pallas_call(
        functools.partial(_cmp_attn_kernel, n_rep=n_rep, tq=tq, c_exp=scale * LOG2E),
        grid=(bg, nq),
        in_specs=[tok_spec,
                  pl.BlockSpec((1, nc, d), lambda b, i: (b, 0, 0)),
                  pl.BlockSpec((1, nc, 2 * d), lambda b, i: (b, 0, 0)),
                  pl.BlockSpec((ns, nc), lambda b, i: (0, 0)),
                  pl.BlockSpec((tq, LANE), lambda b, i: ((b // n_grp) * nq + i, b % n_grp)),
                  pl.BlockSpec((1, LANE), lambda b, i: (0, b % n_grp))],
        out_specs=[tok_spec, pl.BlockSpec((1, tq, ns), lambda b, i: (b, i, 0))],
        out_shape=[jax.ShapeDtypeStruct((batch * seq, n_grp * n_rep * d), BF),
                   jax.ShapeDtypeStruct((bg, seq, ns), F32)],
        compiler_params=_params("parallel", "parallel"),
        name="nsa_cmp_attn",
    )(z, kcmp, v_ext, jnp.asarray(overlap.T, BF), gl, gb)


def _rope_tables(seq, dim):
    inv = 1.0 / (ROPE_THETA ** (jnp.arange(0, dim, 2, dtype=F32) / dim))
    ang = jnp.arange(seq, dtype=F32)[:, None] * inv[None, :]
    cos_h = jnp.concatenate([jnp.cos(ang), jnp.cos(ang)], -1)
    sin_h = jnp.concatenate([-jnp.sin(ang), jnp.sin(ang)], -1)
    return jnp.tile(cos_h, (1, LANE // dim)), jnp.tile(sin_h, (1, LANE // dim))


def _pad_cols(w, n):
    return jnp.pad(w, ((0, 0), (0, n - w.shape[1])))


def _even_mixer(x16, batch, seq, w_in, q_norm, w_uq, kv_norm, w_ukv, sinks):
    H, dn, dr, dv = MLA_HEADS, MLA_NOPE, MLA_ROPE, MLA_V
    G, R, ds = SWA_KV_HEADS, SWA_HEADS // SWA_KV_HEADS, SWA_HEAD_DIM
    n_lat = MLA_Q_RANK + MLA_KV_RANK
    w_in, w_uq, w_ukv = w_in.astype(BF), w_uq.astype(BF), w_ukv.astype(BF)
    o_qs = n_lat + dr
    o_ks = o_qs + SWA_HEADS * ds
    o_vs = o_ks + G * ds
    w_kpe = w_in[:, n_lat:o_qs]
    cols = [w_in[:, o_qs:o_ks]]
    for g in range(G):
        cols += [w_in[:, o_ks + g * ds:o_ks + (g + 1) * ds], w_in[:, o_vs + g * ds:o_vs + (g + 1) * ds]]
    cols += [w_kpe, w_kpe]
    w_rg = jnp.concatenate(cols, -1)
    n_q = SWA_HEADS * ds
    cos, sin = _rope_tables(seq, ds)
    groups = "R" * (n_q // LANE) + "L" * G + "R"
    z, z_lat = _proj(x16, jnp.concatenate([w_rg, w_in[:, :n_lat]], -1),
                     rope=(cos, sin, ds // 2, groups), seq=seq, out_dtype=BF, n_tail=n_lat, name="even_in")

    w_uq_p = jnp.concatenate([w_uq.reshape(-1, H, dn + dr)[:, :, :dn].reshape(-1, H * dn),
                              w_uq.reshape(-1, H, dn + dr)[:, :, dn:].reshape(-1, H * dr)], -1)
    w_ukv_p = jnp.concatenate([w_ukv.reshape(-1, H, dn + dv)[:, :, :dn].reshape(-1, H * dn),
                               w_ukv.reshape(-1, H, dn + dv)[:, :, dn:].reshape(-1, H * dv)], -1)
    cos_q, sin_q = _rope_tables(seq, dr)
    q = _proj(z_lat, w_uq_p, k_block=0, k_size=MLA_Q_RANK, rms_g=q_norm,
              rope=(cos_q, sin_q, dr // 2, "N" * (H * dn // LANE) + "R" * (H * dr // LANE)), seq=seq, out_dtype=BF,
              name="mla_q_up")
    kv = _proj(z_lat, w_ukv_p, k_block=1, k_size=MLA_KV_RANK, rms_g=kv_norm, out_dtype=BF,
               name="mla_kv_up")
    tq = _pick(seq, (512, 256, 128))
    tk = _pick(seq, (512, 256, 128))
    kpe_blk = (n_q + 2 * G * ds) // LANE
    hp = MLA_HEADS_PER_STEP
    n_hg = H // hp
    o_mla = _attn_call(
        "mla", [q, q, kv, z, kv],
        [_spec((tq, hp * dn), "q", lambda b: b % n_hg),
         _spec((tq, hp * dr), "q", lambda b: H * dn // (hp * dr) + b % n_hg),
         _spec((tk, hp * dn), "k", lambda b: b % n_hg),
         _spec((tk, LANE), "k", lambda b: kpe_blk),
         _spec((tk, hp * dv), "k", lambda b: H * dn // (hp * dv) + b % n_hg)],
        batch=batch, seq=seq, n_grp=n_hg, n_rep=hp, dq=2 * LANE, dv=dv, scale=(dn + dr) ** -0.5, window=seq,
        tq=tq, tk=tk, unit_rows=4 * UNIT_ROWS, name="mla_attn")

    tq = _pick(seq, (256, 128))
    o_swa = _attn_call(
        "swa", [z, z],
        [_spec((tq, R * ds), "q", lambda b: b % G),
         _spec((tq, LANE), "k", lambda b: n_q // LANE + b % G)],
        batch=batch, seq=seq, n_grp=G, n_rep=R, dq=LANE, dv=ds, scale=ds ** -0.5, window=SWA_WINDOW,
        tq=tq, tk=tq, sinks=sinks.reshape(G, R), name="swa_attn")
    return [o_mla, o_swa]


def _odd_mixer(x16, batch, seq, w_in, gate_b, cmp_pos_k, cmp_pos_v, k_w1, k_w2, v_w1, v_w2):
    G, d = NSA_KV_GROUPS, NSA_HEAD_DIM
    H = NSA_HEADS
    R = H // G
    kvw = G * d
    w_in = w_in.astype(BF)
    o_kc = H * d
    offs = [o_kc + i * kvw for i in range(7)]
    cos, sin = _rope_tables(seq, d)
    groups = "R" * H + ("R" * G + "N" * G) * 3
    w_gl = jnp.concatenate([_pad_cols(w_in[:, offs[6] + g * 3 * R:offs[6] + (g + 1) * 3 * R], LANE)
                            for g in range(G)], -1)
    cmp_groups = [offs[0] // d + g for g in range(G)] + [offs[1] // d + g for g in range(G)]
    z, gl, zc = _proj(x16, jnp.concatenate([w_in[:, :offs[6]], w_gl], -1),
                      rope=(cos, sin, d // 2, groups), seq=seq, out_dtype=BF, n_tail=G * LANE, chunked=cmp_groups,
                      name="odd_in")
    gb = jnp.concatenate([jnp.pad(gate_b[g * 3 * R:(g + 1) * 3 * R].astype(F32), (0, LANE - 3 * R))
                          for g in range(G)]).reshape(1, G * LANE)

    scale = d ** -0.5
    kcmp = _compress(zc, 0, cmp_pos_k, k_w1, k_w2, batch=batch, seq=seq, n_grp=G, name="nsa_compress_k")
    vcmp = _compress(zc, G, cmp_pos_v, v_w1, v_w2, batch=batch, seq=seq, n_grp=G, name="nsa_compress_v")
    tq = _pick(seq, (256, 128))
    o, sel = _cmp_attention(z, kcmp, vcmp, gl, gb, batch=batch, seq=seq, n_grp=G, n_rep=R, scale=scale, tq=tq)

    def branch(k_off, v_off, **kw):
        return _attn_call(
            "gqa", [z, z, z],
            [_spec((kw["tq"], R * d), "q", lambda b: b % G),
             _spec((kw["tk"], d), "k", lambda b: k_off // d + b % G),
             _spec((kw["tk"], d), "k", lambda b: v_off // d + b % G)],
            batch=batch, seq=seq, n_grp=G, n_rep=R, dq=d, dv=d, scale=scale, **kw)

    o = branch(offs[2], offs[3], window=seq, tq=_pick(seq, (512, 256, 128)), tk=_pick(seq, (512, 256, 128)),
               sel=sel, gate=(gl, gb, 1), add=o, name="nsa_slc_attn")
    o = _band_call(z, 0, offs[4], offs[5], batch=batch, seq=seq, n_grp=G, n_rep=R, d=d, scale=scale,
                   window=NSA_WINDOW, t=tq, gate=(gl, gb, 2), add=o, name="nsa_win_attn")
    return [o]


def kernel(x, ln_g, ln_b, ffn_w_gate, ffn_w_up, ffn_w_down, even_w_in, mla_q_norm, mla_w_uq, mla_kv_norm, mla_w_ukv, swa_sinks, even_w_out, odd_w_in, nsa_gate_b, nsa_cmp_pos_k, nsa_cmp_pos_v, nsa_cmp_k_w1, nsa_cmp_k_w2, nsa_cmp_v_w1, nsa_cmp_v_w2, odd_w_out):
    batch, seq, dm = x.shape
    depth = ln_g.shape[0]
    alpha = float((2 * depth) ** 0.25)
    x32 = x.reshape(batch * seq, dm)
    x16 = x32.astype(BF)
    even_out16 = even_w_out.astype(BF)
    odd_out16 = odd_w_out.astype(BF)

    def ffn(x32, x16, l, idx):
        h, w_down16 = _ffn_up(x16, ffn_w_gate, ffn_w_up, ffn_w_down, l, idx)
        return _mm_res_ln([h], w_down16, (), x32, ln_g[l, 2 * idx], ln_b[l, 2 * idx], alpha=alpha, coef=0.5,
                          name="ffn_down_ln")

    for l in range(depth):
        x32, x16 = ffn(x32, x16, l, 0)
        j = l // 2
        if l % 2 == 0:
            o = _even_mixer(x16, batch, seq, even_w_in[j], mla_q_norm[j], mla_w_uq[j], mla_kv_norm[j],
                            mla_w_ukv[j], swa_sinks[j])
            w_out = even_out16
        else:
            o = _odd_mixer(x16, batch, seq, odd_w_in[j], nsa_gate_b[j], nsa_cmp_pos_k[j], nsa_cmp_pos_v[j],
                           nsa_cmp_k_w1[j], nsa_cmp_k_w2[j], nsa_cmp_v_w1[j], nsa_cmp_v_w2[j])
            w_out = odd_out16
        x32, x16 = _mm_res_ln(o, w_out, (j,), x32, ln_g[l, 1], ln_b[l, 1], alpha=alpha, coef=1.0,
                              name="mixer_out_ln")
        x32, x16 = ffn(x32, x16, l, 1)
    return x32.reshape(batch, seq, dm)
```

```python
import functools
import math

import numpy as np
import jax
import jax.numpy as jnp
from jax import lax
from jax.experimental import pallas as pl
from jax.experimental.pallas import tpu as pltpu

F32 = jnp.float32
BF = jnp.bfloat16

ROPE_THETA = 10000.0
LN_EPS = 1e-5
RMS_EPS = 1e-6
NEG = -1e30
FORCE_SCORE = 1e9
LOG2E = math.log2(math.e)

MLA_HEADS, MLA_NOPE, MLA_ROPE, MLA_V = 8, 128, 64, 128
MLA_Q_RANK, MLA_KV_RANK = 512, 512
SWA_HEADS, SWA_KV_HEADS, SWA_HEAD_DIM, SWA_WINDOW = 16, 2, 64, 128
NSA_HEADS, NSA_KV_GROUPS, NSA_HEAD_DIM = 16, 2, 128
CMP_BLOCK, CMP_STRIDE = 32, 16
SLC_BLOCK, N_SEL, NSA_WINDOW = 64, 8, 512

LANE = 128
UNIT_ROWS = 128
MLA_HEADS_PER_STEP = 8
VMEM_LIMIT_BYTES = 52 * 1024 * 1024
TILE_BUDGET_BYTES = VMEM_LIMIT_BYTES - 6 * 1024 * 1024


def _params(*sem):
    return pltpu.CompilerParams(dimension_semantics=sem, vmem_limit_bytes=VMEM_LIMIT_BYTES)


def _pick(n, candidates):
    for c in candidates:
        if n % c == 0:
            return c
    raise ValueError(f"no tile in {candidates} divides {n}")


def _lane_tile(x, n):
    return x if n == 1 else jnp.concatenate([x] * n, axis=1)


def _proj_kernel(*refs, has_rms, groups, half, n_tail, chunked):
    it = iter(refs)
    x_ref, w_ref = next(it), next(it)
    g_ref = next(it) if has_rms else None
    cos_ref, sin_ref = (next(it), next(it)) if groups else (None, None)
    o_ref = next(it)
    tail_ref = next(it) if n_tail else None
    chunk_ref, stage = (next(it), next(it)) if chunked else (None, None)
    x = x_ref[...]
    if has_rms:
        xf = x.astype(F32)
        x = xf * lax.rsqrt(jnp.mean(xf * xf, -1, keepdims=True) + RMS_EPS) * g_ref[...]
    acc = jnp.dot(x.astype(BF), w_ref[...], preferred_element_type=F32)
    if n_tail:
        tail_ref[...] = acc[:, acc.shape[1] - n_tail:]
        acc = acc[:, :acc.shape[1] - n_tail]
    if not groups:
        o_ref[...] = acc.astype(o_ref.dtype)
        return
    cos, sin = cos_ref[...], sin_ref[...]
    lane = lax.broadcasted_iota(jnp.int32, cos.shape, 1)
    low = lane < LANE // 2
    cos_l, sin_l = jnp.where(low, cos, 1.0), jnp.where(low, sin, 0.0)
    for c, kind in enumerate(groups):
        a = acc[:, c * LANE:(c + 1) * LANE]
        if kind != "N":
            if 2 * half == LANE:
                partner = pltpu.roll(a, half, 1)
            else:
                partner = jnp.where(lane % (2 * half) < half, pltpu.roll(a, LANE - half, 1), pltpu.roll(a, half, 1))
            a = a * cos + partner * sin if kind == "R" else a * cos_l + partner * sin_l
        o_ref[:, c * LANE:(c + 1) * LANE] = a.astype(o_ref.dtype)
        if c in chunked:
            stage[...] = a
            n_ch = a.shape[0] // CMP_STRIDE
            first = chunked.index(c) * CMP_STRIDE
            for l in range(CMP_STRIDE):
                chunk_ref[:, (first + l) * LANE:(first + l + 1) * LANE] = (
                    stage[pl.ds(l, n_ch, stride=CMP_STRIDE), :].astype(chunk_ref.dtype))


def _proj(x, w, *, k_block=0, k_size=None, rms_g=None, rope=None, seq=None, out_dtype=F32, n_tail=0, chunked=(),
          name="proj"):
    m = x.shape[0]
    k_size = x.shape[1] if k_size is None else k_size
    n = w.shape[1]
    n_main = n - n_tail
    out_bytes = jnp.dtype(out_dtype).itemsize
    for tm in (512, 256, 128):
        est = k_size * n * 2 + 2 * tm * k_size * x.dtype.itemsize + 2 * tm * n * out_bytes + 3 * tm * n * 4
        if m % tm == 0 and est <= TILE_BUDGET_BYTES:
            break
    in_specs = [pl.BlockSpec((tm, k_size), lambda i: (i, k_block)),
                pl.BlockSpec((k_size, n), lambda i: (0, 0), pipeline_mode=pl.Buffered(1))]
    args = [x, w]
    if rms_g is not None:
        in_specs.append(pl.BlockSpec((1, k_size), lambda i: (0, 0)))
        args.append(rms_g.reshape(1, k_size).astype(F32))
    half, groups = 0, ()
    if rope is not None:
        cos, sin, half, groups = rope
        sb = seq // tm
        in_specs += [pl.BlockSpec((tm, LANE), lambda i: (i % sb, 0))] * 2
        args += [cos, sin]
    out_specs = [pl.BlockSpec((tm, n_main), lambda i: (i, 0))]
    out_shape = [jax.ShapeDtypeStruct((m, n_main), out_dtype)]
    if n_tail:
        out_specs.append(pl.BlockSpec((tm, n_tail), lambda i: (i, 0)))
        out_shape.append(jax.ShapeDtypeStruct((m, n_tail), F32))
    scratch = []
    if chunked:
        width = len(chunked) * CMP_STRIDE * LANE
        out_specs.append(pl.BlockSpec((tm // CMP_STRIDE, width), lambda i: (i, 0)))
        out_shape.append(jax.ShapeDtypeStruct((m // CMP_STRIDE, width), out_dtype))
        scratch.append(pltpu.VMEM((tm, LANE), F32))
    out = pl.pallas_call(
        functools.partial(_proj_kernel, has_rms=rms_g is not None, groups=tuple(groups), half=half, n_tail=n_tail,
                          chunked=tuple(chunked)),
        grid=(m // tm,),
        in_specs=in_specs,
        out_specs=out_specs,
        out_shape=out_shape,
        scratch_shapes=scratch,
        compiler_params=_params("parallel"),
        name=name,
    )(*args)
    return out if len(out) > 1 else out[0]


def _ffn_up_kernel(x_ref, wg_ref, wu_ref, wd_ref, o_ref, wd16_ref, wg16, wu16):
    @pl.when(pl.program_id(1) == 0)
    def _():
        wg16[...] = wg_ref[...].astype(BF)
        wu16[...] = wu_ref[...].astype(BF)

    wd16_ref[...] = wd_ref[...].astype(BF)
    x = x_ref[...]
    g = jnp.dot(x, wg16[...], preferred_element_type=F32)
    u = jnp.dot(x, wu16[...], preferred_element_type=F32)
    o_ref[...] = (g * jax.nn.sigmoid(g) * u).astype(o_ref.dtype)


def _ffn_up(x16, wg_all, wu_all, wd_all, layer, idx):
    m, k = x16.shape
    n = wg_all.shape[-1]
    n_out = wd_all.shape[-1]
    tm = _pick(m, (1024, 512, 256, 128))
    tn = _pick(n, (512, 256, 128))
    ni = m // tm
    slab = n // ((n // tn) * ni)
    assert slab * (n // tn) * ni == n and slab % 16 == 0
    w_spec = pl.BlockSpec((None, None, k, tn), lambda j, i: (layer, idx, 0, j))
    return pl.pallas_call(
        _ffn_up_kernel,
        grid=(n // tn, ni),
        in_specs=[pl.BlockSpec((tm, k), lambda j, i: (i, 0)), w_spec, w_spec,
                  pl.BlockSpec((None, None, slab, n_out), lambda j, i: (layer, idx, j * ni + i, 0))],
        out_specs=[pl.BlockSpec((tm, tn), lambda j, i: (i, j)),
                   pl.BlockSpec((slab, n_out), lambda j, i: (j * ni + i, 0))],
        out_shape=[jax.ShapeDtypeStruct((m, n), BF), jax.ShapeDtypeStruct((n, n_out), BF)],
        scratch_shapes=[pltpu.VMEM((k, tn), BF), pltpu.VMEM((k, tn), BF)],
        compiler_params=_params("parallel", "arbitrary"),
        name="ffn_up",
    )(x16, wg_all, wu_all, wd_all)


def _mm_res_ln_kernel(*refs, n_x, alpha, coef):
    x_refs = refs[:n_x]
    w_ref, res_ref, g_ref, b_ref, o32_ref, o16_ref = refs[n_x:]
    x = x_refs[0][...] if n_x == 1 else jnp.concatenate([r[...] for r in x_refs], axis=1)
    y = alpha * res_ref[...] + coef * jnp.dot(x, w_ref[...], preferred_element_type=F32)
    mu = jnp.mean(y, -1, keepdims=True)
    yc = y - mu
    var = jnp.mean(yc * yc, -1, keepdims=True)
    out = yc * lax.rsqrt(var + LN_EPS) * g_ref[...] + b_ref[...]
    o32_ref[...] = out
    o16_ref[...] = out.astype(BF)


def _mm_res_ln(x_parts, w_all, w_idx, res, g, b, *, alpha, coef, name):
    m = x_parts[0].shape[0]
    k, n = w_all.shape[-2:]
    for tm in (512, 256, 128):
        est = k * n * 2 + 2 * tm * k * 2 + tm * n * (2 * 4 + 2 * 4 + 2 * 2 + 2 * 4)
        if m % tm == 0 and est <= TILE_BUDGET_BYTES:
            break
    lead = (None,) * len(w_idx)
    return pl.pallas_call(
        functools.partial(_mm_res_ln_kernel, n_x=len(x_parts), alpha=alpha, coef=coef),
        grid=(m // tm,),
        in_specs=[pl.BlockSpec((tm, xp.shape[1]), lambda i: (i, 0)) for xp in x_parts] + [
                  pl.BlockSpec(lead + (k, n), lambda i: tuple(w_idx) + (0, 0), pipeline_mode=pl.Buffered(1)),
                  pl.BlockSpec((tm, n), lambda i: (i, 0)),
                  pl.BlockSpec((1, n), lambda i: (0, 0)),
                  pl.BlockSpec((1, n), lambda i: (0, 0))],
        out_specs=[pl.BlockSpec((tm, n), lambda i: (i, 0)),
                   pl.BlockSpec((tm, n), lambda i: (i, 0))],
        out_shape=[jax.ShapeDtypeStruct((m, n), F32), jax.ShapeDtypeStruct((m, n), BF)],
        compiler_params=_params("parallel"),
        name=name,
    )(*x_parts, w_all, res, g.reshape(1, n).astype(F32), b.reshape(1, n).astype(F32))


def _attn_kernel(qi_ref, kj_ref, fst_ref, lst_ref, msk_ref, *refs, variant, n_grp, n_rep, tq, tk, rc, dv, c_exp,
                 window, seq, has_sel, has_sink, gate_col, has_add, two_paths):
    it = iter(refs)
    if variant == "mla":
        qn_ref, qr_ref, kn_ref, kp_ref, v_ref = (next(it) for _ in range(5))
    elif variant == "swa":
        q_ref, kv_ref = next(it), next(it)
    else:
        q_ref, k_ref, v_ref = next(it), next(it), next(it)
    sel_ref = next(it) if has_sel else None
    sink_ref = next(it) if has_sink else None
    gl_ref, gb_ref = (next(it), next(it)) if gate_col is not None else (None, None)
    add_ref = next(it) if has_add else None
    o_ref = next(it)
    q_sc, m_sc, acc_sc, bias_sc = next(it), next(it), next(it), next(it)
    p = pl.program_id(1)
    qi = qi_ref[p]
    kj = kj_ref[p]
    rows = n_rep * tq

    @pl.when(fst_ref[p] == 1)
    def _():
        if has_sink:
            m_sc[...] = sink_ref[0]
            lane = lax.broadcasted_iota(jnp.int32, acc_sc.shape, 1)
            acc_sc[...] = jnp.where(lane >= dv, 1.0, 0.0)
        else:
            m_sc[...] = jnp.full_like(m_sc, NEG)
            acc_sc[...] = jnp.zeros_like(acc_sc)
        if variant == "mla":
            half_of_lane = lax.broadcasted_iota(jnp.int32, (tq, LANE), 1) // MLA_ROPE
            for r in range(n_rep):
                rope_pair = qr_ref[:, (r // 2) * LANE:(r // 2 + 1) * LANE].astype(F32) * c_exp
                q_sc[r * tq:(r + 1) * tq, :LANE] = (qn_ref[:, r * LANE:(r + 1) * LANE].astype(F32) * c_exp).astype(BF)
                q_sc[r * tq:(r + 1) * tq, LANE:] = jnp.where(half_of_lane == r % 2, rope_pair, 0.0).astype(BF)
        elif variant == "swa":
            low = lax.broadcasted_iota(jnp.int32, (tq, LANE), 1) < dv
            for r in range(n_rep):
                chunk = q_ref[:, (r // 2) * LANE:(r // 2 + 1) * LANE].astype(F32)
                if r % 2 == 1:
                    chunk = pltpu.roll(chunk, dv, 1)
                q_sc[r * tq:(r + 1) * tq, :] = jnp.where(low, chunk * c_exp, 0.0).astype(BF)
        else:
            dq = q_sc.shape[1]
            for r in range(n_rep):
                q_sc[r * tq:(r + 1) * tq, :] = (q_ref[:, r * dq:(r + 1) * dq].astype(F32) * c_exp).astype(BF)

    ones = jnp.ones((tk, dv), BF)
    if variant == "mla":
        ks = [jnp.concatenate([kn_ref[:, r * LANE:(r + 1) * LANE], kp_ref[...]], axis=1) for r in range(n_rep)]
        vs = [jnp.concatenate([v_ref[:, r * dv:(r + 1) * dv], ones], axis=1) for r in range(n_rep)]
    elif variant == "swa":
        swapped = pltpu.roll(kv_ref[...].astype(F32), dv, 1)
        ks = [kv_ref[...]]
        vs = [jnp.where(lax.broadcasted_iota(jnp.int32, (tk, LANE), 1) < dv, swapped, 1.0).astype(BF)]
    else:
        ks = [k_ref[...]]
        vs = [jnp.concatenate([v_ref[...], ones], axis=1)]

    def kv_of(u):
        return (u * rc) // tq if len(ks) > 1 else 0

    def tile_bias():
        row = lax.broadcasted_iota(jnp.int32, (tq, tk), 0)
        col = lax.broadcasted_iota(jnp.int32, (tq, tk), 1)
        dist = qi * tq - kj * tk + row - col
        mask = dist >= 0
        if window < seq:
            mask = mask & (dist < window)
        if has_sel:
            n_blk = sel_ref.shape[-1]
            blk_of_key = (kj * tk + lax.broadcasted_iota(jnp.int32, (n_blk, tk), 1)) // SLC_BLOCK
            expand = jnp.where(lax.broadcasted_iota(jnp.int32, (n_blk, tk), 0) == blk_of_key, 1.0, 0.0).astype(BF)
            picked = jnp.dot(sel_ref[0].astype(BF), expand, preferred_element_type=F32)
            mask = mask & (picked > 0.5)
        return jnp.where(mask, 0.0, NEG)

    def logits(u):
        return lax.dot_general(q_sc[u * rc:(u + 1) * rc, :], ks[kv_of(u)], (((1,), (1,)), ((), ())),
                               preferred_element_type=F32)

    def online_softmax(masked):
        if masked and has_sel:
            bias_sc[...] = tile_bias()
        if masked and not has_sel:
            rel = (lax.broadcasted_iota(jnp.int32, (rc, tk), 0) - lax.broadcasted_iota(jnp.int32, (rc, tk), 1))
        n_units = rows // rc
        s_next = logits(0)
        for u in range(n_units):
            s = s_next
            if u + 1 < n_units:
                s_next = logits(u + 1)
            us = slice(u * rc, (u + 1) * rc)
            m_prev = m_sc[us, :]
            if masked and has_sel:
                s = s + bias_sc[(u * rc) % tq:(u * rc) % tq + rc, :]
            elif masked:
                base = qi * tq - kj * tk + (u * rc) % tq
                visible = rel >= -base
                if window < seq:
                    visible = visible & (rel < window - base)
                s = jnp.where(visible, s, NEG)
            m_new = jnp.maximum(m_prev, jnp.max(s, -1, keepdims=True))
            w = jnp.exp2(s - _lane_tile(m_new, tk // LANE))
            m_sc[us, :] = m_new
            alpha = jnp.exp2(m_prev - m_new)
            acc_sc[us, :] = (_lane_tile(alpha, acc_sc.shape[1] // LANE) * acc_sc[us, :]
                             + jnp.dot(w.astype(BF), vs[kv_of(u)], preferred_element_type=F32))

    if not two_paths:
        online_softmax(True)
    else:
        pl.when(msk_ref[p] == 1)(lambda: online_softmax(True))
        pl.when(msk_ref[p] == 0)(lambda: online_softmax(False))

    @pl.when(lst_ref[p] == 1)
    def _():
        acc = acc_sc[...]
        if dv % LANE == 0:
            o = acc[:, :dv] / acc[:, dv:]
        else:
            o = acc[:, :dv] / acc[:, dv:dv + 1]
        if gate_col is not None:
            gates = jax.nn.sigmoid(gl_ref[...] + gb_ref[...])
        for r in range(n_rep):
            o_r = o[r * tq:(r + 1) * tq]
            if gate_col is not None:
                o_r = o_r * gates[:, 3 * r + gate_col:3 * r + gate_col + 1]
            if has_add:
                o_r = o_r + add_ref[:, r * dv:(r + 1) * dv].astype(F32)
            o_ref[:, r * dv:(r + 1) * dv] = o_r.astype(o_ref.dtype)


def _attn_call(variant, qkv_args, qkv_specs, *, batch, seq, n_grp, n_rep, dq, dv, scale, window, tq, tk,
               sel=None, sinks=None, gate=None, add=None, unit_rows=UNIT_ROWS, name="attn"):
    rows = n_rep * tq
    nq, nk = seq // tq, seq // tk
    pairs, fst, lst, msk = [], [], [], []
    for i in range(nq):
        lo = max(0, i * tq - (window - 1)) // tk
        hi = (i * tq + tq - 1) // tk
        for j in range(lo, hi + 1):
            pairs.append((i, j))
            fst.append(int(j == lo))
            lst.append(int(j == hi))
            inside = i * tq - (j * tk + tk - 1) >= 0 and i * tq + tq - 1 - j * tk < window
            msk.append(int(sel is not None or not inside))
    two_paths = 0 < sum(msk) < len(msk)
    qi = jnp.asarray([a for a, _ in pairs], jnp.int32)
    kj = jnp.asarray([b for _, b in pairs], jnp.int32)

    def q_rows(b, p, qi, kj):
        return (b // n_grp) * nq + qi[p]

    def k_rows(b, p, qi, kj):
        return (b // n_grp) * nk + kj[p]

    in_specs = [mk(q_rows, k_rows) for mk in qkv_specs]
    args = list(qkv_args)
    if sel is not None:
        in_specs.append(pl.BlockSpec((1, tq, sel.shape[-1]), lambda b, p, qi, kj, *_: (b, qi[p], 0)))
        args.append(sel)
    if sinks is not None:
        m0 = jnp.broadcast_to((sinks.astype(F32) * LOG2E)[:, :, None, None], (n_grp, n_rep, tq, LANE))
        in_specs.append(pl.BlockSpec((1, rows, LANE), lambda b, p, qi, kj, *_: (b % n_grp, 0, 0)))
        args.append(m0.reshape(n_grp, rows, LANE))
    gate_col = None
    if gate is not None:
        gl, gb, gate_col = gate
        in_specs += [pl.BlockSpec((tq, LANE), lambda b, p, qi, kj, *_: (q_rows(b, p, qi, kj), b % n_grp)),
                     pl.BlockSpec((1, LANE), lambda b, p, qi, kj, *_: (0, b % n_grp))]
        args += [gl, gb]
    out_spec = pl.BlockSpec((tq, n_rep * dv), lambda b, p, qi, kj, *_: (q_rows(b, p, qi, kj), b % n_grp))
    if add is not None:
        in_specs.append(out_spec)
        args.append(add)
    rc = min(tq, unit_rows)
    kern = functools.partial(_attn_kernel, variant=variant, n_grp=n_grp, n_rep=n_rep, tq=tq, tk=tk, rc=rc, dv=dv,
                             c_exp=scale * LOG2E, window=window, seq=seq, has_sel=sel is not None,
                             has_sink=sinks is not None, gate_col=gate_col, has_add=add is not None,
                             two_paths=two_paths)
    return pl.pallas_call(
        kern,
        grid_spec=pltpu.PrefetchScalarGridSpec(
            num_scalar_prefetch=5,
            grid=(batch * n_grp, len(pairs)),
            in_specs=in_specs,
            out_specs=out_spec,
            scratch_shapes=[pltpu.VMEM((rows, dq), BF), pltpu.VMEM((rows, LANE), F32),
                            pltpu.VMEM((rows, 2 * dv), F32), pltpu.VMEM((tq, tk), F32)]),
        out_shape=jax.ShapeDtypeStruct((batch * seq, n_grp * n_rep * dv), BF),
        compiler_params=_params("parallel", "arbitrary"),
        name=name,
    )(qi, kj, jnp.asarray(fst, jnp.int32), jnp.asarray(lst, jnp.int32), jnp.asarray(msk, jnp.int32), *args)


def _spec(shape, rows_of, col_fn):
    def make(q_rows, k_rows):
        rf = q_rows if rows_of == "q" else k_rows
        return pl.BlockSpec(shape, lambda b, p, qi, kj, *_: (rf(b, p, qi, kj), col_fn(b)))
    return make


def _band_kernel(*refs, n_rep, t, n_kt, rc, dv, c_exp, window, gate_col, has_add):
    it = iter(refs)
    q_ref = next(it)
    k_refs = [next(it) for _ in range(n_kt)]
    v_refs = [next(it) for _ in range(n_kt)]
    gl_ref, gb_ref = (next(it), next(it)) if gate_col is not None else (None, None)
    add_ref = next(it) if has_add else None
    o_ref = next(it)
    i = pl.program_id(1)
    n_keys = n_kt * t
    k = jnp.concatenate([r[...] for r in k_refs], axis=0)
    d = k.shape[1]
    v_ext = jnp.concatenate([jnp.concatenate([r[...] for r in v_refs], axis=0), jnp.ones((n_keys, dv), BF)], axis=1)
    col = lax.broadcasted_iota(jnp.int32, (rc, n_keys), 1)
    rel = lax.broadcasted_iota(jnp.int32, (rc, n_keys), 0) - col
    in_seq = col >= (n_kt - 1 - i) * t
    if gate_col is not None:
        gates = jax.nn.sigmoid(gl_ref[...] + gb_ref[...])

    def q_block(u):
        r, i0 = (u * rc) // t, (u * rc) % t
        return (q_ref[i0:i0 + rc, r * d:(r + 1) * d].astype(F32) * c_exp).astype(BF)

    def logits(u):
        return lax.dot_general(q_block(u), k, (((1,), (1,)), ((), ())), preferred_element_type=F32)

    n_units = n_rep * t // rc
    s_next = logits(0)
    for u in range(n_units):
        s = s_next
        if u + 1 < n_units:
            s_next = logits(u + 1)
        r, i0 = (u * rc) // t, (u * rc) % t
        base = (n_kt - 1) * t + i0
        visible = (rel >= -base) & (rel < window - base) & in_seq
        s = jnp.where(visible, s, NEG)
        w = jnp.exp2(s - jnp.max(s, -1, keepdims=True))
        ext = jnp.dot(w.astype(BF), v_ext, preferred_element_type=F32)
        o_u = ext[:, :dv] / ext[:, dv:]
        if gate_col is not None:
            o_u = o_u * gates[i0:i0 + rc, 3 * r + gate_col:3 * r + gate_col + 1]
        if has_add:
            o_u = o_u + add_ref[i0:i0 + rc, r * dv:(r + 1) * dv].astype(F32)
        o_ref[i0:i0 + rc, r * dv:(r + 1) * dv] = o_u.astype(o_ref.dtype)


def _band_call(z, q_col, k_col, v_col, *, batch, seq, n_grp, n_rep, d, scale, window, t, gate=None, add=None,
               name="band_attn"):
    dv = d
    n_kt = -(-(window - 1) // t) + 1
    nq = seq // t
    rc = min(t, UNIT_ROWS)

    def q_row(b, i):
        return (b // n_grp) * nq + i

    in_specs = [pl.BlockSpec((t, n_rep * d), lambda b, i: (q_row(b, i), q_col // (n_rep * d) + b % n_grp))]
    args = [z]
    for col in (k_col, v_col):
        for j in range(n_kt):
            in_specs.append(pl.BlockSpec(
                (t, d), lambda b, i, j=j, col=col: ((b // n_grp) * nq + jnp.maximum(i - (n_kt - 1) + j, 0),
                                                     col // d + b % n_grp)))
            args.append(z)
    gate_col = None
    if gate is not None:
        gl, gb, gate_col = gate
        in_specs += [pl.BlockSpec((t, LANE), lambda b, i: (q_row(b, i), b % n_grp)),
                     pl.BlockSpec((1, LANE), lambda b, i: (0, b % n_grp))]
        args += [gl, gb]
    out_spec = pl.BlockSpec((t, n_rep * dv), lambda b, i: (q_row(b, i), b % n_grp))
    if add is not None:
        in_specs.append(out_spec)
        args.append(add)
    kern = functools.partial(_band_kernel, n_rep=n_rep, t=t, n_kt=n_kt, rc=rc, dv=dv, c_exp=scale * LOG2E,
                             window=window, gate_col=gate_col, has_add=add is not None)
    return pl.pallas_call(
        kern,
        grid=(batch * n_grp, nq),
        in_specs=in_specs,
        out_specs=out_spec,
        out_shape=jax.ShapeDtypeStruct((batch * seq, n_grp * n_rep * dv), BF),
        compiler_params=_params("parallel", "parallel"),
        name=name,
    )(*args)


def _compress_kernel(x_ref, pa_ref, pb_ref, w1a_ref, w1b_ref, w2_ref, o_ref):
    x = x_ref[...].astype(F32)
    a = jnp.dot((x + pa_ref[...]).astype(BF), w1a_ref[...], preferred_element_type=F32)
    b = jnp.dot((x + pb_ref[...]).astype(BF), w1b_ref[...], preferred_element_type=F32)
    h = jax.nn.gelu(a + pltpu.roll(b, b.shape[0] - 1, 0))
    o_ref[0] = jnp.dot(h.astype(BF), w2_ref[...], preferred_element_type=F32).astype(o_ref.dtype)


def _compress(zc, first_blk, pos, w1, w2, *, batch, seq, n_grp, name):
    d = w2.shape[1]
    hid = w1.shape[1]
    nch = seq // CMP_STRIDE
    width = CMP_STRIDE * d
    pa = pos[:CMP_STRIDE].reshape(1, width).astype(F32)
    pb = pos[CMP_STRIDE:].reshape(1, width).astype(F32)
    w1 = w1.astype(BF)
    return pl.pallas_call(
        _compress_kernel,
        grid=(batch * n_grp,),
        in_specs=[pl.BlockSpec((nch, width), lambda b: (b // n_grp, first_blk + b % n_grp)),
                  pl.BlockSpec((1, width), lambda b: (0, 0)),
                  pl.BlockSpec((1, width), lambda b: (0, 0)),
                  pl.BlockSpec((width, hid), lambda b: (0, 0)),
                  pl.BlockSpec((width, hid), lambda b: (1, 0)),
                  pl.BlockSpec((hid, d), lambda b: (0, 0))],
        out_specs=pl.BlockSpec((1, nch, d), lambda b: (b, 0, 0)),
        out_shape=jax.ShapeDtypeStruct((batch * n_grp, nch, d), BF),
        compiler_params=_params("parallel"),
        name=name,
    )(zc, pa, pb, w1, w1, w2.astype(BF))


def _cmp_attn_kernel(q_ref, k_ref, v_ref, ovt_ref, gl_ref, gb_ref, o_ref, sel_ref, *, n_rep, tq, c_exp):
    qi = pl.program_id(1)
    nc, d = k_ref.shape[1], k_ref.shape[2]
    rows = n_rep * tq
    q = jnp.concatenate([q_ref[:, r * d:(r + 1) * d] for r in range(n_rep)], axis=0)
    s = lax.dot_general(q, k_ref[0], (((1,), (1,)), ((), ())), preferred_element_type=F32)
    qpos = qi * tq + lax.broadcasted_iota(jnp.int32, (tq, nc), 0)
    cmp_end = lax.broadcasted_iota(jnp.int32, (tq, nc), 1) * CMP_STRIDE + (CMP_BLOCK - 1)
    cmask = (cmp_end <= qpos)[None]
    s = jnp.where(cmask, s.reshape(n_rep, tq, nc), NEG)
    e = jnp.exp2((s - jnp.max(s, -1, keepdims=True)) * c_exp)
    e = jnp.where(cmask, e, 0.0).reshape(rows, nc)
    ext = jnp.dot(e.astype(BF), v_ref[0], preferred_element_type=F32)
    den = ext[:, d:]
    inv = 1.0 / jnp.where(den > 0.0, den, 1.0)
    gates = jax.nn.sigmoid(gl_ref[...] + gb_ref[...])
    o = ext[:, :d] * inv
    for r in range(n_rep):
        o_ref[:, r * d:(r + 1) * d] = (o[r * tq:(r + 1) * tq] * gates[:, 3 * r:3 * r + 1]).astype(o_ref.dtype)
    inv_c = inv[:, :nc] if nc <= d else _lane_tile(inv, nc // d)
    prob = e * inv_c
    psum = jnp.sum(prob.reshape(n_rep, tq, nc), axis=0)
    p_hi = psum.astype(BF)
    p_lo = (psum - p_hi.astype(F32)).astype(BF)
    contract_last = (((1,), (1,)), ((), ()))
    imp = (lax.dot_general(ovt_ref[...], p_hi, contract_last, preferred_element_type=F32)
           + lax.dot_general(ovt_ref[...], p_lo, contract_last, preferred_element_type=F32))
    ns = imp.shape[0]
    blk = lax.broadcasted_iota(jnp.int32, (ns, tq), 0)
    cur = (qi * tq + lax.broadcasted_iota(jnp.int32, (ns, tq), 1)) // SLC_BLOCK
    eligible = blk <= cur
    forced = (blk == 0) | (blk == cur) | (blk == cur - 1)
    score = jnp.where(eligible, jnp.where(forced, FORCE_SCORE, imp), -1.0)
    beaten_by = jnp.zeros((ns, tq), F32)
    for kk in range(ns):
        sk = score[kk:kk + 1, :]
        beats = (sk > score) | ((sk == score) & (blk > kk))
        beaten_by = beaten_by + jnp.where(beats, 1.0, 0.0)
    n_sel = min(N_SEL, ns)
    sel_t = jnp.where((beaten_by < n_sel) & (score >= 0.0), 1.0, 0.0)
    sel_sq = jnp.concatenate([sel_t, jnp.zeros((tq - ns, tq), F32)], axis=0) if ns < tq else sel_t
    sel_ref[0] = sel_sq.T[:, :ns]


def _cmp_attention(z, kcmp, vcmp, gl, gb, *, batch, seq, n_grp, n_rep, scale, tq):
    bg, nc, d = kcmp.shape
    ns = seq // SLC_BLOCK
    nq = seq // tq
    starts = np.arange(nc) * CMP_STRIDE
    jb = np.arange(ns)
    overlap = ((starts[:, None] < (jb[None, :] + 1) * SLC_BLOCK)
               & (starts[:, None] + CMP_BLOCK > jb[None, :] * SLC_BLOCK)).astype(np.float32)
    v_ext = jnp.concatenate([vcmp, jnp.ones_like(vcmp)], -1)
    tok_spec = pl.BlockSpec((tq, n_rep * d), lambda b, i: ((b // n_grp) * nq + i, b % n_grp))
    return pl.pallas_call(
        functools.partial(_cmp_attn_kernel, n_rep=n_rep, tq=tq, c_exp=scale * LOG2E),
        grid=(bg, nq),
        in_specs=[tok_spec,
                  pl.BlockSpec((1, nc, d), lambda b, i: (b, 0, 0)),
                  pl.BlockSpec((1, nc, 2 * d), lambda b, i: (b, 0, 0)),
                  pl.BlockSpec((ns, nc), lambda b, i: (0, 0)),
                  pl.BlockSpec((tq, LANE), lambda b, i: ((b // n_grp) * nq + i, b % n_grp)),
                  pl.BlockSpec((1, LANE), lambda b, i: (0, b % n_grp))],
        out_specs=[tok_spec, pl.BlockSpec((1, tq, ns), lambda b, i: (b, i, 0))],
        out_shape=[jax.ShapeDtypeStruct((batch * seq, n_grp * n_rep * d), BF),
                   jax.ShapeDtypeStruct((bg, seq, ns), F32)],
        compiler_params=_params("parallel", "parallel"),
        name="nsa_cmp_attn",
    )(z, kcmp, v_ext, jnp.asarray(overlap.T, BF), gl, gb)


def _rope_tables(seq, dim):
    inv = 1.0 / (ROPE_THETA ** (jnp.arange(0, dim, 2, dtype=F32) / dim))
    ang = jnp.arange(seq, dtype=F32)[:, None] * inv[None, :]
    cos_h = jnp.concatenate([jnp.cos(ang), jnp.cos(ang)], -1)
    sin_h = jnp.concatenate([-jnp.sin(ang), jnp.sin(ang)], -1)
    return jnp.tile(cos_h, (1, LANE // dim)), jnp.tile(sin_h, (1, LANE // dim))


def _pad_cols(w, n):
    return jnp.pad(w, ((0, 0), (0, n - w.shape[1])))


def _even_mixer(x16, batch, seq, w_in, q_norm, w_uq, kv_norm, w_ukv, sinks):
    H, dn, dr, dv = MLA_HEADS, MLA_NOPE, MLA_ROPE, MLA_V
    G, R, ds = SWA_KV_HEADS, SWA_HEADS // SWA_KV_HEADS, SWA_HEAD_DIM
    n_lat = MLA_Q_RANK + MLA_KV_RANK
    w_in, w_uq, w_ukv = w_in.astype(BF), w_uq.astype(BF), w_ukv.astype(BF)
    o_qs = n_lat + dr
    o_ks = o_qs + SWA_HEADS * ds
    o_vs = o_ks + G * ds
    w_kpe = w_in[:, n_lat:o_qs]
    cols = [w_in[:, o_qs:o_ks]]
    for g in range(G):
        cols += [w_in[:, o_ks + g * ds:o_ks + (g + 1) * ds], w_in[:, o_vs + g * ds:o_vs + (g + 1) * ds]]
    cols += [w_kpe, w_kpe]
    w_rg = jnp.concatenate(cols, -1)
    n_q = SWA_HEADS * ds
    cos, sin = _rope_tables(seq, ds)
    groups = "R" * (n_q // LANE) + "L" * G + "R"
    z, z_lat = _proj(x16, jnp.concatenate([w_rg, w_in[:, :n_lat]], -1),
                     rope=(cos, sin, ds // 2, groups), seq=seq, out_dtype=BF, n_tail=n_lat, name="even_in")

    w_uq_p = jnp.concatenate([w_uq.reshape(-1, H, dn + dr)[:, :, :dn].reshape(-1, H * dn),
                              w_uq.reshape(-1, H, dn + dr)[:, :, dn:].reshape(-1, H * dr)], -1)
    w_ukv_p = jnp.concatenate([w_ukv.reshape(-1, H, dn + dv)[:, :, :dn].reshape(-1, H * dn),
                               w_ukv.reshape(-1, H, dn + dv)[:, :, dn:].reshape(-1, H * dv)], -1)
    cos_q, sin_q = _rope_tables(seq, dr)
    q = _proj(z_lat, w_uq_p, k_block=0, k_size=MLA_Q_RANK, rms_g=q_norm,
              rope=(cos_q, sin_q, dr // 2, "N" * (H * dn // LANE) + "R" * (H * dr // LANE)), seq=seq, out_dtype=BF,
              name="mla_q_up")
    kv = _proj(z_lat, w_ukv_p, k_block=1, k_size=MLA_KV_RANK, rms_g=kv_norm, out_dtype=BF,
               name="mla_kv_up")
    tq = _pick(seq, (512, 256, 128))
    tk = _pick(seq, (512, 256, 128))
    kpe_blk = (n_q + 2 * G * ds) // LANE
    hp = MLA_HEADS_PER_STEP
    n_hg = H // hp
    o_mla = _attn_call(
        "mla", [q, q, kv, z, kv],
        [_spec((tq, hp * dn), "q", lambda b: b % n_hg),
         _spec((tq, hp * dr), "q", lambda b: H * dn // (hp * dr) + b % n_hg),
         _spec((tk, hp * dn), "k", lambda b: b % n_hg),
         _spec((tk, LANE), "k", lambda b: kpe_blk),
         _spec((tk, hp * dv), "k", lambda b: H * dn // (hp * dv) + b % n_hg)],
        batch=batch, seq=seq, n_grp=n_hg, n_rep=hp, dq=2 * LANE, dv=dv, scale=(dn + dr) ** -0.5, window=seq,
        tq=tq, tk=tk, unit_rows=4 * UNIT_ROWS, name="mla_attn")

    tq = _pick(seq, (256, 128))
    o_swa = _attn_call(
        "swa", [z, z],
        [_spec((tq, R * ds), "q", lambda b: b % G),
         _spec((tq, LANE), "k", lambda b: n_q // LANE + b % G)],
        batch=batch, seq=seq, n_grp=G, n_rep=R, dq=LANE, dv=ds, scale=ds ** -0.5, window=SWA_WINDOW,
        tq=tq, tk=tq, sinks=sinks.reshape(G, R), name="swa_attn")
    return [o_mla, o_swa]


def _odd_mixer(x16, batch, seq, w_in, gate_b, cmp_pos_k, cmp_pos_v, k_w1, k_w2, v_w1, v_w2):
    G, d = NSA_KV_GROUPS, NSA_HEAD_DIM
    H = NSA_HEADS
    R = H // G
    kvw = G * d
    w_in = w_in.astype(BF)
    o_kc = H * d
    offs = [o_kc + i * kvw for i in range(7)]
    cos, sin = _rope_tables(seq, d)
    groups = "R" * H + ("R" * G + "N" * G) * 3
    w_gl = jnp.concatenate([_pad_cols(w_in[:, offs[6] + g * 3 * R:offs[6] + (g + 1) * 3 * R], LANE)
                            for g in range(G)], -1)
    cmp_groups = [offs[0] // d + g for g in range(G)] + [offs[1] // d + g for g in range(G)]
    z, gl, zc = _proj(x16, jnp.concatenate([w_in[:, :offs[6]], w_gl], -1),
                      rope=(cos, sin, d // 2, groups), seq=seq, out_dtype=BF, n_tail=G * LANE, chunked=cmp_groups,
                      name="odd_in")
    gb = jnp.concatenate([jnp.pad(gate_b[g * 3 * R:(g + 1) * 3 * R].astype(F32), (0, LANE - 3 * R))
                          for g in range(G)]).reshape(1, G * LANE)

    scale = d ** -0.5
    kcmp = _compress(zc, 0, cmp_pos_k, k_w1, k_w2, batch=batch, seq=seq, n_grp=G, name="nsa_compress_k")
    vcmp = _compress(zc, G, cmp_pos_v, v_w1, v_w2, batch=batch, seq=seq, n_grp=G, name="nsa_compress_v")
    tq = _pick(seq, (256, 128))
    o, sel = _cmp_attention(z, kcmp, vcmp, gl, gb, batch=batch, seq=seq, n_grp=G, n_rep=R, scale=scale, tq=tq)

    def branch(k_off, v_off, **kw):
        return _attn_call(
            "gqa", [z, z, z],
            [_spec((kw["tq"], R * d), "q", lambda b: b % G),
             _spec((kw["tk"], d), "k", lambda b: k_off // d + b % G),
             _spec((kw["tk"], d), "k", lambda b: v_off // d + b % G)],
            batch=batch, seq=seq, n_grp=G, n_rep=R, dq=d, dv=d, scale=scale, **kw)

    o = branch(offs[2], offs[3], window=seq, tq=_pick(seq, (512, 256, 128)), tk=_pick(seq, (512, 256, 128)),
               sel=sel, gate=(gl, gb, 1), add=o, name="nsa_slc_attn")
    o = _band_call(z, 0, offs[4], offs[5], batch=batch, seq=seq, n_grp=G, n_rep=R, d=d, scale=scale,
                   window=NSA_WINDOW, t=tq, gate=(gl, gb, 2), add=o, name="nsa_win_attn")
    return [o]


def kernel(x, ln_g, ln_b, ffn_w_gate, ffn_w_up, ffn_w_down, even_w_in, mla_q_norm, mla_w_uq, mla_kv_norm, mla_w_ukv, swa_sinks, even_w_out, odd_w_in, nsa_gate_b, nsa_cmp_pos_k, nsa_cmp_pos_v, nsa_cmp_k_w1, nsa_cmp_k_w2, nsa_cmp_v_w1, nsa_cmp_v_w2, odd_w_out):
    batch, seq, dm = x.shape
    depth = ln_g.shape[0]
    alpha = float((2 * depth) ** 0.25)
    x32 = x.reshape(batch * seq, dm)
    x16 = x32.astype(BF)
    even_out16 = even_w_out.astype(BF)
    odd_out16 = odd_w_out.astype(BF)

    def ffn(x32, x16, l, idx):
        h, w_down16 = _ffn_up(x16, ffn_w_gate, ffn_w_up, ffn_w_down, l, idx)
        return _mm_res_ln([h], w_down16, (), x32, ln_g[l, 2 * idx], ln_b[l, 2 * idx], alpha=alpha, coef=0.5,
                          name="ffn_down_ln")

    for l in range(depth):
        x32, x16 = ffn(x32, x16, l, 0)
        j = l // 2
        if l % 2 == 0:
            o = _even_mixer(x16, batch, seq, even_w_in[j], mla_q_norm[j], mla_w_uq[j], mla_kv_norm[j],
                            mla_w_ukv[j], swa_sinks[j])
            w_out = even_out16
        else:
            o = _odd_mixer(x16, batch, seq, odd_w_in[j], nsa_gate_b[j], nsa_cmp_pos_k[j], nsa_cmp_pos_v[j],
                           nsa_cmp_k_w1[j], nsa_cmp_k_w2[j], nsa_cmp_v_w1[j], nsa_cmp_v_w2[j])
            w_out = odd_out16
        x32, x16 = _mm_res_ln(o, w_out, (j,), x32, ln_g[l, 1], ln_b[l, 1], alpha=alpha, coef=1.0,
                              name="mixer_out_ln")
        x32, x16 = ffn(x32, x16, l, 1)
    return x32.reshape(batch, seq, dm)
```
